```python
import math
import jax
import jax.numpy as jnp
from jax import lax
import numpy as np

D_MODEL = 1024
BATCH = 2
SEQ = 8192
DEPTH = 1
DEC_BATCH = 16
DEC_SEQ = 64
PAST_LEN = 2048

CHUNK = 64
Q_BLOCK = 128
NORM_EPS = 1e-6
NEG_INF = -1e30
HA = 8
DA = 64
DVA = 2 * DA
WA = HA * DVA
DIFF_EPS = 1e-5
HB = 16
DB = 64
WB = HB * DB
LORA_W = 64
LORA_A = 64
LORA_G = 128
GN_EPS = 64e-5
RWKV_COLS = 3 * WB + LORA_W + LORA_A + LORA_G
W_IN = 3 * WA + RWKV_COLS + 2 * D_MODEL
N_GROUPS = 4
N_PER_GROUP = 8
TOP_K = 2
D_EXPERT = 512

kernel_name = 'hybrid_diffattn_rwkv7_hmoe_stream_step'


def _rmsnorm(x, g, eps=NORM_EPS):
    xf = x.astype(jnp.float32)
    y = xf * lax.rsqrt(jnp.mean(xf * xf, axis=-1, keepdims=True) + eps)
    return y * g.astype(jnp.float32)


def _lambda_init(layer_idx):
    return 0.8 - 0.6 * math.exp(-0.3 * layer_idx)


def _diff_core(q, k, v, lam, mask):
    s = jnp.einsum('bqhmd,bkhmd->bhmqk', q, k) * (DA ** -0.5)
    if mask is not None:
        s = jnp.where(mask, s, NEG_INF)
    p = jax.nn.softmax(s, axis=-1)
    a = p[:, :, 0] - lam * p[:, :, 1]
    return jnp.einsum('bhqk,bkhd->bqhd', a, v)


def _diff_attn_prompt(q, k, v, lam):
    B, T = q.shape[0], q.shape[1]
    n_blk = T // Q_BLOCK
    k_chunk = jnp.arange(T) // CHUNK

    def one_block(i):
        qb = lax.dynamic_slice_in_dim(q, i * Q_BLOCK, Q_BLOCK, axis=1)
        q_chunk = (i * Q_BLOCK + jnp.arange(Q_BLOCK)) // CHUNK
        mask = q_chunk[:, None] >= k_chunk[None, :]
        return _diff_core(qb, k, v, lam, mask)

    o = lax.map(one_block, jnp.arange(n_blk))
    return jnp.moveaxis(o, 0, 1).reshape(B, T, HA, DVA)


def _wkv_step(S, inp):
    r, w, k, v, kk, a = inp
    sa = jnp.einsum('bhvk,bhk->bhv', S, -kk)
    S = S * w[:, :, None, :] + sa[..., None] * (kk * a)[:, :, None, :] + v[..., None] * k[:, :, None, :]
    return S, jnp.einsum('bhvk,bhk->bhv', S, r)


def _rwkv7(pb, prev_row, S0, mu, w0, w2, a0, a2, g2, k_k, k_a, r_k, lnx_w, lnx_b, w_b_out):
    B, T = pb.shape[0], pb.shape[1]
    prev = jnp.concatenate([prev_row, pb[:, :-1]], axis=1)
    xs = pb + (prev - pb) * mu
    r = xs[..., :WB]
    k = xs[..., WB:2 * WB]
    v = xs[..., 2 * WB:3 * WB]
    xw = xs[..., 3 * WB:3 * WB + LORA_W]
    xa = xs[..., 3 * WB + LORA_W:3 * WB + LORA_W + LORA_A]
    xg = xs[..., 3 * WB + LORA_W + LORA_A:]
    w = -jax.nn.softplus(-(w0 + jnp.tanh(xw) @ w2)) - 0.5
    decay = jnp.exp(-jnp.exp(w))
    a = jax.nn.sigmoid(a0 + xa @ a2)
    g = jax.nn.sigmoid(xg) @ g2
    kk = (k * k_k).reshape(B, T, HB, DB)
    kk = kk / jnp.maximum(jnp.sqrt(jnp.sum(kk * kk, axis=-1, keepdims=True)), 1e-12)
    k = k * (1.0 + (a - 1.0) * k_a)
    r_h = r.reshape(B, T, HB, DB)
    k_h = k.reshape(B, T, HB, DB)
    v_h = v.reshape(B, T, HB, DB)
    w_h = decay.reshape(B, T, HB, DB)
    a_h = a.reshape(B, T, HB, DB)
    xs_t = (jnp.moveaxis(r_h, 1, 0), jnp.moveaxis(w_h, 1, 0), jnp.moveaxis(k_h, 1, 0),
            jnp.moveaxis(v_h, 1, 0), jnp.moveaxis(kk, 1, 0), jnp.moveaxis(a_h, 1, 0))
    S_final, y = lax.scan(_wkv_step, S0, xs_t)
    y = jnp.moveaxis(y, 0, 1)
    mean = jnp.mean(y, axis=-1, keepdims=True)
    var = jnp.mean(jnp.square(y - mean), axis=-1, keepdims=True)
    yn = ((y - mean) * lax.rsqrt(var + GN_EPS)).reshape(B, T, WB) * lnx_w + lnx_b
    bonus = jnp.sum(r_h * k_h * r_k.reshape(HB, DB), axis=-1, keepdims=True) * v_h
    out = (yn + bonus.reshape(B, T, WB)) * g
    return out @ w_b_out, pb[:, -1:], S_final


def _hier_moe(h, w_group, b_group, w_router, b_router, w1, w3, w2):
    B, T, D = h.shape
    x = h.reshape(B * T, D)
    lg = (x @ w_group + b_group).astype(jnp.float32)
    pg = jax.nn.softmax(lg, axis=-1)
    g_onehot = jax.nn.one_hot(jnp.argmax(lg, axis=-1), N_GROUPS, dtype=jnp.float32)
    p_sel = jnp.sum(pg * g_onehot, axis=-1)
    le = (x @ w_router + b_router).astype(jnp.float32).reshape(-1, N_GROUPS, N_PER_GROUP)
    le_sel = jnp.einsum('nge,ng->ne', le, g_onehot)
    pe = jax.nn.softmax(le_sel, axis=-1)
    top_v, top_i = lax.top_k(pe, TOP_K)
    top_v = top_v / jnp.sum(top_v, axis=-1, keepdims=True)
    w_exp = jnp.einsum('nk,nke->ne', top_v, jax.nn.one_hot(top_i, N_PER_GROUP, dtype=jnp.float32))
    gate = g_onehot[:, :, None] * (w_exp * p_sel[:, None])[:, None, :]
    y = jnp.zeros((B * T, D), jnp.float32)
    for gi in range(N_GROUPS):
        up = jnp.einsum('nd,edf->nef', x, w1[gi])
        lin = jnp.einsum('nd,edf->nef', x, w3[gi])
        z = jax.nn.silu(up) * lin * gate[:, gi, :, None]
        y = y + jnp.einsum('nef,efd->nd', z, w2[gi])
    return y.reshape(B, T, D)


def setup_inputs(seed: int = 0) -> dict:
    key = jax.random.key(seed)
    ks = jax.random.split(key, 36)
    L = DEPTH

    def nrm(k, shape, scale):
        return jax.random.normal(k, shape, jnp.float32) * scale

    def near_one(k, shape):
        return 1.0 + 0.02 * jax.random.normal(k, shape, jnp.float32)

    return {
        'x_prompt': nrm(ks[0], (BATCH, SEQ, D_MODEL), 1.0),
        'x_sample': nrm(ks[1], (DEC_BATCH, DEC_SEQ, D_MODEL), 1.0),
        'cache_attn_k': nrm(ks[2], (L, DEC_BATCH, PAST_LEN, HA, 2 * DA), 1.0),
        'cache_attn_v': nrm(ks[3], (L, DEC_BATCH, PAST_LEN, HA, DVA), 1.0),
        'state_rwkv_shift': nrm(ks[4], (L, DEC_BATCH, 1, RWKV_COLS), 1.0),
        'state_rwkv_wkv': nrm(ks[5], (L, DEC_BATCH, HB, DB, DB), 0.5),
        'norm_mix': near_one(ks[6], (L, D_MODEL)),
        'w_in': nrm(ks[7], (L, D_MODEL, W_IN), D_MODEL ** -0.5),
        'lambda_q1': nrm(ks[8], (L, DA), 0.1),
        'lambda_k1': nrm(ks[9], (L, DA), 0.1),
        'lambda_q2': nrm(ks[10], (L, DA), 0.1),
        'lambda_k2': nrm(ks[11], (L, DA), 0.1),
        'subln_w': near_one(ks[12], (L, DVA)),
        'w_a_out': nrm(ks[13], (L, WA, D_MODEL), WA ** -0.5),
        'rwkv_mu': jax.random.uniform(ks[14], (L, RWKV_COLS), jnp.float32, 0.0, 1.0),
        'rwkv_w0': 0.5 + jax.random.uniform(ks[15], (L, WB), jnp.float32, -6.0, -2.0),
        'rwkv_w2': nrm(ks[16], (L, LORA_W, WB), 0.1),
        'rwkv_a0': nrm(ks[17], (L, WB), 0.1),
        'rwkv_a2': nrm(ks[18], (L, LORA_A, WB), LORA_A ** -0.5),
        'rwkv_g2': nrm(ks[19], (L, LORA_G, WB), LORA_G ** -0.5),
        'rwkv_k_k': 0.85 + 0.02 * jax.random.normal(ks[20], (L, WB), jnp.float32),
        'rwkv_k_a': near_one(ks[21], (L, WB)),
        'rwkv_r_k': nrm(ks[22], (L, WB), 0.1),
        'rwkv_lnx_w': near_one(ks[23], (L, WB)),
        'rwkv_lnx_b': nrm(ks[24], (L, WB), 0.02),
        'w_b_out': nrm(ks[25], (L, WB, D_MODEL), WB ** -0.5),
        'w_o': nrm(ks[26], (L, D_MODEL, D_MODEL), D_MODEL ** -0.5),
        'norm_ffn': near_one(ks[27], (L, D_MODEL)),
        'moe_w_group': nrm(ks[28], (L, D_MODEL, N_GROUPS), D_MODEL ** -0.5),
        'moe_b_group': nrm(ks[29], (L, N_GROUPS), 0.01),
        'moe_w_router': nrm(ks[30], (L, D_MODEL, N_GROUPS * N_PER_GROUP), D_MODEL ** -0.5),
        'moe_b_router': nrm(ks[31], (L, N_GROUPS * N_PER_GROUP), 0.01),
        'moe_w1': nrm(ks[32], (L, N_GROUPS, N_PER_GROUP, D_MODEL, D_EXPERT), D_MODEL ** -0.5),
        'moe_w3': nrm(ks[33], (L, N_GROUPS, N_PER_GROUP, D_MODEL, D_EXPERT), D_MODEL ** -0.5),
        'moe_w2': nrm(ks[34], (L, N_GROUPS, N_PER_GROUP, D_EXPERT, D_MODEL), D_EXPERT ** -0.5),
        'norm_final': near_one(ks[35], (D_MODEL,)),
    }


def reference(x_prompt, x_sample, cache_attn_k, cache_attn_v, state_rwkv_shift, state_rwkv_wkv,
              norm_mix, w_in, lambda_q1, lambda_k1, lambda_q2, lambda_k2, subln_w, w_a_out,
              rwkv_mu, rwkv_w0, rwkv_w2, rwkv_a0, rwkv_a2, rwkv_g2, rwkv_k_k, rwkv_k_a, rwkv_r_k,
              rwkv_lnx_w, rwkv_lnx_b, w_b_out, w_o, norm_ffn, moe_w_group, moe_b_group,
              moe_w_router, moe_b_router, moe_w1, moe_w3, moe_w2, norm_final):

    def run_layer(l, x, prev_row, S0, ck, cv):
        B, T = x.shape[0], x.shape[1]
        h = _rmsnorm(x, norm_mix[l])
        p = jnp.einsum('btd,dc->btc', h, w_in[l]).astype(jnp.float32)
        q = p[..., :WA].reshape(B, T, HA, 2, DA)
        k = p[..., WA:2 * WA].reshape(B, T, HA, 2, DA)
        v = p[..., 2 * WA:3 * WA].reshape(B, T, HA, DVA)
        pb = p[..., 3 * WA:3 * WA + RWKV_COLS]
        gates = jax.nn.sigmoid(p[..., 3 * WA + RWKV_COLS:])
        lam_init = _lambda_init(l)
        lam = (jnp.exp(jnp.sum(lambda_q1[l].astype(jnp.float32) * lambda_k1[l].astype(jnp.float32)))
               - jnp.exp(jnp.sum(lambda_q2[l].astype(jnp.float32) * lambda_k2[l].astype(jnp.float32)))
               + lam_init)
        if ck is None:
            o = _diff_attn_prompt(q, k, v, lam)
        else:
            past = ck.shape[1]
            k_all = jnp.concatenate([ck.astype(jnp.float32).reshape(B, past, HA, 2, DA), k], axis=1)
            v_all = jnp.concatenate([cv.astype(jnp.float32), v], axis=1)
            o = _diff_core(q, k_all, v_all, lam, None)
        o = _rmsnorm(o, subln_w[l], DIFF_EPS) * (1.0 - lam_init)
        y_a = jnp.einsum('btc,cd->btd', o.reshape(B, T, WA), w_a_out[l])
        y_b, new_prev, S_new = _rwkv7(pb, prev_row, S0, rwkv_mu[l], rwkv_w0[l], rwkv_w2[l],
                                      rwkv_a0[l], rwkv_a2[l], rwkv_g2[l], rwkv_k_k[l], rwkv_k_a[l],
                                      rwkv_r_k[l], rwkv_lnx_w[l], rwkv_lnx_b[l], w_b_out[l])
        merged = gates[..., :D_MODEL] * y_a + gates[..., D_MODEL:] * y_b
        x = x + jnp.einsum('btd,de->bte', merged, w_o[l])
        x = x + _hier_moe(_rmsnorm(x, norm_ffn[l]), moe_w_group[l], moe_b_group[l], moe_w_router[l],
                          moe_b_router[l], moe_w1[l], moe_w3[l], moe_w2[l])
        return x, k.reshape(B, T, HA, 2 * DA), v, new_prev, S_new

    xp = x_prompt.astype(jnp.float32)
    bp = xp.shape[0]
    kp, vp, shp, wkp = [], [], [], []
    for l in range(DEPTH):
        prev0 = jnp.zeros((bp, 1, RWKV_COLS), jnp.float32)
        S00 = jnp.zeros((bp, HB, DB, DB), jnp.float32)
        xp, k_new, v_new, sh_new, S_new = run_layer(l, xp, prev0, S00, None, None)
        kp.append(k_new)
        vp.append(v_new)
        shp.append(sh_new)
        wkp.append(S_new)
    y_prompt = _rmsnorm(xp, norm_final)

    xs = x_sample.astype(jnp.float32)
    ks_, vs_, shs, wks = [], [], [], []
    for l in range(DEPTH):
        xs, k_new, v_new, sh_new, S_new = run_layer(
            l, xs, state_rwkv_shift[l].astype(jnp.float32), state_rwkv_wkv[l].astype(jnp.float32),
            cache_attn_k[l], cache_attn_v[l])
        ks_.append(k_new)
        vs_.append(v_new)
        shs.append(sh_new)
        wks.append(S_new)
    y_sample = _rmsnorm(xs, norm_final)

    new_k_prompt = jnp.stack(kp, axis=0)
    new_v_prompt = jnp.stack(vp, axis=0)
    new_shift_prompt = jnp.stack(shp, axis=0)
    new_wkv_prompt = jnp.stack(wkp, axis=0)
    new_k_sample = jnp.stack(ks_, axis=0)
    new_v_sample = jnp.stack(vs_, axis=0)
    new_shift_sample = jnp.stack(shs, axis=0)
    new_wkv_sample = jnp.stack(wks, axis=0)
    return (y_prompt, y_sample, new_k_prompt, new_v_prompt, new_shift_prompt, new_wkv_prompt,
            new_k_sample, new_v_sample, new_shift_sample, new_wkv_sample)
```

```python
import functools
import math

import jax
import jax.numpy as jnp
from jax import lax
from jax.experimental import pallas as pl
from jax.experimental.pallas import tpu as pltpu

F32 = jnp.float32
BF16 = jnp.bfloat16

LANES = 128
HEAD_B = 64
CHUNK_B = 64
NORM_EPS = 1e-6
DIFF_EPS = 1e-5
GN_EPS = 64e-5
NEG_INF = -1e30
VMEM_LIMIT = 56 * 1024 * 1024

NN = (((1,), (0,)), ((), ()))
NT = (((1,), (1,)), ((), ()))
TN = (((0,), (0,)), ((), ()))


def _dot(a, b, dims=NN):
    return lax.dot_general(a, b, dims, preferred_element_type=F32)


def _split2(a):
    hi = a.astype(BF16)
    lo = (a - hi.astype(F32)).astype(BF16)
    return hi, lo


def _split3(a):
    hi = a.astype(BF16)
    r = a - hi.astype(F32)
    mid = r.astype(BF16)
    lo = (r - mid.astype(F32)).astype(BF16)
    return hi, mid, lo


def _mm(a, b, passes=1, dims=NN):
    if passes == 1:
        return _dot(a.astype(BF16), b.astype(BF16), dims)
    a_hi, a_lo = _split2(a)
    b_hi, b_lo = _split2(b)
    return _dot(a_hi, b_hi, dims) + (_dot(a_hi, b_lo, dims) + _dot(a_lo, b_hi, dims))


def _mm_exact_rhs(a, b_bf16, dims=NN):
    hi, mid, lo = _split3(a)
    return _dot(hi, b_bf16, dims) + (_dot(mid, b_bf16, dims) + _dot(lo, b_bf16, dims))


def _rmsnorm(x, g, eps):
    return x * lax.rsqrt(jnp.mean(x * x, axis=-1, keepdims=True) + eps) * g


def _sigmoid(x):
    return 1.0 / (1.0 + jnp.exp(-x))


def _params(sem):
    return pltpu.CompilerParams(dimension_semantics=sem, vmem_limit_bytes=VMEM_LIMIT)


def _qkv_kernel(x_ref, g_ref, w_ref, qkv_ref, kf_ref, vf_ref, h_ref, *, q_scale):
    j = pl.program_id(1)

    @pl.when(j == 0)
    def _():
        h_ref[...] = _rmsnorm(x_ref[...], g_ref[...], NORM_EPS).astype(BF16)

    p = _dot(h_ref[...], w_ref[...])

    @pl.when(j == 0)
    def _():
        qkv_ref[...] = (p * q_scale).astype(BF16)

    @pl.when(j == 1)
    def _():
        qkv_ref[...] = p.astype(BF16)
        kf_ref[...] = p

    @pl.when(j == 2)
    def _():
        qkv_ref[...] = p.astype(BF16)
        vf_ref[...] = p


def _qkv_proj(x2d, gain, w_qkv, q_scale):
    n, d = x2d.shape
    wa = w_qkv.shape[1] // 3
    tm = min(1024, n)
    return pl.pallas_call(
        functools.partial(_qkv_kernel, q_scale=q_scale),
        grid=(n // tm, 3),
        in_specs=[
            pl.BlockSpec((tm, d), lambda i, j: (i, 0)),
            pl.BlockSpec((1, d), lambda i, j: (0, 0)),
            pl.BlockSpec((d, wa), lambda i, j: (0, j)),
        ],
        out_specs=[
            pl.BlockSpec((tm, wa), lambda i, j: (i, j)),
            pl.BlockSpec((tm, wa), lambda i, j: (i, 0)),
            pl.BlockSpec((tm, wa), lambda i, j: (i, 0)),
        ],
        out_shape=[
            jax.ShapeDtypeStruct((n, 3 * wa), BF16),
            jax.ShapeDtypeStruct((n, wa), F32),
            jax.ShapeDtypeStruct((n, wa), F32),
        ],
        scratch_shapes=[pltpu.VMEM((tm, d), BF16)],
        compiler_params=_params(("parallel", "arbitrary")),
        name="qkv_proj",
    )(x2d, gain, w_qkv)


def _norm_mm_kernel(x_ref, g_ref, w_ref, o_ref, *, act):
    h = _rmsnorm(x_ref[...], g_ref[...], NORM_EPS).astype(BF16)
    p = _dot(h, w_ref[...])
    if act:
        p = _sigmoid(p)
    o_ref[...] = p


def _norm_mm(x2d, gain, w, act, name):
    n, d = x2d.shape
    c = w.shape[1]
    tm = min(512, n)
    return pl.pallas_call(
        functools.partial(_norm_mm_kernel, act=act),
        grid=(n // tm,),
        in_specs=[
            pl.BlockSpec((tm, d), lambda i: (i, 0)),
            pl.BlockSpec((1, d), lambda i: (0, 0)),
            pl.BlockSpec((d, c), lambda i: (0, 0)),
        ],
        out_specs=pl.BlockSpec((tm, c), lambda i: (i, 0)),
        out_shape=jax.ShapeDtypeStruct((n, c), F32),
        compiler_params=_params(("parallel",)),
        name=name,
    )(x2d, gain, w)


def _lambda(lq1_ref, lk1_ref, lq2_ref, lk2_ref, lam_init):
    s1 = jnp.sum(lq1_ref[...] * lk1_ref[...], axis=-1, keepdims=True)
    s2 = jnp.sum(lq2_ref[...] * lk2_ref[...], axis=-1, keepdims=True)
    return jnp.exp(s1) - jnp.exp(s2) + lam_init


def _split_q(q):
    lane = lax.broadcasted_iota(jnp.int32, (1, q.shape[1]), 1)
    first = lane < (q.shape[1] // 2)
    zero = jnp.zeros_like(q)
    return jnp.where(first, q, zero), jnp.where(first, zero, q)


def _softmax_block(s, m, l, acc, v):
    m_new = jnp.maximum(m, jnp.max(s, axis=-1, keepdims=True))
    alpha = jnp.exp(m - m_new)
    p = jnp.exp(s - m_new)
    l_new = alpha * l + jnp.sum(p, axis=-1, keepdims=True)
    acc_new = alpha * acc + _dot(p.astype(BF16), v)
    return m_new, l_new, acc_new


def _diff_finish(acc1, l1, acc2, l2, lam, sub_w, lam_init):
    o = acc1 / l1 - lam * (acc2 / l2)
    return _rmsnorm(o, sub_w, DIFF_EPS) * (1.0 - lam_init)


def _attn_prompt_kernel(lq1_ref, lk1_ref, lq2_ref, lk2_ref, sub_ref, q_ref, k_ref, v_ref, o_ref,
                        *, tq, chunk, lam_init):
    i = pl.program_id(2)
    lam = _lambda(lq1_ref, lk1_ref, lq2_ref, lk2_ref, lam_init)
    q1, q2 = _split_q(q_ref[...])
    dv = v_ref.shape[1]

    def kv_step(j, carry, masked):
        m1, l1, a1, m2, l2, a2 = carry
        start = pl.multiple_of(j * tq, tq)
        ks = k_ref[pl.ds(start, tq), :]
        vs = v_ref[pl.ds(start, tq), :]
        s1 = _dot(q1, ks, NT)
        s2 = _dot(q2, ks, NT)
        if masked:
            rc = lax.broadcasted_iota(jnp.int32, (tq, tq), 0) // chunk
            cc = lax.broadcasted_iota(jnp.int32, (tq, tq), 1) // chunk
            vis = rc >= cc
            s1 = jnp.where(vis, s1, NEG_INF)
            s2 = jnp.where(vis, s2, NEG_INF)
        m1, l1, a1 = _softmax_block(s1, m1, l1, a1, vs)
        m2, l2, a2 = _softmax_block(s2, m2, l2, a2, vs)
        return m1, l1, a1, m2, l2, a2

    neg = jnp.full((tq, 1), NEG_INF, F32)
    zero1 = jnp.zeros((tq, 1), F32)
    zacc = jnp.zeros((tq, dv), F32)
    carry = (neg, zero1, zacc, neg, zero1, zacc)
    carry = lax.fori_loop(0, i, lambda j, c: kv_step(j, c, False), carry)
    m1, l1, a1, m2, l2, a2 = kv_step(i, carry, True)
    o_ref[...] = _diff_finish(a1, l1, a2, l2, lam, sub_ref[...], lam_init).astype(o_ref.dtype)


def _attn_prompt(qkv, lams, sub_w, *, n_heads, chunk, lam_init):
    b, t, c3 = qkv.shape
    wa = c3 // 3
    dv = wa // n_heads
    tq = min(512, t)
    lam_specs = [pl.BlockSpec((1, lams[0].shape[1]), lambda bb, h, i: (0, 0)) for _ in range(4)]
    return pl.pallas_call(
        functools.partial(_attn_prompt_kernel, tq=tq, chunk=chunk, lam_init=lam_init),
        grid=(b, n_heads, t // tq),
        in_specs=lam_specs + [
            pl.BlockSpec((1, dv), lambda bb, h, i: (0, 0)),
            pl.BlockSpec((None, tq, dv), lambda bb, h, i: (bb, i, h)),
            pl.BlockSpec((None, t, dv), lambda bb, h, i: (bb, 0, n_heads + h)),
            pl.BlockSpec((None, t, dv), lambda bb, h, i: (bb, 0, 2 * n_heads + h)),
        ],
        out_specs=pl.BlockSpec((None, tq, dv), lambda bb, h, i: (bb, i, h)),
        out_shape=jax.ShapeDtypeStruct((b, t, wa), BF16),
        compiler_params=_params(("parallel", "parallel", "arbitrary")),
        name="diff_attn_prompt",
    )(*lams, sub_w, qkv, qkv, qkv)


def _attn_sample_kernel(lq1_ref, lk1_ref, lq2_ref, lk2_ref, sub_ref, q_ref, kn_ref, vn_ref, ck_ref, cv_ref,
                        o_ref, *, lam_init):
    lam = _lambda(lq1_ref, lk1_ref, lq2_ref, lk2_ref, lam_init)
    q1, q2 = _split_q(q_ref[...])
    tq = q1.shape[0]
    dv = vn_ref.shape[1]
    ck = ck_ref[...].astype(BF16)
    cv = cv_ref[...].astype(BF16)
    kn = kn_ref[...]
    vn = vn_ref[...]
    outs = []
    for q in (q1, q2):
        m = jnp.full((tq, 1), NEG_INF, F32)
        l = jnp.zeros((tq, 1), F32)
        acc = jnp.zeros((tq, dv), F32)
        m, l, acc = _softmax_block(_dot(q, ck, NT), m, l, acc, cv)
        m, l, acc = _softmax_block(_dot(q, kn, NT), m, l, acc, vn)
        outs.append((acc, l))
    (a1, l1), (a2, l2) = outs
    o_ref[...] = _diff_finish(a1, l1, a2, l2, lam, sub_ref[...], lam_init).astype(o_ref.dtype)


def _attn_sample(qkv, cache_k, cache_v, lams, sub_w, *, n_heads, lam_init):
    b, t, c3 = qkv.shape
    wa = c3 // 3
    dv = wa // n_heads
    past = cache_k.shape[1]
    lam_specs = [pl.BlockSpec((1, lams[0].shape[1]), lambda bb, h: (0, 0)) for _ in range(4)]
    return pl.pallas_call(
        functools.partial(_attn_sample_kernel, lam_init=lam_init),
        grid=(b, n_heads),
        in_specs=lam_specs + [
            pl.BlockSpec((1, dv), lambda bb, h: (0, 0)),
            pl.BlockSpec((None, t, dv), lambda bb, h: (bb, 0, h)),
            pl.BlockSpec((None, t, dv), lambda bb, h: (bb, 0, n_heads + h)),
            pl.BlockSpec((None, t, dv), lambda bb, h: (bb, 0, 2 * n_heads + h)),
            pl.BlockSpec((None, past, dv), lambda bb, h: (bb, 0, h)),
            pl.BlockSpec((None, past, dv), lambda bb, h: (bb, 0, h)),
        ],
        out_specs=pl.BlockSpec((None, t, dv), lambda bb, h: (bb, 0, h)),
        out_shape=jax.ShapeDtypeStruct((b, t, wa), BF16),
        compiler_params=_params(("parallel", "parallel")),
        name="diff_attn_sample",
    )(*lams, sub_w, qkv, qkv, qkv, cache_k, cache_v)


def _seg_ones(n=LANES, seg=HEAD_B):
    r = lax.broadcasted_iota(jnp.int32, (n, n), 0) // seg
    c = lax.broadcasted_iota(jnp.int32, (n, n), 1) // seg
    return jnp.where(r == c, 1.0, 0.0).astype(BF16)


def _rwkv_prep_kernel(pb_ref, prev_ref, mu_ref, w0_ref, w2p_ref, a0_ref, a2p_ref, g2_ref, kk_ref, ka_ref,
                      rk_ref, r_o, k_o, v_o, ld_o, na_o, b_o, g_o, bon_o, carry_ref, *, wb):
    i = pl.program_id(1)

    @pl.when(i == 0)
    def _():
        carry_ref[...] = prev_ref[...]

    pb = pb_ref[...]
    tm = pb.shape[0]
    row = lax.broadcasted_iota(jnp.int32, (tm, 1), 0)
    prev = jnp.where(row == 0, carry_ref[...], pltpu.roll(pb, 1, axis=0))
    carry_ref[...] = pb[tm - 1:tm, :]
    xs = pb + (prev - pb) * mu_ref[...]

    n_slab = wb // LANES
    x_wa = xs[:, 3 * wb:3 * wb + LANES]
    x_g = xs[:, 3 * wb + LANES:3 * wb + 2 * LANES]
    lw = _mm(jnp.tanh(x_wa), w2p_ref[...], 3)
    la = _mm(x_wa, a2p_ref[...], 3)
    g = _mm(_sigmoid(x_g), g2_ref[...], 3)
    seg = _seg_ones()

    for s in range(n_slab):
        sl = slice(s * LANES, (s + 1) * LANES)
        r = xs[:, s * LANES:(s + 1) * LANES]
        k = xs[:, wb + s * LANES:wb + (s + 1) * LANES]
        v = xs[:, 2 * wb + s * LANES:2 * wb + (s + 1) * LANES]
        z = -(w0_ref[:, sl] + lw[:, sl])
        softplus = jnp.maximum(z, 0.0) + jnp.log1p(jnp.exp(-jnp.abs(z)))
        w_log = -softplus - 0.5
        a = _sigmoid(a0_ref[:, sl] + la[:, sl])
        kk = k * kk_ref[:, sl]
        nrm = jnp.sqrt(_mm_exact_rhs(kk * kk, seg))
        kk = kk / jnp.maximum(nrm, 1e-12)
        k2 = k * (1.0 + (a - 1.0) * ka_ref[:, sl])
        bonus = _mm_exact_rhs(r * k2 * rk_ref[:, sl], seg) * v
        r_o[s] = r
        k_o[s] = k2
        v_o[s] = v
        ld_o[s] = -jnp.exp(w_log)
        na_o[s] = -kk
        b_o[s] = kk * a
        g_o[s] = g[:, sl]
        bon_o[s] = bonus


def _rwkv_prep(pb, prev_row, mu, w0, w2p, a0, a2p, g2, k_k, k_a, r_k, *, wb):
    b, t, cols = pb.shape
    tm = min(256, t)
    n_slab = wb // LANES
    vec = lambda n: pl.BlockSpec((1, n), lambda bb, i: (0, 0))
    full = lambda a: pl.BlockSpec(a.shape, lambda bb, i: (0, 0))
    out_spec = pl.BlockSpec((None, n_slab, tm, LANES), lambda bb, i: (bb, 0, i, 0))
    out_sds = jax.ShapeDtypeStruct((b, n_slab, t, LANES), F32)
    return pl.pallas_call(
        functools.partial(_rwkv_prep_kernel, wb=wb),
        grid=(b, t // tm),
        in_specs=[
            pl.BlockSpec((None, tm, cols), lambda bb, i: (bb, i, 0)),
            pl.BlockSpec((None, 1, cols), lambda bb, i: (bb, 0, 0)),
            vec(cols), vec(wb), full(w2p), vec(wb), full(a2p), full(g2), vec(wb), vec(wb), vec(wb),
        ],
        out_specs=[out_spec] * 8,
        out_shape=[out_sds] * 8,
        scratch_shapes=[pltpu.VMEM((1, cols), F32)],
        compiler_params=_params(("parallel", "arbitrary")),
        name="rwkv_prep",
    )(pb, prev_row, mu, w0, w2p, a0, a2p, g2, k_k, k_a, r_k)


def _stack2(x, first):
    zero = jnp.zeros_like(x)
    return jnp.concatenate([jnp.where(first, x, zero), jnp.where(first, zero, x)], axis=0)


def _unit_lower_inverse(a, passes):
    n = a.shape[0]
    eye = jnp.where(lax.broadcasted_iota(jnp.int32, (n, n), 0) == lax.broadcasted_iota(jnp.int32, (n, n), 1),
                    1.0, 0.0).astype(F32)
    inv = eye + a
    p = a
    steps = int(math.log2(CHUNK_B)) - 1
    for _ in range(steps):
        p = _mm(p, p, passes)
        inv = inv + _mm(inv, p, passes)
    return inv


def _rwkv_chunk_kernel(r_ref, k_ref, v_ref, ld_ref, na_ref, b_ref, g_ref, bon_ref, s0_ref, lnw_ref, lnb_ref,
                       y_ref, sT_ref, st_ref, *, passes):
    c = pl.program_id(1)
    n_chunks = pl.num_programs(1)
    n_slab = r_ref.shape[0]
    cl = r_ref.shape[1]

    @pl.when(c == 0)
    def _():
        st_ref[...] = s0_ref[...]

    lane = lax.broadcasted_iota(jnp.int32, (1, LANES), 1)
    first = lane < HEAD_B
    ti = lax.broadcasted_iota(jnp.int32, (cl, cl), 0)
    tj = lax.broadcasted_iota(jnp.int32, (cl, cl), 1)
    tri_incl = jnp.where(ti >= tj, 1.0, 0.0).astype(BF16)
    seg = _seg_ones()
    n2 = 2 * cl
    ri = lax.broadcasted_iota(jnp.int32, (n2, n2), 0) % cl
    ci = lax.broadcasted_iota(jnp.int32, (n2, n2), 1) % cl
    strict = ri > ci
    incl = ri >= ci

    for s in range(n_slab):
        ld = ld_ref[s]
        hi, mid, lo = _split3(ld)
        cum = _dot(tri_incl, hi) + (_dot(tri_incl, mid) + _dot(tri_incl, lo))
        cum_last = cum[cl - 1:cl, :]
        g_incl = jnp.exp(cum)
        g_excl = jnp.exp(cum - ld)
        g_inv = jnp.exp(-cum)
        g_end = jnp.exp(cum_last - cum)
        g_last = jnp.exp(cum_last)

        r = r_ref[s]
        k = k_ref[s]
        v = v_ref[s]
        na = na_ref[s]
        bv = b_ref[s]
        la = _stack2(na * g_excl, first)
        lr = _stack2(r * g_incl, first)
        lb = _stack2(bv * g_inv, first)
        lk = _stack2(k * g_inv, first)
        lbh = _stack2(bv * g_end, first)
        lkh = _stack2(k * g_end, first)
        v2 = _stack2(v, first)

        zero = jnp.zeros((n2, n2), F32)
        g_ab = jnp.where(strict, _mm(la, lb, passes, NT), zero)
        g_ak = jnp.where(strict, _mm(la, lk, passes, NT), zero)
        g_rb = jnp.where(incl, _mm(lr, lb, passes, NT), zero)
        g_rk = jnp.where(incl, _mm(lr, lk, passes, NT), zero)
        inv = _unit_lower_inverse(g_ab, passes)

        st = st_ref[s]
        x = _mm(la, st, passes, NT) + _mm(g_ak, v2, passes)
        u = _mm(inv, x, passes)
        y2 = _mm(lr, st, passes, NT) + _mm(g_rb, u, passes) + _mm(g_rk, v2, passes)
        st_new = st * g_last + _mm(u, lbh, passes, TN) + _mm(v2, lkh, passes, TN)
        st_ref[s] = st_new

        y = y2[:cl, :] + y2[cl:, :]
        inv_n = 1.0 / HEAD_B
        mean = _mm_exact_rhs(y, seg) * inv_n
        d = y - mean
        var = _mm_exact_rhs(d * d, seg) * inv_n
        sl = slice(s * LANES, (s + 1) * LANES)
        yn = d * lax.rsqrt(var + GN_EPS) * lnw_ref[:, sl] + lnb_ref[:, sl]
        y_ref[s] = ((yn + bon_ref[s]) * g_ref[s]).astype(y_ref.dtype)

    @pl.when(c == n_chunks - 1)
    def _():
        sT_ref[...] = st_ref[...]


def _rwkv_chunks(prep, s0_bd, lnx_w, lnx_b, *, passes):
    b, n_slab, t, _ = prep[0].shape
    cl = min(CHUNK_B, t)
    wb = n_slab * LANES
    in_spec = pl.BlockSpec((None, n_slab, cl, LANES), lambda bb, c: (bb, 0, c, 0))
    st_spec = pl.BlockSpec((None, n_slab, LANES, LANES), lambda bb, c: (bb, 0, 0, 0))
    vec = pl.BlockSpec((1, wb), lambda bb, c: (0, 0))
    return pl.pallas_call(
        functools.partial(_rwkv_chunk_kernel, passes=passes),
        grid=(b, t // cl),
        in_specs=[in_spec] * 8 + [st_spec, vec, vec],
        out_specs=[in_spec, st_spec],
        out_shape=[
            jax.ShapeDtypeStruct((b, n_slab, t, LANES), BF16),
            jax.ShapeDtypeStruct((b, n_slab, LANES, LANES), F32),
        ],
        scratch_shapes=[pltpu.VMEM((n_slab, LANES, LANES), F32)],
        compiler_params=_params(("parallel", "arbitrary")),
        name="rwkv_chunks",
    )(*prep, s0_bd, lnx_w, lnx_b)


def _state_to_blockdiag(s):
    b, h, n, _ = s.shape
    s = s.reshape(b, h // 2, 2, n, n)
    z = jnp.zeros_like(s[:, :, 0])
    top = jnp.concatenate([s[:, :, 0], z], axis=-1)
    bot = jnp.concatenate([z, s[:, :, 1]], axis=-1)
    return jnp.concatenate([top, bot], axis=-2)


def _state_from_blockdiag(sbd):
    b, hs, n2, _ = sbd.shape
    n = n2 // 2
    return jnp.stack([sbd[:, :, :n, :n], sbd[:, :, n:, n:]], axis=2).reshape(b, 2 * hs, n, n)


def _route(logits, n_groups, n_per_group):
    ne = n_groups * n_per_group
    lane = lax.broadcasted_iota(jnp.int32, logits.shape, 1)
    big = jnp.int32(1 << 30)
    is_grp = (lane >= ne) & (lane < ne + n_groups)
    lg = jnp.where(is_grp, logits, NEG_INF)
    mg = jnp.max(lg, axis=-1, keepdims=True)
    gi = jnp.min(jnp.where(is_grp & (lg == mg), lane, big), axis=-1, keepdims=True) - ne
    eg = jnp.where(is_grp, jnp.exp(lg - mg), 0.0)
    p_sel = 1.0 / jnp.sum(eg, axis=-1, keepdims=True)
    in_grp = (lane < ne) & ((lane // n_per_group) == gi)
    le = jnp.where(in_grp, logits, NEG_INF)
    me = jnp.max(le, axis=-1, keepdims=True)
    ee = jnp.where(in_grp, jnp.exp(le - me), 0.0)
    pe = ee / jnp.sum(ee, axis=-1, keepdims=True)
    v1 = jnp.max(jnp.where(in_grp, pe, -1.0), axis=-1, keepdims=True)
    i1 = jnp.min(jnp.where(in_grp & (pe == v1), lane, big), axis=-1, keepdims=True)
    rest = in_grp & (lane != i1)
    v2 = jnp.max(jnp.where(rest, pe, -1.0), axis=-1, keepdims=True)
    i2 = jnp.min(jnp.where(rest & (pe == v2), lane, big), axis=-1, keepdims=True)
    tot = v1 + v2
    return jnp.where(lane == i1, (v1 / tot) * p_sel, 0.0) + jnp.where(lane == i2, (v2 / tot) * p_sel, 0.0)


def _mix_kernel(x_ref, oa_ref, ob_ref, gt_ref, wa_ref, wb_ref, wo_ref, gn_ref, wr_ref, br_ref,
                x2_ref, hn_ref, gate_ref, *, n_groups, n_per_group):
    d = x_ref.shape[1]
    n_slab = ob_ref.shape[0]
    y_a = _dot(oa_ref[...], wa_ref[...])
    ob = jnp.concatenate([ob_ref[s] for s in range(n_slab)], axis=1)
    y_b = _dot(ob, wb_ref[...])
    merged = gt_ref[:, :d] * y_a + gt_ref[:, d:] * y_b
    x2 = x_ref[...] + _dot(merged.astype(BF16), wo_ref[...])
    x2_ref[...] = x2
    hn = _rmsnorm(x2, gn_ref[...], NORM_EPS)
    hn_ref[...] = hn.astype(BF16)
    logits = _mm(hn, wr_ref[...], 3) + br_ref[...]
    gate_ref[...] = _route(logits, n_groups, n_per_group)


def _mix(x2d, o_a, o_b, gates, w_a, w_b, w_o, gain_ffn, w_route, b_route, *, n_groups, n_per_group):
    b, n_slab, t, _ = o_b.shape
    n, d = x2d.shape
    tm = min(512, t)
    nt = t // tm
    full = lambda a: pl.BlockSpec(a.shape, lambda i: (0,) * a.ndim)
    return pl.pallas_call(
        functools.partial(_mix_kernel, n_groups=n_groups, n_per_group=n_per_group),
        grid=(n // tm,),
        in_specs=[
            pl.BlockSpec((tm, d), lambda i: (i, 0)),
            pl.BlockSpec((tm, o_a.shape[1]), lambda i: (i, 0)),
            pl.BlockSpec((None, n_slab, tm, LANES), lambda i: (i // nt, 0, i % nt, 0)),
            pl.BlockSpec((tm, 2 * d), lambda i: (i, 0)),
            full(w_a), full(w_b), full(w_o), full(gain_ffn), full(w_route), full(b_route),
        ],
        out_specs=[
            pl.BlockSpec((tm, d), lambda i: (i, 0)),
            pl.BlockSpec((tm, d), lambda i: (i, 0)),
            pl.BlockSpec((tm, LANES), lambda i: (i, 0)),
        ],
        out_shape=[
            jax.ShapeDtypeStruct((n, d), F32),
            jax.ShapeDtypeStruct((n, d), BF16),
            jax.ShapeDtypeStruct((n, LANES), F32),
        ],
        compiler_params=_params(("parallel",)),
        name="mix_route",
    )(x2d, o_a, o_b, gates, w_a, w_b, w_o, gain_ffn, w_route, b_route)


def _moe_kernel(hn_ref, gate_ref, w1_ref, w3_ref, w2_ref, x2_ref, gf_ref, y_ref, acc_ref):
    e = pl.program_id(1)

    @pl.when(e == 0)
    def _():
        acc_ref[...] = jnp.zeros_like(acc_ref)

    gate = gate_ref[...]
    lane = lax.broadcasted_iota(jnp.int32, gate.shape, 1)
    ge = jnp.sum(jnp.where(lane == e, gate, 0.0), axis=-1, keepdims=True)
    hn = hn_ref[...]
    up = _dot(hn, w1_ref[...])
    lin = _dot(hn, w3_ref[...])
    z = (up * _sigmoid(up)) * lin * ge
    acc_ref[...] += _dot(z.astype(BF16), w2_ref[...])

    @pl.when(e == pl.num_programs(1) - 1)
    def _():
        y_ref[...] = _rmsnorm(x2_ref[...] + acc_ref[...], gf_ref[...], NORM_EPS)


def _moe(hn, gate, w1, w3, w2, x2, gain_final):
    n, d = hn.shape
    ne, _, f = w1.shape
    tm = min(1024, n)
    return pl.pallas_call(
        _moe_kernel,
        grid=(n // tm, ne),
        in_specs=[
            pl.BlockSpec((tm, d), lambda i, e: (i, 0)),
            pl.BlockSpec((tm, LANES), lambda i, e: (i, 0)),
            pl.BlockSpec((None, d, f), lambda i, e: (e, 0, 0)),
            pl.BlockSpec((None, d, f), lambda i, e: (e, 0, 0)),
            pl.BlockSpec((None, f, d), lambda i, e: (e, 0, 0)),
            pl.BlockSpec((tm, d), lambda i, e: (i, 0)),
            pl.BlockSpec((1, d), lambda i, e: (0, 0)),
        ],
        out_specs=pl.BlockSpec((tm, d), lambda i, e: (i, 0)),
        out_shape=jax.ShapeDtypeStruct((n, d), F32),
        scratch_shapes=[pltpu.VMEM((tm, d), F32)],
        compiler_params=_params(("parallel", "arbitrary")),
        name="moe_experts",
    )(hn, gate, w1, w3, w2, x2, gain_final)


RWKV_PASSES = 3


def _run_path(x, prev_row, s0, cache, wts, dims):
    b, t, d = x.shape
    x2d = x.reshape(b * t, d)
    qkv, kf, vf = _qkv_proj(x2d, wts["norm_mix"], wts["w_qkv"], dims["da"] ** -0.5)
    pb = _norm_mm(x2d, wts["norm_mix"], wts["w_pb"], False, "rwkv_proj")
    gates = _norm_mm(x2d, wts["norm_mix"], wts["w_gates"], True, "gate_proj")

    qkv3 = qkv.reshape(b, t, -1)
    if cache is None:
        o_a = _attn_prompt(qkv3, wts["lams"], wts["subln_w"], n_heads=dims["ha"], chunk=dims["chunk"],
                           lam_init=dims["lam_init"])
    else:
        o_a = _attn_sample(qkv3, cache[0], cache[1], wts["lams"], wts["subln_w"], n_heads=dims["ha"],
                           lam_init=dims["lam_init"])

    pb3 = pb.reshape(b, t, -1)
    prep = _rwkv_prep(pb3, prev_row, wts["mu"], wts["w0"], wts["w2p"], wts["a0"], wts["a2p"], wts["g2"],
                      wts["k_k"], wts["k_a"], wts["r_k"], wb=dims["wb"])
    o_b, s_new = _rwkv_chunks(prep, _state_to_blockdiag(s0), wts["lnx_w"], wts["lnx_b"], passes=RWKV_PASSES)

    x2, hn, gate = _mix(x2d, o_a.reshape(b * t, -1), o_b, gates, wts["w_a_out"], wts["w_b_out"], wts["w_o"],
                        wts["norm_ffn"], wts["w_route"], wts["b_route"], n_groups=dims["n_groups"],
                        n_per_group=dims["n_per_group"])
    y = _moe(hn, gate, wts["moe_w1"], wts["moe_w3"], wts["moe_w2"], x2, wts["norm_final"])

    ha = dims["ha"]
    return (y.reshape(b, t, d), kf.reshape(1, b, t, ha, -1), vf.reshape(1, b, t, ha, -1),
            pb3[:, t - 1:t, :][None], _state_from_blockdiag(s_new)[None])


def kernel(x_prompt, x_sample, cache_attn_k, cache_attn_v, state_rwkv_shift, state_rwkv_wkv, norm_mix, w_in, lambda_q1, lambda_k1, lambda_q2, lambda_k2, subln_w, w_a_out, rwkv_mu, rwkv_w0, rwkv_w2, rwkv_a0, rwkv_a2, rwkv_g2, rwkv_k_k, rwkv_k_a, rwkv_r_k, rwkv_lnx_w, rwkv_lnx_b, w_b_out, w_o, norm_ffn, moe_w_group, moe_b_group, moe_w_router, moe_b_router, moe_w1, moe_w3, moe_w2, norm_final):
    assert w_in.shape[0] == 1, "single-layer trunk"
    l = 0
    d = x_prompt.shape[-1]
    ha, dva = cache_attn_k.shape[3], cache_attn_v.shape[4]
    wa = ha * dva
    hb, db = state_rwkv_wkv.shape[2], state_rwkv_wkv.shape[3]
    wb = hb * db
    lora_w, lora_a, lora_g = rwkv_w2.shape[1], rwkv_a2.shape[1], rwkv_g2.shape[1]
    rwkv_cols = 3 * wb + lora_w + lora_a + lora_g
    n_groups = moe_w_group.shape[-1]
    n_per_group = moe_w_router.shape[-1] // n_groups
    ne = n_groups * n_per_group
    assert db == HEAD_B and lora_w + lora_a == LANES and lora_g == LANES and dva == LANES
    assert ne + n_groups <= LANES

    row = lambda v: v.reshape(1, -1).astype(F32)
    w_l = w_in[l]
    zeros_w = jnp.zeros((lora_a, wb), F32)
    zeros_a = jnp.zeros((lora_w, wb), F32)
    w_route = jnp.zeros((d, LANES), F32)
    w_route = w_route.at[:, :ne].set(moe_w_router[l]).at[:, ne:ne + n_groups].set(moe_w_group[l])
    b_route = jnp.zeros((1, LANES), F32)
    b_route = b_route.at[0, :ne].set(moe_b_router[l]).at[0, ne:ne + n_groups].set(moe_b_group[l])
    f = moe_w1.shape[-1]
    wts = dict(
        norm_mix=row(norm_mix[l]),
        w_qkv=w_l[:, :3 * wa].astype(BF16),
        w_pb=w_l[:, 3 * wa:3 * wa + rwkv_cols].astype(BF16),
        w_gates=w_l[:, 3 * wa + rwkv_cols:].astype(BF16),
        lams=[row(lambda_q1[l]), row(lambda_k1[l]), row(lambda_q2[l]), row(lambda_k2[l])],
        subln_w=row(subln_w[l]),
        mu=row(rwkv_mu[l]), w0=row(rwkv_w0[l]), a0=row(rwkv_a0[l]),
        w2p=jnp.concatenate([rwkv_w2[l], zeros_w], axis=0),
        a2p=jnp.concatenate([zeros_a, rwkv_a2[l]], axis=0),
        g2=rwkv_g2[l].astype(F32),
        k_k=row(rwkv_k_k[l]), k_a=row(rwkv_k_a[l]), r_k=row(rwkv_r_k[l]),
        lnx_w=row(rwkv_lnx_w[l]), lnx_b=row(rwkv_lnx_b[l]),
        w_a_out=w_a_out[l].astype(BF16), w_b_out=w_b_out[l].astype(BF16), w_o=w_o[l].astype(BF16),
        norm_ffn=row(norm_ffn[l]), w_route=w_route, b_route=b_route,
        moe_w1=moe_w1[l].reshape(ne, d, f).astype(BF16),
        moe_w3=moe_w3[l].reshape(ne, d, f).astype(BF16),
        moe_w2=moe_w2[l].reshape(ne, f, d).astype(BF16),
        norm_final=row(norm_final),
    )
    dims = dict(ha=ha, da=dva // 2, wb=wb, chunk=CHUNK_B, lam_init=0.8 - 0.6 * math.exp(-0.3 * l),
                n_groups=n_groups, n_per_group=n_per_group)

    bp = x_prompt.shape[0]
    yp, kp, vp, shp, wkp = _run_path(
        x_prompt.astype(F32), jnp.zeros((bp, 1, rwkv_cols), F32), jnp.zeros((bp, hb, db, db), F32), None, wts, dims)

    bs, past = cache_attn_k.shape[1], cache_attn_k.shape[2]
    cache = (cache_attn_k[l].reshape(bs, past, wa), cache_attn_v[l].reshape(bs, past, wa))
    ys, ks_, vs_, shs, wks = _run_path(
        x_sample.astype(F32), state_rwkv_shift[l].astype(F32), state_rwkv_wkv[l].astype(F32), cache, wts, dims)

    return (yp, ys, kp, vp, shp, wkp, ks_, vs_, shs, wks)
```

```python
import functools
import math

import jax
import jax.numpy as jnp
from jax import lax
from jax.experimental import pallas as pl
from jax.experimental.pallas import tpu as pltpu

F32 = jnp.float32
BF16 = jnp.bfloat16

LANES = 128
HEAD_B = 64
CHUNK_B = 64
CHUNKS_PER_STEP = 2
ROUTE_ROWS = 8
EXPERT_TILE = 256
NORM_EPS = 1e-6
DIFF_EPS = 1e-5
GN_EPS = 64e-5
NEG_INF = -1e30
VMEM_LIMIT = 56 * 1024 * 1024

NN = (((1,), (0,)), ((), ()))
NT = (((1,), (1,)), ((), ()))
TN = (((0,), (0,)), ((), ()))


def _dot(a, b, dims=NN):
    return lax.dot_general(a, b, dims, preferred_element_type=F32)


def _split2(a):
    hi = a.astype(BF16)
    lo = (a - hi.astype(F32)).astype(BF16)
    return hi, lo


def _split3(a):
    hi = a.astype(BF16)
    r = a - hi.astype(F32)
    mid = r.astype(BF16)
    lo = (r - mid.astype(F32)).astype(BF16)
    return hi, mid, lo


def _mm(a, b, passes=1, dims=NN):
    if passes == 1:
        return _dot(a.astype(BF16), b.astype(BF16), dims)
    a_hi, a_lo = _split2(a)
    b_hi, b_lo = _split2(b)
    return _dot(a_hi, b_hi, dims) + (_dot(a_hi, b_lo, dims) + _dot(a_lo, b_hi, dims))


def _mm_exact_rhs(a, b_bf16, dims=NN):
    hi, mid, lo = _split3(a)
    return _dot(hi, b_bf16, dims) + (_dot(mid, b_bf16, dims) + _dot(lo, b_bf16, dims))


def _mm_exact_rhs2(a, b_bf16, dims=NN):
    hi, lo = _split2(a)
    return _dot(hi, b_bf16, dims) + _dot(lo, b_bf16, dims)


def _rmsnorm(x, g, eps):
    return x * lax.rsqrt(jnp.mean(x * x, axis=-1, keepdims=True) + eps) * g


def _sigmoid(x):
    return 1.0 / (1.0 + jnp.exp(-x))


def _params(sem):
    return pltpu.CompilerParams(dimension_semantics=sem, vmem_limit_bytes=VMEM_LIMIT)


def _qkv_kernel(x_ref, g_ref, w_ref, qkv_ref, kf_ref, vf_ref, h_ref, *, q_scale):
    j = pl.program_id(1)

    @pl.when(j == 0)
    def _():
        h_ref[...] = _rmsnorm(x_ref[...], g_ref[...], NORM_EPS).astype(BF16)

    p = _dot(h_ref[...], w_ref[...])

    @pl.when(j == 0)
    def _():
        qkv_ref[...] = (p * q_scale).astype(BF16)

    @pl.when(j == 1)
    def _():
        qkv_ref[...] = p.astype(BF16)
        kf_ref[...] = p

    @pl.when(j == 2)
    def _():
        qkv_ref[...] = p.astype(BF16)
        vf_ref[...] = p


def _qkv_proj(x2d, gain, w_qkv, q_scale):
    n, d = x2d.shape
    wa = w_qkv.shape[1] // 3
    tm = min(1024, n)
    return pl.pallas_call(
        functools.partial(_qkv_kernel, q_scale=q_scale),
        grid=(n // tm, 3),
        in_specs=[
            pl.BlockSpec((tm, d), lambda i, j: (i, 0)),
            pl.BlockSpec((1, d), lambda i, j: (0, 0)),
            pl.BlockSpec((d, wa), lambda i, j: (0, j)),
        ],
        out_specs=[
            pl.BlockSpec((tm, wa), lambda i, j: (i, j)),
            pl.BlockSpec((tm, wa), lambda i, j: (i, 0)),
            pl.BlockSpec((tm, wa), lambda i, j: (i, 0)),
        ],
        out_shape=[
            jax.ShapeDtypeStruct((n, 3 * wa), BF16),
            jax.ShapeDtypeStruct((n, wa), F32),
            jax.ShapeDtypeStruct((n, wa), F32),
        ],
        scratch_shapes=[pltpu.VMEM((tm, d), BF16)],
        compiler_params=_params(("parallel", "arbitrary")),
        name="qkv_proj",
    )(x2d, gain, w_qkv)


def _norm_mm_kernel(x_ref, g_ref, w_ref, o_ref, *, act):
    h = _rmsnorm(x_ref[...], g_ref[...], NORM_EPS).astype(BF16)
    p = _dot(h, w_ref[...])
    if act:
        p = _sigmoid(p)
    o_ref[...] = p


def _norm_mm(x2d, gain, w, act, name):
    n, d = x2d.shape
    c = w.shape[1]
    tm = min(512, n)
    return pl.pallas_call(
        functools.partial(_norm_mm_kernel, act=act),
        grid=(n // tm,),
        in_specs=[
            pl.BlockSpec((tm, d), lambda i: (i, 0)),
            pl.BlockSpec((1, d), lambda i: (0, 0)),
            pl.BlockSpec((d, c), lambda i: (0, 0)),
        ],
        out_specs=pl.BlockSpec((tm, c), lambda i: (i, 0)),
        out_shape=jax.ShapeDtypeStruct((n, c), F32),
        compiler_params=_params(("parallel",)),
        name=name,
    )(x2d, gain, w)


def _lambda(lq1_ref, lk1_ref, lq2_ref, lk2_ref, lam_init):
    s1 = jnp.sum(lq1_ref[...] * lk1_ref[...], axis=-1, keepdims=True)
    s2 = jnp.sum(lq2_ref[...] * lk2_ref[...], axis=-1, keepdims=True)
    return jnp.exp(s1) - jnp.exp(s2) + lam_init


def _split_q(q):
    lane = lax.broadcasted_iota(jnp.int32, (1, q.shape[1]), 1)
    first = lane < (q.shape[1] // 2)
    zero = jnp.zeros_like(q)
    return jnp.where(first, q, zero), jnp.where(first, zero, q)


def _softmax_block(s, m, l, acc, v):
    m_new = jnp.maximum(m, jnp.max(s, axis=-1, keepdims=True))
    alpha = jnp.exp(m - m_new)
    p = jnp.exp(s - m_new)
    l_new = alpha * l + jnp.sum(p, axis=-1, keepdims=True)
    acc_new = alpha * acc + _dot(p.astype(BF16), v)
    return m_new, l_new, acc_new


def _diff_finish(acc1, l1, acc2, l2, lam, sub_w, lam_init):
    o = acc1 / l1 - lam * (acc2 / l2)
    return _rmsnorm(o, sub_w, DIFF_EPS) * (1.0 - lam_init)


def _attn_prompt_kernel(lq1_ref, lk1_ref, lq2_ref, lk2_ref, sub_ref, q_ref, k_ref, v_ref, o_ref,
                        *, tq, chunk, lam_init):
    i = pl.program_id(2)
    lam = _lambda(lq1_ref, lk1_ref, lq2_ref, lk2_ref, lam_init)
    q1, q2 = _split_q(q_ref[...])
    dv = v_ref.shape[1]

    def kv_step(j, carry, masked):
        m1, l1, a1, m2, l2, a2 = carry
        start = pl.multiple_of(j * tq, tq)
        ks = k_ref[pl.ds(start, tq), :]
        vs = v_ref[pl.ds(start, tq), :]
        s1 = _dot(q1, ks, NT)
        s2 = _dot(q2, ks, NT)
        if masked:
            rc = lax.broadcasted_iota(jnp.int32, (tq, tq), 0) // chunk
            cc = lax.broadcasted_iota(jnp.int32, (tq, tq), 1) // chunk
            vis = rc >= cc
            s1 = jnp.where(vis, s1, NEG_INF)
            s2 = jnp.where(vis, s2, NEG_INF)
        m1, l1, a1 = _softmax_block(s1, m1, l1, a1, vs)
        m2, l2, a2 = _softmax_block(s2, m2, l2, a2, vs)
        return m1, l1, a1, m2, l2, a2

    neg = jnp.full((tq, 1), NEG_INF, F32)
    zero1 = jnp.zeros((tq, 1), F32)
    zacc = jnp.zeros((tq, dv), F32)
    carry = (neg, zero1, zacc, neg, zero1, zacc)
    carry = lax.fori_loop(0, i, lambda j, c: kv_step(j, c, False), carry)
    m1, l1, a1, m2, l2, a2 = kv_step(i, carry, True)
    o_ref[...] = _diff_finish(a1, l1, a2, l2, lam, sub_ref[...], lam_init).astype(o_ref.dtype)


def _attn_prompt(qkv, lams, sub_w, *, n_heads, chunk, lam_init):
    b, t, c3 = qkv.shape
    wa = c3 // 3
    dv = wa // n_heads
    tq = min(512, t)
    lam_specs = [pl.BlockSpec((1, lams[0].shape[1]), lambda bb, h, i: (0, 0)) for _ in range(4)]
    return pl.pallas_call(
        functools.partial(_attn_prompt_kernel, tq=tq, chunk=chunk, lam_init=lam_init),
        grid=(b, n_heads, t // tq),
        in_specs=lam_specs + [
            pl.BlockSpec((1, dv), lambda bb, h, i: (0, 0)),
            pl.BlockSpec((None, tq, dv), lambda bb, h, i: (bb, i, h)),
            pl.BlockSpec((None, t, dv), lambda bb, h, i: (bb, 0, n_heads + h)),
            pl.BlockSpec((None, t, dv), lambda bb, h, i: (bb, 0, 2 * n_heads + h)),
        ],
        out_specs=pl.BlockSpec((None, tq, dv), lambda bb, h, i: (bb, i, h)),
        out_shape=jax.ShapeDtypeStruct((b, t, wa), BF16),
        compiler_params=_params(("parallel", "parallel", "arbitrary")),
        name="diff_attn_prompt",
    )(*lams, sub_w, qkv, qkv, qkv)


def _attn_sample_kernel(lq1_ref, lk1_ref, lq2_ref, lk2_ref, sub_ref, q_ref, kn_ref, vn_ref, ck_ref, cv_ref,
                        o_ref, *, n_heads, lam_init):
    lam = _lambda(lq1_ref, lk1_ref, lq2_ref, lk2_ref, lam_init)
    tq = q_ref.shape[0]
    dv = q_ref.shape[1] // n_heads
    past = ck_ref.shape[0] // n_heads
    for h in range(n_heads):
        cols = slice(h * dv, (h + 1) * dv)
        q1, q2 = _split_q(q_ref[:, cols])
        ck = ck_ref[pl.ds(h, past, stride=n_heads), :].astype(BF16)
        cv = cv_ref[pl.ds(h, past, stride=n_heads), :].astype(BF16)
        kn = kn_ref[:, cols]
        vn = vn_ref[:, cols]
        outs = []
        for q in (q1, q2):
            m = jnp.full((tq, 1), NEG_INF, F32)
            l = jnp.zeros((tq, 1), F32)
            acc = jnp.zeros((tq, dv), F32)
            m, l, acc = _softmax_block(_dot(q, ck, NT), m, l, acc, cv)
            m, l, acc = _softmax_block(_dot(q, kn, NT), m, l, acc, vn)
            outs.append((acc, l))
        (a1, l1), (a2, l2) = outs
        o_ref[:, cols] = _diff_finish(a1, l1, a2, l2, lam, sub_ref[...], lam_init).astype(o_ref.dtype)


def _attn_sample(qkv, cache_k, cache_v, lams, sub_w, *, n_heads, lam_init):
    b, t, c3 = qkv.shape
    wa = c3 // 3
    dv = wa // n_heads
    rows = cache_k.shape[1]
    lam_specs = [pl.BlockSpec((1, lams[0].shape[1]), lambda bb: (0, 0)) for _ in range(4)]
    return pl.pallas_call(
        functools.partial(_attn_sample_kernel, n_heads=n_heads, lam_init=lam_init),
        grid=(b,),
        in_specs=lam_specs + [
            pl.BlockSpec((1, dv), lambda bb: (0, 0)),
            pl.BlockSpec((None, t, wa), lambda bb: (bb, 0, 0)),
            pl.BlockSpec((None, t, wa), lambda bb: (bb, 0, 1)),
            pl.BlockSpec((None, t, wa), lambda bb: (bb, 0, 2)),
            pl.BlockSpec((None, rows, dv), lambda bb: (bb, 0, 0)),
            pl.BlockSpec((None, rows, dv), lambda bb: (bb, 0, 0)),
        ],
        out_specs=pl.BlockSpec((None, t, wa), lambda bb: (bb, 0, 0)),
        out_shape=jax.ShapeDtypeStruct((b, t, wa), BF16),
        compiler_params=_params(("parallel",)),
        name="diff_attn_sample",
    )(*lams, sub_w, qkv, qkv, qkv, cache_k, cache_v)


def _seg_ones(n=LANES, seg=HEAD_B):
    r = lax.broadcasted_iota(jnp.int32, (n, n), 0) // seg
    c = lax.broadcasted_iota(jnp.int32, (n, n), 1) // seg
    return jnp.where(r == c, 1.0, 0.0).astype(BF16)


def _rwkv_prep_kernel(pb_ref, prev_ref, mu_ref, w0_ref, w2p_ref, a0_ref, a2p_ref, g2_ref, kk_ref, ka_ref,
                      rk_ref, r_o, k_o, v_o, ld_o, na_o, b_o, g_o, bon_o, carry_ref, *, wb):
    i = pl.program_id(1)

    @pl.when(i == 0)
    def _():
        carry_ref[...] = prev_ref[...]

    pb = pb_ref[...]
    tm = pb.shape[0]
    row = lax.broadcasted_iota(jnp.int32, (tm, 1), 0)
    prev = jnp.where(row == 0, carry_ref[...], pltpu.roll(pb, 1, axis=0))
    carry_ref[...] = pb[tm - 1:tm, :]
    xs = pb + (prev - pb) * mu_ref[...]

    n_slab = wb // LANES
    x_wa = xs[:, 3 * wb:3 * wb + LANES]
    x_g = xs[:, 3 * wb + LANES:3 * wb + 2 * LANES]
    lw = _mm(jnp.tanh(x_wa), w2p_ref[...], 3)
    la = _mm(x_wa, a2p_ref[...], 3)
    g = _mm(_sigmoid(x_g), g2_ref[...], 3)
    seg = _seg_ones()

    for s in range(n_slab):
        sl = slice(s * LANES, (s + 1) * LANES)
        r = xs[:, s * LANES:(s + 1) * LANES]
        k = xs[:, wb + s * LANES:wb + (s + 1) * LANES]
        v = xs[:, 2 * wb + s * LANES:2 * wb + (s + 1) * LANES]
        z = -(w0_ref[:, sl] + lw[:, sl])
        softplus = jnp.maximum(z, 0.0) + jnp.log1p(jnp.exp(-jnp.abs(z)))
        w_log = -softplus - 0.5
        a = _sigmoid(a0_ref[:, sl] + la[:, sl])
        kk = k * kk_ref[:, sl]
        nrm = jnp.sqrt(_mm_exact_rhs(kk * kk, seg))
        kk = kk / jnp.maximum(nrm, 1e-12)
        k2 = k * (1.0 + (a - 1.0) * ka_ref[:, sl])
        bonus = _mm_exact_rhs(r * k2 * rk_ref[:, sl], seg) * v
        r_o[s] = r
        k_o[s] = k2
        v_o[s] = v
        ld_o[s] = -jnp.exp(w_log)
        na_o[s] = -kk
        b_o[s] = kk * a
        g_o[s] = g[:, sl]
        bon_o[s] = bonus


def _rwkv_prep(pb, prev_row, mu, w0, w2p, a0, a2p, g2, k_k, k_a, r_k, *, wb):
    b, t, cols = pb.shape
    tm = min(256, t)
    n_slab = wb // LANES
    vec = lambda n: pl.BlockSpec((1, n), lambda bb, i: (0, 0))
    full = lambda a: pl.BlockSpec(a.shape, lambda bb, i: (0, 0))
    out_spec = pl.BlockSpec((None, n_slab, tm, LANES), lambda bb, i: (bb, 0, i, 0))
    out_sds = jax.ShapeDtypeStruct((b, n_slab, t, LANES), F32)
    return pl.pallas_call(
        functools.partial(_rwkv_prep_kernel, wb=wb),
        grid=(b, t // tm),
        in_specs=[
            pl.BlockSpec((None, tm, cols), lambda bb, i: (bb, i, 0)),
            pl.BlockSpec((None, 1, cols), lambda bb, i: (bb, 0, 0)),
            vec(cols), vec(wb), full(w2p), vec(wb), full(a2p), full(g2), vec(wb), vec(wb), vec(wb),
        ],
        out_specs=[out_spec] * 8,
        out_shape=[out_sds] * 8,
        scratch_shapes=[pltpu.VMEM((1, cols), F32)],
        compiler_params=_params(("parallel", "arbitrary")),
        name="rwkv_prep",
    )(pb, prev_row, mu, w0, w2p, a0, a2p, g2, k_k, k_a, r_k)


def _stack2(x, first):
    xb = x.astype(BF16)
    zero = jnp.zeros_like(xb)
    return jnp.concatenate([jnp.where(first, xb, zero), jnp.where(first, zero, xb)], axis=0)


def _rwkv_chunk_kernel(r_ref, k_ref, v_ref, ld_ref, na_ref, b_ref, g_ref, bon_ref, s0_ref, lnw_ref, lnb_ref,
                       y_ref, sT_ref, st_ref, *, cl):
    c = pl.program_id(1)
    n_chunks = pl.num_programs(1)
    n_slab = r_ref.shape[0]
    n_sub = r_ref.shape[1] // cl

    @pl.when(c == 0)
    def _():
        st_ref[...] = s0_ref[...]

    lane = lax.broadcasted_iota(jnp.int32, (1, LANES), 1)
    first = lane < HEAD_B
    ti = lax.broadcasted_iota(jnp.int32, (cl, cl), 0)
    tj = lax.broadcasted_iota(jnp.int32, (cl, cl), 1)
    tri_incl = jnp.where(ti >= tj, 1.0, 0.0).astype(BF16)
    seg = _seg_ones()
    n2 = 2 * cl
    ri = lax.broadcasted_iota(jnp.int32, (n2, n2), 0) % cl
    ci = lax.broadcasted_iota(jnp.int32, (n2, n2), 1) % cl
    strict = ri > ci
    incl = ri >= ci
    zero = jnp.zeros((n2, n2), F32)
    inv_n = 1.0 / HEAD_B

    items = [(s, q) for q in range(n_sub) for s in range(n_slab)]
    rows = lambda q: slice(q * cl, (q + 1) * cl)

    pre = []
    for s, q in items:
        ld = ld_ref[s, rows(q), :]
        hi, mid, lo = _split3(ld)
        cum = _dot(tri_incl, hi) + (_dot(tri_incl, mid) + _dot(tri_incl, lo))
        pre.append((ld, cum))

    stacks = []
    for (s, q), (ld, cum) in zip(items, pre):
        cum_last = cum[cl - 1:cl, :]
        g_incl = jnp.exp(cum)
        g_excl = jnp.exp(cum - ld)
        g_inv = jnp.exp(-cum)
        g_end = jnp.exp(cum_last - cum)
        k = k_ref[s, rows(q), :]
        bv = b_ref[s, rows(q), :]
        stacks.append(dict(
            g_last=jnp.exp(cum_last),
            la=_stack2(na_ref[s, rows(q), :] * g_excl, first),
            lr=_stack2(r_ref[s, rows(q), :] * g_incl, first),
            l2=jnp.concatenate([_stack2(bv * g_inv, first), _stack2(k * g_inv, first)], axis=0),
            lh=jnp.concatenate([_stack2(bv * g_end, first), _stack2(k * g_end, first)], axis=0),
            v2=_stack2(v_ref[s, rows(q), :], first)))

    grams = [_dot(jnp.concatenate([d["la"], d["lr"]], axis=0), d["l2"], NT) for d in stacks]
    for d, gram in zip(stacks, grams):
        d["g_ak"] = jnp.where(strict, gram[:n2, n2:], zero).astype(BF16)
        d["g_r"] = jnp.concatenate([jnp.where(incl, gram[n2:, :n2], zero),
                                    jnp.where(incl, gram[n2:, n2:], zero)], axis=1).astype(BF16)
    rr = lax.broadcasted_iota(jnp.int32, (n2, n2), 0)
    cc = lax.broadcasted_iota(jnp.int32, (n2, n2), 1)
    eye = jnp.where(rr == cc, 1.0, 0.0).astype(F32)
    pows = [jnp.where(strict, gram[:n2, :n2], zero) for gram in grams]
    invs = [eye + p for p in pows]
    for _ in range(int(math.log2(cl)) - 1):
        pows = [_dot(p.astype(BF16), p.astype(BF16)) for p in pows]
        invs = [inv + _dot(inv.astype(BF16), p.astype(BF16)) for inv, p in zip(invs, pows)]
    akv = [_dot(d["g_ak"], d["v2"]) for d in stacks]

    sts = [st_ref[s] for s in range(n_slab)]
    for q in range(n_sub):
        base = q * n_slab
        stb = [st.astype(BF16) for st in sts]
        xs = [_dot(stacks[base + s]["la"], stb[s], NT) + akv[base + s] for s in range(n_slab)]
        us = [_dot(invs[base + s].astype(BF16), xs[s].astype(BF16)) for s in range(n_slab)]
        uvs = [jnp.concatenate([us[s].astype(BF16), stacks[base + s]["v2"]], axis=0) for s in range(n_slab)]
        y2s = [_dot(stacks[base + s]["lr"], stb[s], NT) + _dot(stacks[base + s]["g_r"], uvs[s])
               for s in range(n_slab)]
        sts = [sts[s] * stacks[base + s]["g_last"] + _dot(uvs[s], stacks[base + s]["lh"], TN)
               for s in range(n_slab)]
        ys = [y2[:cl, :] + y2[cl:, :] for y2 in y2s]
        means = [_mm_exact_rhs2(y, seg) * inv_n for y in ys]
        ds = [y - m for y, m in zip(ys, means)]
        vars_ = [_mm_exact_rhs2(d * d, seg) * inv_n for d in ds]
        for s in range(n_slab):
            sl = slice(s * LANES, (s + 1) * LANES)
            yn = ds[s] * lax.rsqrt(vars_[s] + GN_EPS) * lnw_ref[:, sl] + lnb_ref[:, sl]
            y_ref[s, rows(q), :] = ((yn + bon_ref[s, rows(q), :]) * g_ref[s, rows(q), :]).astype(y_ref.dtype)
    for s in range(n_slab):
        st_ref[s] = sts[s]

    @pl.when(c == n_chunks - 1)
    def _():
        sT_ref[...] = st_ref[...]


def _rwkv_chunks(prep, s0_bd, lnx_w, lnx_b):
    b, n_slab, t, _ = prep[0].shape
    cl = min(CHUNK_B, t)
    step = min(CHUNKS_PER_STEP * cl, t)
    wb = n_slab * LANES
    in_spec = pl.BlockSpec((None, n_slab, step, LANES), lambda bb, c: (bb, 0, c, 0))
    st_spec = pl.BlockSpec((None, n_slab, LANES, LANES), lambda bb, c: (bb, 0, 0, 0))
    vec = pl.BlockSpec((1, wb), lambda bb, c: (0, 0))
    return pl.pallas_call(
        functools.partial(_rwkv_chunk_kernel, cl=cl),
        grid=(b, t // step),
        in_specs=[in_spec] * 8 + [st_spec, vec, vec],
        out_specs=[in_spec, st_spec],
        out_shape=[
            jax.ShapeDtypeStruct((b, n_slab, t, LANES), BF16),
            jax.ShapeDtypeStruct((b, n_slab, LANES, LANES), F32),
        ],
        scratch_shapes=[pltpu.VMEM((n_slab, LANES, LANES), F32)],
        compiler_params=_params(("parallel", "arbitrary")),
        name="rwkv_chunks",
    )(*prep, s0_bd, lnx_w, lnx_b)


def _state_to_blockdiag(s):
    b, h, n, _ = s.shape
    s = s.reshape(b, h // 2, 2, n, n)
    z = jnp.zeros_like(s[:, :, 0])
    top = jnp.concatenate([s[:, :, 0], z], axis=-1)
    bot = jnp.concatenate([z, s[:, :, 1]], axis=-1)
    return jnp.concatenate([top, bot], axis=-2)


def _state_from_blockdiag(sbd):
    b, hs, n2, _ = sbd.shape
    n = n2 // 2
    return jnp.stack([sbd[:, :, :n, :n], sbd[:, :, n:, n:]], axis=2).reshape(b, 2 * hs, n, n)


def _route(logits, n_groups, n_per_group):
    ne = n_groups * n_per_group
    lane = lax.broadcasted_iota(jnp.int32, logits.shape, 1)
    big = jnp.int32(1 << 30)
    is_grp = (lane >= ne) & (lane < ne + n_groups)
    lg = jnp.where(is_grp, logits, NEG_INF)
    mg = jnp.max(lg, axis=-1, keepdims=True)
    gi = jnp.min(jnp.where(is_grp & (lg == mg), lane, big), axis=-1, keepdims=True) - ne
    eg = jnp.where(is_grp, jnp.exp(lg - mg), 0.0)
    p_sel = 1.0 / jnp.sum(eg, axis=-1, keepdims=True)
    in_grp = (lane < ne) & ((lane // n_per_group) == gi)
    le = jnp.where(in_grp, logits, NEG_INF)
    me = jnp.max(le, axis=-1, keepdims=True)
    ee = jnp.where(in_grp, jnp.exp(le - me), 0.0)
    pe = ee / jnp.sum(ee, axis=-1, keepdims=True)
    v1 = jnp.max(jnp.where(in_grp, pe, -1.0), axis=-1, keepdims=True)
    i1 = jnp.min(jnp.where(in_grp & (pe == v1), lane, big), axis=-1, keepdims=True)
    rest = in_grp & (lane != i1)
    v2 = jnp.max(jnp.where(rest, pe, -1.0), axis=-1, keepdims=True)
    i2 = jnp.min(jnp.where(rest & (pe == v2), lane, big), axis=-1, keepdims=True)
    tot = v1 + v2
    w1 = (v1 / tot) * p_sel
    w2 = (v2 / tot) * p_sel
    return (jnp.where(lane == 0, w1, 0.0) + jnp.where(lane == 1, w2, 0.0)
            + jnp.where(lane == 2, i1.astype(F32), 0.0) + jnp.where(lane == 3, i2.astype(F32), 0.0))


def _mix_kernel(x_ref, oa_ref, ob_ref, gt_ref, wa_ref, wb_ref, wo_ref, gn_ref, wr_ref, br_ref,
                x2_ref, hn_ref, route_ref, route_t_ref, *, n_groups, n_per_group):
    d = x_ref.shape[1]
    nb, n_slab, t_blk, _ = ob_ref.shape
    y_a = _dot(oa_ref[...], wa_ref[...])
    ob = jnp.concatenate([ob_ref[:, s].reshape(nb * t_blk, LANES) for s in range(n_slab)], axis=1)
    y_b = _dot(ob, wb_ref[...])
    merged = gt_ref[:, :d] * y_a + gt_ref[:, d:] * y_b
    x2 = x_ref[...] + _dot(merged.astype(BF16), wo_ref[...])
    x2_ref[...] = x2
    hn = _rmsnorm(x2, gn_ref[...], NORM_EPS)
    hn_ref[...] = hn
    logits = _mm(hn, wr_ref[...], 3) + br_ref[...]
    route = _route(logits, n_groups, n_per_group)
    route_ref[...] = route
    route_t_ref[...] = route.T[:ROUTE_ROWS, :]


def _mix(x2d, o_a, o_b, gates, w_a, w_b, w_o, gain_ffn, w_route, b_route, *, n_groups, n_per_group):
    b, n_slab, t, _ = o_b.shape
    n, d = x2d.shape
    if t >= 512:
        nb, t_blk = 1, 512
    else:
        nb, t_blk = min(b, 512 // t), t
    tm = nb * t_blk
    nt = t // t_blk
    full = lambda a: pl.BlockSpec(a.shape, lambda i: (0,) * a.ndim)
    return pl.pallas_call(
        functools.partial(_mix_kernel, n_groups=n_groups, n_per_group=n_per_group),
        grid=(n // tm,),
        in_specs=[
            pl.BlockSpec((tm, d), lambda i: (i, 0)),
            pl.BlockSpec((tm, o_a.shape[1]), lambda i: (i, 0)),
            pl.BlockSpec((nb, n_slab, t_blk, LANES), lambda i: (i // nt, 0, i % nt, 0)),
            pl.BlockSpec((tm, 2 * d), lambda i: (i, 0)),
            full(w_a), full(w_b), full(w_o), full(gain_ffn), full(w_route), full(b_route),
        ],
        out_specs=[
            pl.BlockSpec((tm, d), lambda i: (i, 0)),
            pl.BlockSpec((tm, d), lambda i: (i, 0)),
            pl.BlockSpec((tm, LANES), lambda i: (i, 0)),
            pl.BlockSpec((ROUTE_ROWS, tm), lambda i: (0, i)),
        ],
        out_shape=[
            jax.ShapeDtypeStruct((n, d), F32),
            jax.ShapeDtypeStruct((n, d), F32),
            jax.ShapeDtypeStruct((n, LANES), F32),
            jax.ShapeDtypeStruct((ROUTE_ROWS, n), F32),
        ],
        compiler_params=_params(("parallel",)),
        name="mix_route",
    )(x2d, o_a, o_b, gates, w_a, w_b, w_o, gain_ffn, w_route, b_route)


def _rank_kernel(rt_ref, ri_ref, cnt_ref, carry_ref, *, ne_pad):
    i = pl.program_id(0)

    @pl.when(i == 0)
    def _():
        carry_ref[...] = jnp.zeros_like(carry_ref)

    rt = rt_ref[...]
    tt = rt.shape[1]
    e1 = rt[2:3, :].astype(jnp.int32)
    e2 = rt[3:4, :].astype(jnp.int32)
    eid = lax.broadcasted_iota(jnp.int32, (ne_pad, tt), 0)
    m1 = eid == e1
    m2 = eid == e2
    member = jnp.where(m1 | m2, 1.0, 0.0)
    before = jnp.where(lax.broadcasted_iota(jnp.int32, (tt, tt), 0) < lax.broadcasted_iota(jnp.int32, (tt, tt), 1),
                       1.0, 0.0).astype(BF16)
    carry = carry_ref[...]
    rank = _dot(member.astype(BF16), before) + carry[:, :1]
    r1 = jnp.sum(jnp.where(m1, rank, 0.0), axis=0, keepdims=True).astype(jnp.int32)
    r2 = jnp.sum(jnp.where(m2, rank, 0.0), axis=0, keepdims=True).astype(jnp.int32)
    ri_ref[...] = jnp.concatenate([e1, e2, r1, r2, jnp.zeros((ROUTE_ROWS - 4, tt), jnp.int32)], axis=0)
    carry = carry + jnp.sum(member, axis=1, keepdims=True)
    carry_ref[...] = carry

    @pl.when(i == pl.num_programs(0) - 1)
    def _():
        cnt_ref[...] = carry


def _rank(route_t, ne_pad):
    _, n = route_t.shape
    tt = min(512, n)
    return pl.pallas_call(
        functools.partial(_rank_kernel, ne_pad=ne_pad),
        grid=(n // tt,),
        in_specs=[pl.BlockSpec((ROUTE_ROWS, tt), lambda i: (0, i))],
        out_specs=[pl.BlockSpec((ROUTE_ROWS, tt), lambda i: (0, i)),
                   pl.BlockSpec((ne_pad, LANES), lambda i: (0, 0))],
        out_shape=[jax.ShapeDtypeStruct((ROUTE_ROWS, n), jnp.int32),
                   jax.ShapeDtypeStruct((ne_pad, LANES), F32)],
        scratch_shapes=[pltpu.VMEM((ne_pad, LANES), F32)],
        compiler_params=_params(("arbitrary",)),
        name="moe_rank",
    )(route_t)


def _row_copy(src, src_row, dst, dst_row, sem):
    return pltpu.make_async_copy(src.at[pl.ds(src_row, 1)], dst.at[pl.ds(dst_row, 1)], sem)


def _scatter_kernel(starts_ref, ri_ref, hn_ref, xs_in_ref, xs_ref, sem):
    del xs_in_ref
    i = pl.program_id(0)
    tt = ri_ref.shape[1]

    def issue(t, carry):
        tok = i * tt + t
        p1 = starts_ref[ri_ref[0, t]] + ri_ref[2, t]
        p2 = starts_ref[ri_ref[1, t]] + ri_ref[3, t]
        _row_copy(hn_ref, tok, xs_ref, p1, sem).start()
        _row_copy(hn_ref, tok, xs_ref, p2, sem).start()
        return carry

    lax.fori_loop(0, tt, issue, 0, unroll=8)

    def drain(t, carry):
        _row_copy(hn_ref, 0, xs_ref, 0, sem).wait()
        _row_copy(hn_ref, 0, xs_ref, 0, sem).wait()
        return carry

    lax.fori_loop(0, tt, drain, 0, unroll=8)


def _scatter(starts, route_i, hn, n_rows):
    n, d = hn.shape
    tt = min(512, n)
    xs0 = jnp.zeros((n_rows, d), hn.dtype)
    return pl.pallas_call(
        _scatter_kernel,
        grid_spec=pltpu.PrefetchScalarGridSpec(
            num_scalar_prefetch=1,
            grid=(n // tt,),
            in_specs=[
                pl.BlockSpec((ROUTE_ROWS, tt), lambda i, st: (0, i), memory_space=pltpu.SMEM),
                pl.BlockSpec(memory_space=pl.ANY),
                pl.BlockSpec(memory_space=pl.ANY),
            ],
            out_specs=pl.BlockSpec(memory_space=pl.ANY),
            scratch_shapes=[pltpu.SemaphoreType.DMA(())],
        ),
        out_shape=jax.ShapeDtypeStruct((n_rows, d), hn.dtype),
        input_output_aliases={3: 0},
        compiler_params=_params(("arbitrary",)),
        name="moe_scatter",
    )(starts, route_i, hn, xs0)


def _expert_kernel(te_ref, nu_ref, xs_ref, w1_ref, w3_ref, w2_ref, o_ref):
    i = pl.program_id(0)

    @pl.when(i < nu_ref[0])
    def _():
        x = xs_ref[...].astype(BF16)
        up = _dot(x, w1_ref[...])
        lin = _dot(x, w3_ref[...])
        z = (up * _sigmoid(up)) * lin
        o_ref[...] = _dot(z.astype(BF16), w2_ref[...])

    @pl.when(i >= nu_ref[0])
    def _():
        o_ref[...] = jnp.zeros_like(o_ref)


def _experts(tile_expert, n_used, xs, w1, w3, w2):
    n_rows, d = xs.shape
    _, _, f = w1.shape
    tm = EXPERT_TILE
    last = lambda i, nu: jnp.minimum(i, jnp.maximum(nu[0] - 1, 0))
    return pl.pallas_call(
        _expert_kernel,
        grid_spec=pltpu.PrefetchScalarGridSpec(
            num_scalar_prefetch=2,
            grid=(n_rows // tm,),
            in_specs=[
                pl.BlockSpec((tm, d), lambda i, te, nu: (last(i, nu), 0)),
                pl.BlockSpec((None, d, f), lambda i, te, nu: (te[i], 0, 0)),
                pl.BlockSpec((None, d, f), lambda i, te, nu: (te[i], 0, 0)),
                pl.BlockSpec((None, f, d), lambda i, te, nu: (te[i], 0, 0)),
            ],
            out_specs=pl.BlockSpec((tm, d), lambda i, te, nu: (i, 0)),
        ),
        out_shape=jax.ShapeDtypeStruct((n_rows, d), F32),
        compiler_params=_params(("arbitrary",)),
        name="moe_experts",
    )(tile_expert, n_used, xs, w1, w3, w2)


def _combine_kernel(starts_ref, ri_ref, rf_ref, x2_ref, gf_ref, o_ref, y_ref, buf1, buf2, sem):
    tt = ri_ref.shape[1]

    def issue(t, carry):
        p1 = starts_ref[ri_ref[0, t]] + ri_ref[2, t]
        p2 = starts_ref[ri_ref[1, t]] + ri_ref[3, t]
        _row_copy(o_ref, p1, buf1, t, sem).start()
        _row_copy(o_ref, p2, buf2, t, sem).start()
        return carry

    lax.fori_loop(0, tt, issue, 0, unroll=8)

    def drain(t, carry):
        _row_copy(o_ref, 0, buf1, 0, sem).wait()
        _row_copy(o_ref, 0, buf2, 0, sem).wait()
        return carry

    lax.fori_loop(0, tt, drain, 0, unroll=8)
    rf = rf_ref[...]
    y = x2_ref[...] + rf[:, 0:1] * buf1[...] + rf[:, 1:2] * buf2[...]
    y_ref[...] = _rmsnorm(y, gf_ref[...], NORM_EPS)


def _combine(starts, route_i, route_f, x2, gain_final, o):
    n, d = x2.shape
    tt = min(256, n)
    return pl.pallas_call(
        _combine_kernel,
        grid_spec=pltpu.PrefetchScalarGridSpec(
            num_scalar_prefetch=1,
            grid=(n // tt,),
            in_specs=[
                pl.BlockSpec((ROUTE_ROWS, tt), lambda i, st: (0, i), memory_space=pltpu.SMEM),
                pl.BlockSpec((tt, LANES), lambda i, st: (i, 0)),
                pl.BlockSpec((tt, d), lambda i, st: (i, 0)),
                pl.BlockSpec((1, d), lambda i, st: (0, 0)),
                pl.BlockSpec(memory_space=pl.ANY),
            ],
            out_specs=pl.BlockSpec((tt, d), lambda i, st: (i, 0)),
            scratch_shapes=[pltpu.VMEM((tt, d), F32), pltpu.VMEM((tt, d), F32), pltpu.SemaphoreType.DMA(())],
        ),
        out_shape=jax.ShapeDtypeStruct((n, d), F32),
        compiler_params=_params(("arbitrary",)),
        name="moe_combine",
    )(starts, route_i, route_f, x2, gain_final, o)


def _moe(hn, route_f, route_t, w1, w3, w2, x2, gain_final):
    n, d = hn.shape
    ne = w1.shape[0]
    ne_pad = -(-ne // 8) * 8
    tm = EXPERT_TILE
    n_tiles = -(-2 * n // tm) + ne
    route_i, counts = _rank(route_t, ne_pad)
    counts = counts[:ne, 0].astype(jnp.int32)
    tiles = (counts + tm - 1) // tm
    ends = jnp.cumsum(tiles)
    starts = (ends - tiles) * tm
    n_used = ends[-1:]
    tile_expert = jnp.minimum(jnp.sum(jnp.arange(n_tiles)[:, None] >= ends[None, :], axis=1), ne - 1)
    xs = _scatter(starts, route_i, hn, n_tiles * tm)
    o = _experts(tile_expert.astype(jnp.int32), n_used.astype(jnp.int32), xs, w1, w3, w2)
    return _combine(starts, route_i, route_f, x2, gain_final, o)


def _run_path(x, prev_row, s0, cache, wts, dims):
    b, t, d = x.shape
    x2d = x.reshape(b * t, d)
    qkv, kf, vf = _qkv_proj(x2d, wts["norm_mix"], wts["w_qkv"], dims["da"] ** -0.5)
    pb = _norm_mm(x2d, wts["norm_mix"], wts["w_pb"], False, "rwkv_proj")
    gates = _norm_mm(x2d, wts["norm_mix"], wts["w_gates"], True, "gate_proj")

    qkv3 = qkv.reshape(b, t, -1)
    if cache is None:
        o_a = _attn_prompt(qkv3, wts["lams"], wts["subln_w"], n_heads=dims["ha"], chunk=dims["chunk"],
                           lam_init=dims["lam_init"])
    else:
        o_a = _attn_sample(qkv3, cache[0], cache[1], wts["lams"], wts["subln_w"], n_heads=dims["ha"],
                           lam_init=dims["lam_init"])

    pb3 = pb.reshape(b, t, -1)
    prep = _rwkv_prep(pb3, prev_row, wts["mu"], wts["w0"], wts["w2p"], wts["a0"], wts["a2p"], wts["g2"],
                      wts["k_k"], wts["k_a"], wts["r_k"], wb=dims["wb"])
    o_b, s_new = _rwkv_chunks(prep, _state_to_blockdiag(s0), wts["lnx_w"], wts["lnx_b"])

    x2, hn, route_f, route_t = _mix(x2d, o_a.reshape(b * t, -1), o_b, gates, wts["w_a_out"], wts["w_b_out"],
                                    wts["w_o"], wts["norm_ffn"], wts["w_route"], wts["b_route"],
                                    n_groups=dims["n_groups"], n_per_group=dims["n_per_group"])
    y = _moe(hn, route_f, route_t, wts["moe_w1"], wts["moe_w3"], wts["moe_w2"], x2, wts["norm_final"])

    ha = dims["ha"]
    return (y.reshape(b, t, d), kf.reshape(1, b, t, ha, -1), vf.reshape(1, b, t, ha, -1),
            pb3[:, t - 1:t, :][None], _state_from_blockdiag(s_new)[None])


def kernel(x_prompt, x_sample, cache_attn_k, cache_attn_v, state_rwkv_shift, state_rwkv_wkv, norm_mix, w_in, lambda_q1, lambda_k1, lambda_q2, lambda_k2, subln_w, w_a_out, rwkv_mu, rwkv_w0, rwkv_w2, rwkv_a0, rwkv_a2, rwkv_g2, rwkv_k_k, rwkv_k_a, rwkv_r_k, rwkv_lnx_w, rwkv_lnx_b, w_b_out, w_o, norm_ffn, moe_w_group, moe_b_group, moe_w_router, moe_b_router, moe_w1, moe_w3, moe_w2, norm_final):
    assert w_in.shape[0] == 1, "single-layer trunk"
    l = 0
    d = x_prompt.shape[-1]
    ha, dva = cache_attn_k.shape[3], cache_attn_v.shape[4]
    wa = ha * dva
    hb, db = state_rwkv_wkv.shape[2], state_rwkv_wkv.shape[3]
    wb = hb * db
    lora_w, lora_a, lora_g = rwkv_w2.shape[1], rwkv_a2.shape[1], rwkv_g2.shape[1]
    rwkv_cols = 3 * wb + lora_w + lora_a + lora_g
    n_groups = moe_w_group.shape[-1]
    n_per_group = moe_w_router.shape[-1] // n_groups
    ne = n_groups * n_per_group
    assert db == HEAD_B and lora_w + lora_a == LANES and lora_g == LANES and dva == LANES
    assert ne + n_groups <= LANES

    row = lambda v: v.reshape(1, -1).astype(F32)
    w_l = w_in[l]
    zeros_w = jnp.zeros((lora_a, wb), F32)
    zeros_a = jnp.zeros((lora_w, wb), F32)
    w_route = jnp.zeros((d, LANES), F32)
    w_route = w_route.at[:, :ne].set(moe_w_router[l]).at[:, ne:ne + n_groups].set(moe_w_group[l])
    b_route = jnp.zeros((1, LANES), F32)
    b_route = b_route.at[0, :ne].set(moe_b_router[l]).at[0, ne:ne + n_groups].set(moe_b_group[l])
    f = moe_w1.shape[-1]
    wts = dict(
        norm_mix=row(norm_mix[l]),
        w_qkv=w_l[:, :3 * wa].astype(BF16),
        w_pb=w_l[:, 3 * wa:3 * wa + rwkv_cols].astype(BF16),
        w_gates=w_l[:, 3 * wa + rwkv_cols:].astype(BF16),
        lams=[row(lambda_q1[l]), row(lambda_k1[l]), row(lambda_q2[l]), row(lambda_k2[l])],
        subln_w=row(subln_w[l]),
        mu=row(rwkv_mu[l]), w0=row(rwkv_w0[l]), a0=row(rwkv_a0[l]),
        w2p=jnp.concatenate([rwkv_w2[l], zeros_w], axis=0),
        a2p=jnp.concatenate([zeros_a, rwkv_a2[l]], axis=0),
        g2=rwkv_g2[l].astype(F32),
        k_k=row(rwkv_k_k[l]), k_a=row(rwkv_k_a[l]), r_k=row(rwkv_r_k[l]),
        lnx_w=row(rwkv_lnx_w[l]), lnx_b=row(rwkv_lnx_b[l]),
        w_a_out=w_a_out[l].astype(BF16), w_b_out=w_b_out[l].astype(BF16), w_o=w_o[l].astype(BF16),
        norm_ffn=row(norm_ffn[l]), w_route=w_route, b_route=b_route,
        moe_w1=moe_w1[l].reshape(ne, d, f).astype(BF16),
        moe_w3=moe_w3[l].reshape(ne, d, f).astype(BF16),
        moe_w2=moe_w2[l].reshape(ne, f, d).astype(BF16),
        norm_final=row(norm_final),
    )
    dims = dict(ha=ha, da=dva // 2, wb=wb, chunk=CHUNK_B, lam_init=0.8 - 0.6 * math.exp(-0.3 * l),
                n_groups=n_groups, n_per_group=n_per_group)

    bp = x_prompt.shape[0]
    yp, kp, vp, shp, wkp = _run_path(
        x_prompt.astype(F32), jnp.zeros((bp, 1, rwkv_cols), F32), jnp.zeros((bp, hb, db, db), F32), None, wts, dims)

    bs, past = cache_attn_k.shape[1], cache_attn_k.shape[2]
    cache = (cache_attn_k[l].reshape(bs, past * ha, dva), cache_attn_v[l].reshape(bs, past * ha, dva))
    ys, ks_, vs_, shs, wks = _run_path(
        x_sample.astype(F32), state_rwkv_shift[l].astype(F32), state_rwkv_wkv[l].astype(F32), cache, wts, dims)

    return (yp, ys, kp, vp, shp, wkp, ks_, vs_, shs, wks)
```

```python
import functools
import math

import jax
import jax.numpy as jnp
from jax import lax
from jax.experimental import pallas as pl
from jax.experimental.pallas import tpu as pltpu

F32 = jnp.float32
BF16 = jnp.bfloat16

LANES = 128
HEAD_B = 64
CHUNK_B = 64
CHUNKS_PER_STEP = 2
ATTN_TILE = 512
ROUTE_ROWS = 8
EXPERT_TILE = 256
NORM_EPS = 1e-6
DIFF_EPS = 1e-5
GN_EPS = 64e-5
NEG_INF = -1e30
VMEM_LIMIT = 56 * 1024 * 1024

NN = (((1,), (0,)), ((), ()))
NT = (((1,), (1,)), ((), ()))
TN = (((0,), (0,)), ((), ()))


def _dot(a, b, dims=NN):
    return lax.dot_general(a, b, dims, preferred_element_type=F32)


def _split2(a):
    hi = a.astype(BF16)
    lo = (a - hi.astype(F32)).astype(BF16)
    return hi, lo


def _split3(a):
    hi = a.astype(BF16)
    r = a - hi.astype(F32)
    mid = r.astype(BF16)
    lo = (r - mid.astype(F32)).astype(BF16)
    return hi, mid, lo


def _mm(a, b, passes=1, dims=NN):
    if passes == 1:
        return _dot(a.astype(BF16), b.astype(BF16), dims)
    a_hi, a_lo = _split2(a)
    b_hi, b_lo = _split2(b)
    return _dot(a_hi, b_hi, dims) + (_dot(a_hi, b_lo, dims) + _dot(a_lo, b_hi, dims))


def _mm_exact_rhs(a, b_bf16, dims=NN):
    hi, mid, lo = _split3(a)
    return _dot(hi, b_bf16, dims) + (_dot(mid, b_bf16, dims) + _dot(lo, b_bf16, dims))


def _mm_exact_rhs2(a, b_bf16, dims=NN):
    hi, lo = _split2(a)
    return _dot(hi, b_bf16, dims) + _dot(lo, b_bf16, dims)


def _rmsnorm(x, g, eps):
    return x * lax.rsqrt(jnp.mean(x * x, axis=-1, keepdims=True) + eps) * g


def _sigmoid(x):
    return 1.0 / (1.0 + jnp.exp(-x))


def _params(sem):
    return pltpu.CompilerParams(dimension_semantics=sem, vmem_limit_bytes=VMEM_LIMIT)


def _qkv_kernel(x_ref, g_ref, w_ref, qkv_ref, kf_ref, vf_ref, *rest, q_scale, transposed):
    if transposed:
        qt_ref, vt_ref, h_ref = rest
    else:
        (h_ref,) = rest
    j = pl.program_id(1)

    @pl.when(j == 0)
    def _():
        h_ref[...] = _rmsnorm(x_ref[...], g_ref[...], NORM_EPS).astype(BF16)

    p = _dot(h_ref[...], w_ref[...])

    @pl.when(j == 0)
    def _():
        q = p * q_scale
        qkv_ref[...] = q.astype(BF16)
        if transposed:
            qt_ref[...] = q.T.astype(BF16)

    @pl.when(j == 1)
    def _():
        qkv_ref[...] = p.astype(BF16)
        kf_ref[...] = p

    @pl.when(j == 2)
    def _():
        qkv_ref[...] = p.astype(BF16)
        vf_ref[...] = p
        if transposed:
            vt_ref[...] = p.T.astype(BF16)


def _qkv_proj(x2d, gain, w_qkv, q_scale, transposed):
    n, d = x2d.shape
    wa = w_qkv.shape[1] // 3
    tm = min(ATTN_TILE, n)
    out_specs = [
        pl.BlockSpec((tm, wa), lambda i, j: (i, j)),
        pl.BlockSpec((tm, wa), lambda i, j: (i, 0)),
        pl.BlockSpec((tm, wa), lambda i, j: (i, 0)),
    ]
    out_shape = [
        jax.ShapeDtypeStruct((n, 3 * wa), BF16),
        jax.ShapeDtypeStruct((n, wa), F32),
        jax.ShapeDtypeStruct((n, wa), F32),
    ]
    if transposed:
        out_specs += [pl.BlockSpec((None, wa, tm), lambda i, j: (i, 0, 0))] * 2
        out_shape += [jax.ShapeDtypeStruct((n // tm, wa, tm), BF16)] * 2
    return pl.pallas_call(
        functools.partial(_qkv_kernel, q_scale=q_scale, transposed=transposed),
        grid=(n // tm, 3),
        in_specs=[
            pl.BlockSpec((tm, d), lambda i, j: (i, 0)),
            pl.BlockSpec((1, d), lambda i, j: (0, 0)),
            pl.BlockSpec((d, wa), lambda i, j: (0, j)),
        ],
        out_specs=out_specs,
        out_shape=out_shape,
        scratch_shapes=[pltpu.VMEM((tm, d), BF16)],
        compiler_params=_params(("parallel", "arbitrary")),
        name="qkv_proj",
    )(x2d, gain, w_qkv)


def _norm_mm_kernel(x_ref, g_ref, w_ref, o_ref, *, act):
    h = _rmsnorm(x_ref[...], g_ref[...], NORM_EPS).astype(BF16)
    p = _dot(h, w_ref[...])
    if act:
        p = _sigmoid(p)
    o_ref[...] = p


def _norm_mm(x2d, gain, w, act, name):
    n, d = x2d.shape
    c = w.shape[1]
    tm = min(512, n)
    return pl.pallas_call(
        functools.partial(_norm_mm_kernel, act=act),
        grid=(n // tm,),
        in_specs=[
            pl.BlockSpec((tm, d), lambda i: (i, 0)),
            pl.BlockSpec((1, d), lambda i: (0, 0)),
            pl.BlockSpec((d, c), lambda i: (0, 0)),
        ],
        out_specs=pl.BlockSpec((tm, c), lambda i: (i, 0)),
        out_shape=jax.ShapeDtypeStruct((n, c), F32),
        compiler_params=_params(("parallel",)),
        name=name,
    )(x2d, gain, w)


def _lambda(lq1_ref, lk1_ref, lq2_ref, lk2_ref, lam_init):
    s1 = jnp.sum(lq1_ref[...] * lk1_ref[...], axis=-1, keepdims=True)
    s2 = jnp.sum(lq2_ref[...] * lk2_ref[...], axis=-1, keepdims=True)
    return jnp.exp(s1) - jnp.exp(s2) + lam_init


def _split_q(q):
    lane = lax.broadcasted_iota(jnp.int32, (1, q.shape[1]), 1)
    first = lane < (q.shape[1] // 2)
    zero = jnp.zeros_like(q)
    return jnp.where(first, q, zero), jnp.where(first, zero, q)


def _softmax_block(s, m, l, acc, v):
    m_new = jnp.maximum(m, jnp.max(s, axis=-1, keepdims=True))
    alpha = jnp.exp2(m - m_new)
    p = jnp.exp2(s - m_new)
    l_new = alpha * l + jnp.sum(p, axis=-1, keepdims=True)
    acc_new = alpha * acc + _dot(p.astype(BF16), v)
    return m_new, l_new, acc_new


def _diff_finish(acc1, l1, acc2, l2, lam, sub_w, lam_init):
    o = acc1 / l1 - lam * (acc2 / l2)
    return _rmsnorm(o, sub_w, DIFF_EPS) * (1.0 - lam_init)


def _attn_prompt_kernel(lq1_ref, lk1_ref, lq2_ref, lk2_ref, subc_ref, qt_ref, k_ref, vt_ref, o_ref,
                        s0_ref, s1_ref, s2_ref, s3_ref, mt_ref, m_ref, l_ref, acc_ref, *, chunk, lam_init):
    i = pl.program_id(2)
    dv, tq = qt_ref.shape
    tk = vt_ref.shape[2]
    lam = _lambda(lq1_ref, lk1_ref, lq2_ref, lk2_ref, lam_init)
    qt = qt_ref[...]
    feat = lax.broadcasted_iota(jnp.int32, (dv, 1), 0)
    zq = jnp.zeros_like(qt)
    q_maps = (jnp.where(feat < dv // 2, qt, zq), jnp.where(feat < dv // 2, zq, qt))
    acc_ref[...] = jnp.zeros_like(acc_ref)
    rb = 64
    kb = min(256, tk)

    m_ref[...] = jnp.full(m_ref.shape, NEG_INF, F32)
    l_ref[...] = jnp.zeros_like(l_ref)
    buf_a, buf_b = (s0_ref, s1_ref), (s2_ref, s3_ref)

    def scores(j, buf, masked):
        ks = k_ref[pl.ds(pl.multiple_of(j * tk, tk), tk), :]
        for mi in range(2):
            s = _dot(ks, q_maps[mi])
            if masked:
                kc = lax.broadcasted_iota(jnp.int32, (tk, tq), 0) // chunk
                qc = lax.broadcasted_iota(jnp.int32, (tk, tq), 1) // chunk
                s = jnp.where(kc <= qc, s, NEG_INF)
            buf[mi][...] = s
            mt_ref[2 * (buf is buf_b) + mi] = jnp.max(s, axis=0, keepdims=True)

    def softmax_pv(j, buf):
        vt = vt_ref[j]
        m_old = [m_ref[mi] for mi in range(2)]
        m_new = [jnp.maximum(m_old[mi], mt_ref[2 * (buf is buf_b) + mi]) for mi in range(2)]
        alpha = [jnp.exp2(m_old[mi] - m_new[mi]) for mi in range(2)]
        lsum = [jnp.zeros((8, tq), F32) for _ in range(2)]
        pv = [None, None]
        for hf in range(tk // kb):
            for mi in range(2):
                blocks = []
                for r in range(kb // rb):
                    lo = hf * kb + r * rb
                    p = jnp.exp2(buf[mi][lo:lo + rb, :] - m_new[mi])
                    lsum[mi] = lsum[mi] + jnp.sum(p.reshape(rb // 8, 8, tq), axis=0)
                    blocks.append(p.astype(BF16))
                part = _dot(vt[:, hf * kb:(hf + 1) * kb], jnp.concatenate(blocks, axis=0))
                pv[mi] = part if pv[mi] is None else pv[mi] + part
        for mi in range(2):
            acc_ref[mi] = acc_ref[mi] * alpha[mi] + pv[mi]
            m_ref[mi] = m_new[mi]
            l_ref[mi] = alpha[mi] * l_ref[mi] + lsum[mi]

    def stage_even(j, next_masked):
        scores(j + 1, buf_b, next_masked)
        softmax_pv(j, buf_a)

    def stage_odd(j, next_masked):
        scores(j + 1, buf_a, next_masked)
        softmax_pv(j, buf_b)

    @pl.when(i == 0)
    def _():
        scores(0, buf_a, True)
        softmax_pv(0, buf_a)

    @pl.when(i > 0)
    def _():
        scores(0, buf_a, False)

    def pair(jj, carry):
        stage_even(2 * jj, False)
        stage_odd(2 * jj + 1, False)
        return carry

    lax.fori_loop(0, jnp.maximum(i - 1, 0) // 2, pair, 0)

    @pl.when(i % 2 == 1)
    def _():
        stage_even(i - 1, True)
        softmax_pv(i, buf_b)

    @pl.when((i % 2 == 0) & (i > 0))
    def _():
        stage_even(i - 2, False)
        stage_odd(i - 1, True)
        softmax_pv(i, buf_a)

    l1 = jnp.sum(l_ref[0], axis=0, keepdims=True)
    l2 = jnp.sum(l_ref[1], axis=0, keepdims=True)
    ot = acc_ref[0] / l1 - lam * (acc_ref[1] / l2)
    ms = jnp.mean(ot * ot, axis=0, keepdims=True)
    ot = ot * lax.rsqrt(ms + DIFF_EPS) * subc_ref[...] * (1.0 - lam_init)
    o_ref[...] = ot.T.astype(o_ref.dtype)


def _attn_prompt(qkv, qt, vt, lams, sub_col, *, n_heads, chunk, lam_init):
    b, t, c3 = qkv.shape
    wa = c3 // 3
    dv = wa // n_heads
    nk, tile = qt.shape[1], qt.shape[3]
    lam_specs = [pl.BlockSpec((1, lams[0].shape[1]), lambda bb, h, i: (0, 0)) for _ in range(4)]
    return pl.pallas_call(
        functools.partial(_attn_prompt_kernel, chunk=chunk, lam_init=lam_init),
        grid=(b, n_heads, nk),
        in_specs=lam_specs + [
            pl.BlockSpec((dv, 1), lambda bb, h, i: (0, 0)),
            pl.BlockSpec((None, None, dv, tile), lambda bb, h, i: (bb, i, h, 0)),
            pl.BlockSpec((None, t, dv), lambda bb, h, i: (bb, 0, n_heads + h)),
            pl.BlockSpec((None, nk, dv, tile), lambda bb, h, i: (bb, 0, h, 0)),
        ],
        out_specs=pl.BlockSpec((None, tile, dv), lambda bb, h, i: (bb, i, h)),
        out_shape=jax.ShapeDtypeStruct((b, t, wa), BF16),
        scratch_shapes=[pltpu.VMEM((tile, tile), F32)] * 4
        + [pltpu.VMEM((4, 1, tile), F32),
           pltpu.VMEM((2, 1, tile), F32),
           pltpu.VMEM((2, 8, tile), F32),
           pltpu.VMEM((2, dv, tile), F32)],
        compiler_params=_params(("parallel", "parallel", "arbitrary")),
        name="diff_attn_prompt",
    )(*lams, sub_col, qt, qkv, vt)


def _attn_sample_kernel(lq1_ref, lk1_ref, lq2_ref, lk2_ref, sub_ref, q_ref, kn_ref, vn_ref, ck_ref, cv_ref,
                        o_ref, *, n_heads, lam_init):
    lam = _lambda(lq1_ref, lk1_ref, lq2_ref, lk2_ref, lam_init)
    tq = q_ref.shape[0]
    dv = q_ref.shape[1] // n_heads
    past = ck_ref.shape[0] // n_heads
    for h in range(n_heads):
        cols = slice(h * dv, (h + 1) * dv)
        q1, q2 = _split_q(q_ref[:, cols])
        ck = ck_ref[pl.ds(h, past, stride=n_heads), :].astype(BF16)
        cv = cv_ref[pl.ds(h, past, stride=n_heads), :].astype(BF16)
        kn = kn_ref[:, cols]
        vn = vn_ref[:, cols]
        outs = []
        for q in (q1, q2):
            m = jnp.full((tq, 1), NEG_INF, F32)
            l = jnp.zeros((tq, 1), F32)
            acc = jnp.zeros((tq, dv), F32)
            m, l, acc = _softmax_block(_dot(q, ck, NT), m, l, acc, cv)
            m, l, acc = _softmax_block(_dot(q, kn, NT), m, l, acc, vn)
            outs.append((acc, l))
        (a1, l1), (a2, l2) = outs
        o_ref[:, cols] = _diff_finish(a1, l1, a2, l2, lam, sub_ref[...], lam_init).astype(o_ref.dtype)


def _attn_sample(qkv, cache_k, cache_v, lams, sub_w, *, n_heads, lam_init):
    b, t, c3 = qkv.shape
    wa = c3 // 3
    dv = wa // n_heads
    rows = cache_k.shape[1]
    lam_specs = [pl.BlockSpec((1, lams[0].shape[1]), lambda bb: (0, 0)) for _ in range(4)]
    return pl.pallas_call(
        functools.partial(_attn_sample_kernel, n_heads=n_heads, lam_init=lam_init),
        grid=(b,),
        in_specs=lam_specs + [
            pl.BlockSpec((1, dv), lambda bb: (0, 0)),
            pl.BlockSpec((None, t, wa), lambda bb: (bb, 0, 0)),
            pl.BlockSpec((None, t, wa), lambda bb: (bb, 0, 1)),
            pl.BlockSpec((None, t, wa), lambda bb: (bb, 0, 2)),
            pl.BlockSpec((None, rows, dv), lambda bb: (bb, 0, 0)),
            pl.BlockSpec((None, rows, dv), lambda bb: (bb, 0, 0)),
        ],
        out_specs=pl.BlockSpec((None, t, wa), lambda bb: (bb, 0, 0)),
        out_shape=jax.ShapeDtypeStruct((b, t, wa), BF16),
        compiler_params=_params(("parallel",)),
        name="diff_attn_sample",
    )(*lams, sub_w, qkv, qkv, qkv, cache_k, cache_v)


def _seg_ones(n=LANES, seg=HEAD_B):
    r = lax.broadcasted_iota(jnp.int32, (n, n), 0) // seg
    c = lax.broadcasted_iota(jnp.int32, (n, n), 1) // seg
    return jnp.where(r == c, 1.0, 0.0).astype(BF16)


def _rwkv_prep_kernel(pb_ref, prev_ref, mu_ref, w0_ref, w2p_ref, a0_ref, a2p_ref, g2_ref, kk_ref, ka_ref,
                      rk_ref, r_o, k_o, v_o, ld_o, na_o, b_o, g_o, bon_o, carry_ref, *, wb):
    i = pl.program_id(1)

    @pl.when(i == 0)
    def _():
        carry_ref[...] = prev_ref[...]

    pb = pb_ref[...]
    tm = pb.shape[0]
    row = lax.broadcasted_iota(jnp.int32, (tm, 1), 0)
    prev = jnp.where(row == 0, carry_ref[...], pltpu.roll(pb, 1, axis=0))
    carry_ref[...] = pb[tm - 1:tm, :]
    xs = pb + (prev - pb) * mu_ref[...]

    n_slab = wb // LANES
    x_wa = xs[:, 3 * wb:3 * wb + LANES]
    x_g = xs[:, 3 * wb + LANES:3 * wb + 2 * LANES]
    lw = _mm(jnp.tanh(x_wa), w2p_ref[...], 3)
    la = _mm(x_wa, a2p_ref[...], 3)
    g = _mm(_sigmoid(x_g), g2_ref[...], 3)
    seg = _seg_ones()

    for s in range(n_slab):
        sl = slice(s * LANES, (s + 1) * LANES)
        r = xs[:, s * LANES:(s + 1) * LANES]
        k = xs[:, wb + s * LANES:wb + (s + 1) * LANES]
        v = xs[:, 2 * wb + s * LANES:2 * wb + (s + 1) * LANES]
        z = -(w0_ref[:, sl] + lw[:, sl])
        softplus = jnp.maximum(z, 0.0) + jnp.log1p(jnp.exp(-jnp.abs(z)))
        w_log = -softplus - 0.5
        a = _sigmoid(a0_ref[:, sl] + la[:, sl])
        kk = k * kk_ref[:, sl]
        nrm = jnp.sqrt(_mm_exact_rhs(kk * kk, seg))
        kk = kk / jnp.maximum(nrm, 1e-12)
        k2 = k * (1.0 + (a - 1.0) * ka_ref[:, sl])
        bonus = _mm_exact_rhs(r * k2 * rk_ref[:, sl], seg) * v
        r_o[s] = r
        k_o[s] = k2
        v_o[s] = v
        ld_o[s] = -jnp.exp(w_log)
        na_o[s] = -kk
        b_o[s] = kk * a
        g_o[s] = g[:, sl]
        bon_o[s] = bonus


def _rwkv_prep(pb, prev_row, mu, w0, w2p, a0, a2p, g2, k_k, k_a, r_k, *, wb):
    b, t, cols = pb.shape
    tm = min(256, t)
    n_slab = wb // LANES
    vec = lambda n: pl.BlockSpec((1, n), lambda bb, i: (0, 0))
    full = lambda a: pl.BlockSpec(a.shape, lambda bb, i: (0, 0))
    out_spec = pl.BlockSpec((None, n_slab, tm, LANES), lambda bb, i: (bb, 0, i, 0))
    out_sds = jax.ShapeDtypeStruct((b, n_slab, t, LANES), F32)
    return pl.pallas_call(
        functools.partial(_rwkv_prep_kernel, wb=wb),
        grid=(b, t // tm),
        in_specs=[
            pl.BlockSpec((None, tm, cols), lambda bb, i: (bb, i, 0)),
            pl.BlockSpec((None, 1, cols), lambda bb, i: (bb, 0, 0)),
            vec(cols), vec(wb), full(w2p), vec(wb), full(a2p), full(g2), vec(wb), vec(wb), vec(wb),
        ],
        out_specs=[out_spec] * 8,
        out_shape=[out_sds] * 8,
        scratch_shapes=[pltpu.VMEM((1, cols), F32)],
        compiler_params=_params(("parallel", "arbitrary")),
        name="rwkv_prep",
    )(pb, prev_row, mu, w0, w2p, a0, a2p, g2, k_k, k_a, r_k)


def _stack2(x, first):
    xb = x.astype(BF16)
    zero = jnp.zeros_like(xb)
    return jnp.concatenate([jnp.where(first, xb, zero), jnp.where(first, zero, xb)], axis=0)


def _rwkv_chunk_kernel(r_ref, k_ref, v_ref, ld_ref, na_ref, b_ref, g_ref, bon_ref, s0_ref, lnw_ref, lnb_ref,
                       y_ref, sT_ref, st_ref, *, cl):
    c = pl.program_id(1)
    n_chunks = pl.num_programs(1)
    n_slab = r_ref.shape[0]
    n_sub = r_ref.shape[1] // cl

    @pl.when(c == 0)
    def _():
        st_ref[...] = s0_ref[...]

    lane = lax.broadcasted_iota(jnp.int32, (1, LANES), 1)
    first = lane < HEAD_B
    ti = lax.broadcasted_iota(jnp.int32, (cl, cl), 0)
    tj = lax.broadcasted_iota(jnp.int32, (cl, cl), 1)
    tri_incl = jnp.where(ti >= tj, 1.0, 0.0).astype(BF16)
    seg = _seg_ones()
    n2 = 2 * cl
    ri = lax.broadcasted_iota(jnp.int32, (n2, n2), 0) % cl
    ci = lax.broadcasted_iota(jnp.int32, (n2, n2), 1) % cl
    strict = ri > ci
    incl = ri >= ci
    zero = jnp.zeros((n2, n2), F32)
    inv_n = 1.0 / HEAD_B

    items = [(s, q) for q in range(n_sub) for s in range(n_slab)]
    rows = lambda q: slice(q * cl, (q + 1) * cl)

    pre = []
    for s, q in items:
        ld = ld_ref[s, rows(q), :]
        hi, mid, lo = _split3(ld)
        cum = _dot(tri_incl, hi) + (_dot(tri_incl, mid) + _dot(tri_incl, lo))
        pre.append((ld, cum))

    stacks = []
    for (s, q), (ld, cum) in zip(items, pre):
        cum_last = cum[cl - 1:cl, :]
        g_incl = jnp.exp(cum)
        g_excl = jnp.exp(cum - ld)
        g_inv = jnp.exp(-cum)
        g_end = jnp.exp(cum_last - cum)
        k = k_ref[s, rows(q), :]
        bv = b_ref[s, rows(q), :]
        stacks.append(dict(
            g_last=jnp.exp(cum_last),
            la=_stack2(na_ref[s, rows(q), :] * g_excl, first),
            lr=_stack2(r_ref[s, rows(q), :] * g_incl, first),
            l2=jnp.concatenate([_stack2(bv * g_inv, first), _stack2(k * g_inv, first)], axis=0),
            lh=jnp.concatenate([_stack2(bv * g_end, first), _stack2(k * g_end, first)], axis=0),
            v2=_stack2(v_ref[s, rows(q), :], first)))

    grams = [_dot(jnp.concatenate([d["la"], d["lr"]], axis=0), d["l2"], NT) for d in stacks]
    for d, gram in zip(stacks, grams):
        d["g_ak"] = jnp.where(strict, gram[:n2, n2:], zero).astype(BF16)
        d["g_r"] = jnp.concatenate([jnp.where(incl, gram[n2:, :n2], zero),
                                    jnp.where(incl, gram[n2:, n2:], zero)], axis=1).astype(BF16)
    rr = lax.broadcasted_iota(jnp.int32, (n2, n2), 0)
    cc = lax.broadcasted_iota(jnp.int32, (n2, n2), 1)
    eye = jnp.where(rr == cc, 1.0, 0.0).astype(F32)
    pows = [jnp.where(strict, gram[:n2, :n2], zero) for gram in grams]
    invs = [eye + p for p in pows]
    for _ in range(int(math.log2(cl)) - 1):
        pows = [_dot(p.astype(BF16), p.astype(BF16)) for p in pows]
        invs = [inv + _dot(inv.astype(BF16), p.astype(BF16)) for inv, p in zip(invs, pows)]
    akv = [_dot(d["g_ak"], d["v2"]) for d in stacks]

    sts = [st_ref[s] for s in range(n_slab)]
    for q in range(n_sub):
        base = q * n_slab
        stb = [st.astype(BF16) for st in sts]
        xs = [_dot(stacks[base + s]["la"], stb[s], NT) + akv[base + s] for s in range(n_slab)]
        us = [_dot(invs[base + s].astype(BF16), xs[s].astype(BF16)) for s in range(n_slab)]
        uvs = [jnp.concatenate([us[s].astype(BF16), stacks[base + s]["v2"]], axis=0) for s in range(n_slab)]
        y2s = [_dot(stacks[base + s]["lr"], stb[s], NT) + _dot(stacks[base + s]["g_r"], uvs[s])
               for s in range(n_slab)]
        sts = [sts[s] * stacks[base + s]["g_last"] + _dot(uvs[s], stacks[base + s]["lh"], TN)
               for s in range(n_slab)]
        ys = [y2[:cl, :] + y2[cl:, :] for y2 in y2s]
        means = [_mm_exact_rhs2(y, seg) * inv_n for y in ys]
        ds = [y - m for y, m in zip(ys, means)]
        vars_ = [_mm_exact_rhs2(d * d, seg) * inv_n for d in ds]
        for s in range(n_slab):
            sl = slice(s * LANES, (s + 1) * LANES)
            yn = ds[s] * lax.rsqrt(vars_[s] + GN_EPS) * lnw_ref[:, sl] + lnb_ref[:, sl]
            y_ref[s, rows(q), :] = ((yn + bon_ref[s, rows(q), :]) * g_ref[s, rows(q), :]).astype(y_ref.dtype)
    for s in range(n_slab):
        st_ref[s] = sts[s]

    @pl.when(c == n_chunks - 1)
    def _():
        sT_ref[...] = st_ref[...]


def _rwkv_chunks(prep, s0_bd, lnx_w, lnx_b):
    b, n_slab, t, _ = prep[0].shape
    cl = min(CHUNK_B, t)
    step = min(CHUNKS_PER_STEP * cl, t)
    wb = n_slab * LANES
    in_spec = pl.BlockSpec((None, n_slab, step, LANES), lambda bb, c: (bb, 0, c, 0))
    st_spec = pl.BlockSpec((None, n_slab, LANES, LANES), lambda bb, c: (bb, 0, 0, 0))
    vec = pl.BlockSpec((1, wb), lambda bb, c: (0, 0))
    return pl.pallas_call(
        functools.partial(_rwkv_chunk_kernel, cl=cl),
        grid=(b, t // step),
        in_specs=[in_spec] * 8 + [st_spec, vec, vec],
        out_specs=[in_spec, st_spec],
        out_shape=[
            jax.ShapeDtypeStruct((b, n_slab, t, LANES), BF16),
            jax.ShapeDtypeStruct((b, n_slab, LANES, LANES), F32),
        ],
        scratch_shapes=[pltpu.VMEM((n_slab, LANES, LANES), F32)],
        compiler_params=_params(("parallel", "arbitrary")),
        name="rwkv_chunks",
    )(*prep, s0_bd, lnx_w, lnx_b)


def _state_to_blockdiag(s):
    b, h, n, _ = s.shape
    s = s.reshape(b, h // 2, 2, n, n)
    z = jnp.zeros_like(s[:, :, 0])
    top = jnp.concatenate([s[:, :, 0], z], axis=-1)
    bot = jnp.concatenate([z, s[:, :, 1]], axis=-1)
    return jnp.concatenate([top, bot], axis=-2)


def _state_from_blockdiag(sbd):
    b, hs, n2, _ = sbd.shape
    n = n2 // 2
    return jnp.stack([sbd[:, :, :n, :n], sbd[:, :, n:, n:]], axis=2).reshape(b, 2 * hs, n, n)


def _route(logits, n_groups, n_per_group):
    ne = n_groups * n_per_group
    lane = lax.broadcasted_iota(jnp.int32, logits.shape, 1)
    big = jnp.int32(1 << 30)
    is_grp = (lane >= ne) & (lane < ne + n_groups)
    lg = jnp.where(is_grp, logits, NEG_INF)
    mg = jnp.max(lg, axis=-1, keepdims=True)
    gi = jnp.min(jnp.where(is_grp & (lg == mg), lane, big), axis=-1, keepdims=True) - ne
    eg = jnp.where(is_grp, jnp.exp(lg - mg), 0.0)
    p_sel = 1.0 / jnp.sum(eg, axis=-1, keepdims=True)
    in_grp = (lane < ne) & ((lane // n_per_group) == gi)
    le = jnp.where(in_grp, logits, NEG_INF)
    me = jnp.max(le, axis=-1, keepdims=True)
    ee = jnp.where(in_grp, jnp.exp(le - me), 0.0)
    pe = ee / jnp.sum(ee, axis=-1, keepdims=True)
    v1 = jnp.max(jnp.where(in_grp, pe, -1.0), axis=-1, keepdims=True)
    i1 = jnp.min(jnp.where(in_grp & (pe == v1), lane, big), axis=-1, keepdims=True)
    rest = in_grp & (lane != i1)
    v2 = jnp.max(jnp.where(rest, pe, -1.0), axis=-1, keepdims=True)
    i2 = jnp.min(jnp.where(rest & (pe == v2), lane, big), axis=-1, keepdims=True)
    tot = v1 + v2
    w1 = (v1 / tot) * p_sel
    w2 = (v2 / tot) * p_sel
    return (jnp.where(lane == 0, w1, 0.0) + jnp.where(lane == 1, w2, 0.0)
            + jnp.where(lane == 2, i1.astype(F32), 0.0) + jnp.where(lane == 3, i2.astype(F32), 0.0))


def _mix_kernel(x_ref, oa_ref, ob_ref, gt_ref, wa_ref, wb_ref, wo_ref, gn_ref, wr_ref, br_ref,
                x2_ref, hn_ref, route_ref, route_t_ref, *, n_groups, n_per_group):
    d = x_ref.shape[1]
    nb, n_slab, t_blk, _ = ob_ref.shape
    y_a = _dot(oa_ref[...], wa_ref[...])
    ob = jnp.concatenate([ob_ref[:, s].reshape(nb * t_blk, LANES) for s in range(n_slab)], axis=1)
    y_b = _dot(ob, wb_ref[...])
    merged = gt_ref[:, :d] * y_a + gt_ref[:, d:] * y_b
    x2 = x_ref[...] + _dot(merged.astype(BF16), wo_ref[...])
    x2_ref[...] = x2
    hn = _rmsnorm(x2, gn_ref[...], NORM_EPS)
    hn_ref[...] = hn
    logits = _mm(hn, wr_ref[...], 3) + br_ref[...]
    route = _route(logits, n_groups, n_per_group)
    route_ref[...] = route
    route_t_ref[...] = route.T[:ROUTE_ROWS, :]


def _mix(x2d, o_a, o_b, gates, w_a, w_b, w_o, gain_ffn, w_route, b_route, *, n_groups, n_per_group):
    b, n_slab, t, _ = o_b.shape
    n, d = x2d.shape
    if t >= 512:
        nb, t_blk = 1, 512
    else:
        nb, t_blk = min(b, 512 // t), t
    tm = nb * t_blk
    nt = t // t_blk
    full = lambda a: pl.BlockSpec(a.shape, lambda i: (0,) * a.ndim)
    return pl.pallas_call(
        functools.partial(_mix_kernel, n_groups=n_groups, n_per_group=n_per_group),
        grid=(n // tm,),
        in_specs=[
            pl.BlockSpec((tm, d), lambda i: (i, 0)),
            pl.BlockSpec((tm, o_a.shape[1]), lambda i: (i, 0)),
            pl.BlockSpec((nb, n_slab, t_blk, LANES), lambda i: (i // nt, 0, i % nt, 0)),
            pl.BlockSpec((tm, 2 * d), lambda i: (i, 0)),
            full(w_a), full(w_b), full(w_o), full(gain_ffn), full(w_route), full(b_route),
        ],
        out_specs=[
            pl.BlockSpec((tm, d), lambda i: (i, 0)),
            pl.BlockSpec((tm, d), lambda i: (i, 0)),
            pl.BlockSpec((tm, LANES), lambda i: (i, 0)),
            pl.BlockSpec((ROUTE_ROWS, tm), lambda i: (0, i)),
        ],
        out_shape=[
            jax.ShapeDtypeStruct((n, d), F32),
            jax.ShapeDtypeStruct((n, d), F32),
            jax.ShapeDtypeStruct((n, LANES), F32),
            jax.ShapeDtypeStruct((ROUTE_ROWS, n), F32),
        ],
        compiler_params=_params(("parallel",)),
        name="mix_route",
    )(x2d, o_a, o_b, gates, w_a, w_b, w_o, gain_ffn, w_route, b_route)


def _rank_kernel(rt_ref, ri_ref, cnt_ref, carry_ref, *, ne_pad):
    i = pl.program_id(0)

    @pl.when(i == 0)
    def _():
        carry_ref[...] = jnp.zeros_like(carry_ref)

    rt = rt_ref[...]
    tt = rt.shape[1]
    e1 = rt[2:3, :].astype(jnp.int32)
    e2 = rt[3:4, :].astype(jnp.int32)
    eid = lax.broadcasted_iota(jnp.int32, (ne_pad, tt), 0)
    m1 = eid == e1
    m2 = eid == e2
    member = jnp.where(m1 | m2, 1.0, 0.0)
    before = jnp.where(lax.broadcasted_iota(jnp.int32, (tt, tt), 0) < lax.broadcasted_iota(jnp.int32, (tt, tt), 1),
                       1.0, 0.0).astype(BF16)
    carry = carry_ref[...]
    rank = _dot(member.astype(BF16), before) + carry[:, :1]
    r1 = jnp.sum(jnp.where(m1, rank, 0.0), axis=0, keepdims=True).astype(jnp.int32)
    r2 = jnp.sum(jnp.where(m2, rank, 0.0), axis=0, keepdims=True).astype(jnp.int32)
    ri_ref[...] = jnp.concatenate([e1, e2, r1, r2, jnp.zeros((ROUTE_ROWS - 4, tt), jnp.int32)], axis=0)
    carry = carry + jnp.sum(member, axis=1, keepdims=True)
    carry_ref[...] = carry

    @pl.when(i == pl.num_programs(0) - 1)
    def _():
        cnt_ref[...] = carry


def _rank(route_t, ne_pad):
    _, n = route_t.shape
    tt = min(512, n)
    return pl.pallas_call(
        functools.partial(_rank_kernel, ne_pad=ne_pad),
        grid=(n // tt,),
        in_specs=[pl.BlockSpec((ROUTE_ROWS, tt), lambda i: (0, i))],
        out_specs=[pl.BlockSpec((ROUTE_ROWS, tt), lambda i: (0, i)),
                   pl.BlockSpec((ne_pad, LANES), lambda i: (0, 0))],
        out_shape=[jax.ShapeDtypeStruct((ROUTE_ROWS, n), jnp.int32),
                   jax.ShapeDtypeStruct((ne_pad, LANES), F32)],
        scratch_shapes=[pltpu.VMEM((ne_pad, LANES), F32)],
        compiler_params=_params(("arbitrary",)),
        name="moe_rank",
    )(route_t)


def _row_copy(src, src_row, dst, dst_row, sem):
    return pltpu.make_async_copy(src.at[pl.ds(src_row, 1)], dst.at[pl.ds(dst_row, 1)], sem)


def _scatter_kernel(starts_ref, ri_ref, hn_ref, xs_in_ref, xs_ref, sem):
    del xs_in_ref
    tt = ri_ref.shape[1]

    def issue(t, carry):
        p1 = starts_ref[ri_ref[0, t]] + ri_ref[2, t]
        p2 = starts_ref[ri_ref[1, t]] + ri_ref[3, t]
        _row_copy(hn_ref, t, xs_ref, p1, sem).start()
        _row_copy(hn_ref, t, xs_ref, p2, sem).start()
        return carry

    lax.fori_loop(0, tt, issue, 0, unroll=8)

    def drain(t, carry):
        _row_copy(hn_ref, 0, xs_ref, 0, sem).wait()
        _row_copy(hn_ref, 0, xs_ref, 0, sem).wait()
        return carry

    lax.fori_loop(0, tt, drain, 0, unroll=8)


def _scatter(starts, route_i, hn, n_rows):
    n, d = hn.shape
    tt = min(512, n)
    xs0 = jnp.zeros((n_rows, d), hn.dtype)
    return pl.pallas_call(
        _scatter_kernel,
        grid_spec=pltpu.PrefetchScalarGridSpec(
            num_scalar_prefetch=1,
            grid=(n // tt,),
            in_specs=[
                pl.BlockSpec((ROUTE_ROWS, tt), lambda i, st: (0, i), memory_space=pltpu.SMEM),
                pl.BlockSpec((tt, d), lambda i, st: (i, 0)),
                pl.BlockSpec(memory_space=pl.ANY),
            ],
            out_specs=pl.BlockSpec(memory_space=pl.ANY),
            scratch_shapes=[pltpu.SemaphoreType.DMA(())],
        ),
        out_shape=jax.ShapeDtypeStruct((n_rows, d), hn.dtype),
        input_output_aliases={3: 0},
        compiler_params=_params(("arbitrary",)),
        name="moe_scatter",
    )(starts, route_i, hn, xs0)


def _expert_kernel(te_ref, nu_ref, xs_ref, w1_ref, w3_ref, w2_ref, o_ref):
    i = pl.program_id(0)

    @pl.when(i < nu_ref[0])
    def _():
        x = xs_ref[...].astype(BF16)
        up = _dot(x, w1_ref[...])
        lin = _dot(x, w3_ref[...])
        z = (up * _sigmoid(up)) * lin
        o_ref[...] = _dot(z.astype(BF16), w2_ref[...])

    @pl.when(i >= nu_ref[0])
    def _():
        o_ref[...] = jnp.zeros_like(o_ref)


def _experts(tile_expert, n_used, xs, w1, w3, w2):
    n_rows, d = xs.shape
    _, _, f = w1.shape
    tm = EXPERT_TILE
    last = lambda i, nu: jnp.minimum(i, jnp.maximum(nu[0] - 1, 0))
    return pl.pallas_call(
        _expert_kernel,
        grid_spec=pltpu.PrefetchScalarGridSpec(
            num_scalar_prefetch=2,
            grid=(n_rows // tm,),
            in_specs=[
                pl.BlockSpec((tm, d), lambda i, te, nu: (last(i, nu), 0)),
                pl.BlockSpec((None, d, f), lambda i, te, nu: (te[i], 0, 0)),
                pl.BlockSpec((None, d, f), lambda i, te, nu: (te[i], 0, 0)),
                pl.BlockSpec((None, f, d), lambda i, te, nu: (te[i], 0, 0)),
            ],
            out_specs=pl.BlockSpec((tm, d), lambda i, te, nu: (i, 0)),
        ),
        out_shape=jax.ShapeDtypeStruct((n_rows, d), F32),
        compiler_params=_params(("arbitrary",)),
        name="moe_experts",
    )(tile_expert, n_used, xs, w1, w3, w2)


def _combine_kernel(starts_ref, ri_ref, rf_ref, x2_ref, gf_ref, o_ref, y_ref, buf1, buf2, sem):
    tt = ri_ref.shape[1]

    def issue(t, carry):
        p1 = starts_ref[ri_ref[0, t]] + ri_ref[2, t]
        p2 = starts_ref[ri_ref[1, t]] + ri_ref[3, t]
        _row_copy(o_ref, p1, buf1, t, sem).start()
        _row_copy(o_ref, p2, buf2, t, sem).start()
        return carry

    lax.fori_loop(0, tt, issue, 0, unroll=8)

    def drain(t, carry):
        _row_copy(o_ref, 0, buf1, 0, sem).wait()
        _row_copy(o_ref, 0, buf2, 0, sem).wait()
        return carry

    lax.fori_loop(0, tt, drain, 0, unroll=8)
    rf = rf_ref[...]
    y = x2_ref[...] + rf[:, 0:1] * buf1[...] + rf[:, 1:2] * buf2[...]
    y_ref[...] = _rmsnorm(y, gf_ref[...], NORM_EPS)


def _combine(starts, route_i, route_f, x2, gain_final, o):
    n, d = x2.shape
    tt = min(256, n)
    return pl.pallas_call(
        _combine_kernel,
        grid_spec=pltpu.PrefetchScalarGridSpec(
            num_scalar_prefetch=1,
            grid=(n // tt,),
            in_specs=[
                pl.BlockSpec((ROUTE_ROWS, tt), lambda i, st: (0, i), memory_space=pltpu.SMEM),
                pl.BlockSpec((tt, LANES), lambda i, st: (i, 0)),
                pl.BlockSpec((tt, d), lambda i, st: (i, 0)),
                pl.BlockSpec((1, d), lambda i, st: (0, 0)),
                pl.BlockSpec(memory_space=pl.ANY),
            ],
            out_specs=pl.BlockSpec((tt, d), lambda i, st: (i, 0)),
            scratch_shapes=[pltpu.VMEM((tt, d), F32), pltpu.VMEM((tt, d), F32), pltpu.SemaphoreType.DMA(())],
        ),
        out_shape=jax.ShapeDtypeStruct((n, d), F32),
        compiler_params=_params(("arbitrary",)),
        name="moe_combine",
    )(starts, route_i, route_f, x2, gain_final, o)


def _moe(hn, route_f, route_t, w1, w3, w2, x2, gain_final):
    n, d = hn.shape
    ne = w1.shape[0]
    ne_pad = -(-ne // 8) * 8
    tm = EXPERT_TILE
    n_tiles = -(-2 * n // tm) + ne
    route_i, counts = _rank(route_t, ne_pad)
    counts = counts[:ne, 0].astype(jnp.int32)
    tiles = (counts + tm - 1) // tm
    ends = jnp.cumsum(tiles)
    starts = (ends - tiles) * tm
    n_used = ends[-1:]
    tile_expert = jnp.minimum(jnp.sum(jnp.arange(n_tiles)[:, None] >= ends[None, :], axis=1), ne - 1)
    xs = _scatter(starts, route_i, hn, n_tiles * tm)
    o = _experts(tile_expert.astype(jnp.int32), n_used.astype(jnp.int32), xs, w1, w3, w2)
    return _combine(starts, route_i, route_f, x2, gain_final, o)


def _run_path(x, prev_row, s0, cache, wts, dims):
    b, t, d = x.shape
    x2d = x.reshape(b * t, d)
    q_scale = dims["da"] ** -0.5 * math.log2(math.e)
    pb = _norm_mm(x2d, wts["norm_mix"], wts["w_pb"], False, "rwkv_proj")
    gates = _norm_mm(x2d, wts["norm_mix"], wts["w_gates"], True, "gate_proj")

    if cache is None:
        qkv, kf, vf, qt, vt = _qkv_proj(x2d, wts["norm_mix"], wts["w_qkv"], q_scale, True)
        qkv3 = qkv.reshape(b, t, -1)
        tiles = lambda a: a.reshape(b, t // a.shape[2], a.shape[1], a.shape[2])
        o_a = _attn_prompt(qkv3, tiles(qt), tiles(vt), wts["lams"], wts["subln_w"].reshape(-1, 1),
                           n_heads=dims["ha"], chunk=dims["chunk"], lam_init=dims["lam_init"])
    else:
        qkv, kf, vf = _qkv_proj(x2d, wts["norm_mix"], wts["w_qkv"], q_scale, False)
        qkv3 = qkv.reshape(b, t, -1)
        o_a = _attn_sample(qkv3, cache[0], cache[1], wts["lams"], wts["subln_w"], n_heads=dims["ha"],
                           lam_init=dims["lam_init"])

    pb3 = pb.reshape(b, t, -1)
    prep = _rwkv_prep(pb3, prev_row, wts["mu"], wts["w0"], wts["w2p"], wts["a0"], wts["a2p"], wts["g2"],
                      wts["k_k"], wts["k_a"], wts["r_k"], wb=dims["wb"])
    o_b, s_new = _rwkv_chunks(prep, _state_to_blockdiag(s0), wts["lnx_w"], wts["lnx_b"])

    x2, hn, route_f, route_t = _mix(x2d, o_a.reshape(b * t, -1), o_b, gates, wts["w_a_out"], wts["w_b_out"],
                                    wts["w_o"], wts["norm_ffn"], wts["w_route"], wts["b_route"],
                                    n_groups=dims["n_groups"], n_per_group=dims["n_per_group"])
    y = _moe(hn, route_f, route_t, wts["moe_w1"], wts["moe_w3"], wts["moe_w2"], x2, wts["norm_final"])

    ha = dims["ha"]
    return (y.reshape(b, t, d), kf.reshape(1, b, t, ha, -1), vf.reshape(1, b, t, ha, -1),
            pb3[:, t - 1:t, :][None], _state_from_blockdiag(s_new)[None])


def kernel(x_prompt, x_sample, cache_attn_k, cache_attn_v, state_rwkv_shift, state_rwkv_wkv, norm_mix, w_in, lambda_q1, lambda_k1, lambda_q2, lambda_k2, subln_w, w_a_out, rwkv_mu, rwkv_w0, rwkv_w2, rwkv_a0, rwkv_a2, rwkv_g2, rwkv_k_k, rwkv_k_a, rwkv_r_k, rwkv_lnx_w, rwkv_lnx_b, w_b_out, w_o, norm_ffn, moe_w_group, moe_b_group, moe_w_router, moe_b_router, moe_w1, moe_w3, moe_w2, norm_final):
    assert w_in.shape[0] == 1, "single-layer trunk"
    l = 0
    d = x_prompt.shape[-1]
    ha, dva = cache_attn_k.shape[3], cache_attn_v.shape[4]
    wa = ha * dva
    hb, db = state_rwkv_wkv.shape[2], state_rwkv_wkv.shape[3]
    wb = hb * db
    lora_w, lora_a, lora_g = rwkv_w2.shape[1], rwkv_a2.shape[1], rwkv_g2.shape[1]
    rwkv_cols = 3 * wb + lora_w + lora_a + lora_g
    n_groups = moe_w_group.shape[-1]
    n_per_group = moe_w_router.shape[-1] // n_groups
    ne = n_groups * n_per_group
    assert db == HEAD_B and lora_w + lora_a == LANES and lora_g == LANES and dva == LANES
    assert ne + n_groups <= LANES

    row = lambda v: v.reshape(1, -1).astype(F32)
    w_l = w_in[l]
    zeros_w = jnp.zeros((lora_a, wb), F32)
    zeros_a = jnp.zeros((lora_w, wb), F32)
    w_route = jnp.zeros((d, LANES), F32)
    w_route = w_route.at[:, :ne].set(moe_w_router[l]).at[:, ne:ne + n_groups].set(moe_w_group[l])
    b_route = jnp.zeros((1, LANES), F32)
    b_route = b_route.at[0, :ne].set(moe_b_router[l]).at[0, ne:ne + n_groups].set(moe_b_group[l])
    f = moe_w1.shape[-1]
    wts = dict(
        norm_mix=row(norm_mix[l]),
        w_qkv=w_l[:, :3 * wa].astype(BF16),
        w_pb=w_l[:, 3 * wa:3 * wa + rwkv_cols].astype(BF16),
        w_gates=w_l[:, 3 * wa + rwkv_cols:].astype(BF16),
        lams=[row(lambda_q1[l]), row(lambda_k1[l]), row(lambda_q2[l]), row(lambda_k2[l])],
        subln_w=row(subln_w[l]),
        mu=row(rwkv_mu[l]), w0=row(rwkv_w0[l]), a0=row(rwkv_a0[l]),
        w2p=jnp.concatenate([rwkv_w2[l], zeros_w], axis=0),
        a2p=jnp.concatenate([zeros_a, rwkv_a2[l]], axis=0),
        g2=rwkv_g2[l].astype(F32),
        k_k=row(rwkv_k_k[l]), k_a=row(rwkv_k_a[l]), r_k=row(rwkv_r_k[l]),
        lnx_w=row(rwkv_lnx_w[l]), lnx_b=row(rwkv_lnx_b[l]),
        w_a_out=w_a_out[l].astype(BF16), w_b_out=w_b_out[l].astype(BF16), w_o=w_o[l].astype(BF16),
        norm_ffn=row(norm_ffn[l]), w_route=w_route, b_route=b_route,
        moe_w1=moe_w1[l].reshape(ne, d, f).astype(BF16),
        moe_w3=moe_w3[l].reshape(ne, d, f).astype(BF16),
        moe_w2=moe_w2[l].reshape(ne, f, d).astype(BF16),
        norm_final=row(norm_final),
    )
    dims = dict(ha=ha, da=dva // 2, wb=wb, chunk=CHUNK_B, lam_init=0.8 - 0.6 * math.exp(-0.3 * l),
                n_groups=n_groups, n_per_group=n_per_group)

    bp = x_prompt.shape[0]
    yp, kp, vp, shp, wkp = _run_path(
        x_prompt.astype(F32), jnp.zeros((bp, 1, rwkv_cols), F32), jnp.zeros((bp, hb, db, db), F32), None, wts, dims)

    bs, past = cache_attn_k.shape[1], cache_attn_k.shape[2]
    cache = (cache_attn_k[l].reshape(bs, past * ha, dva), cache_attn_v[l].reshape(bs, past * ha, dva))
    ys, ks_, vs_, shs, wks = _run_path(
        x_sample.astype(F32), state_rwkv_shift[l].astype(F32), state_rwkv_wkv[l].astype(F32), cache, wts, dims)

    return (yp, ys, kp, vp, shp, wkp, ks_, vs_, shs, wks)
```

```python
import functools
import math

import jax
import jax.numpy as jnp
from jax import lax
from jax.experimental import pallas as pl
from jax.experimental.pallas import tpu as pltpu

F32 = jnp.float32
BF16 = jnp.bfloat16

LANES = 128
HEAD_B = 64
CHUNK_B = 64
CHUNKS_PER_STEP = 2
ATTN_TILE = 512
ROW_TILE = 8
ROUTE_ROWS = 8
EXPERT_TILE = 256
NORM_EPS = 1e-6
DIFF_EPS = 1e-5
GN_EPS = 64e-5
NEG_INF = -1e30
VMEM_LIMIT = 56 * 1024 * 1024

NN = (((1,), (0,)), ((), ()))
NT = (((1,), (1,)), ((), ()))
TN = (((0,), (0,)), ((), ()))


def _dot(a, b, dims=NN):
    return lax.dot_general(a, b, dims, preferred_element_type=F32)


def _split2(a):
    hi = a.astype(BF16)
    lo = (a - hi.astype(F32)).astype(BF16)
    return hi, lo


def _split3(a):
    hi = a.astype(BF16)
    r = a - hi.astype(F32)
    mid = r.astype(BF16)
    lo = (r - mid.astype(F32)).astype(BF16)
    return hi, mid, lo


def _mm(a, b, passes=1, dims=NN):
    if passes == 1:
        return _dot(a.astype(BF16), b.astype(BF16), dims)
    a_hi, a_lo = _split2(a)
    b_hi, b_lo = _split2(b)
    return _dot(a_hi, b_hi, dims) + (_dot(a_hi, b_lo, dims) + _dot(a_lo, b_hi, dims))


def _mm_exact_rhs(a, b_bf16, dims=NN):
    hi, mid, lo = _split3(a)
    return _dot(hi, b_bf16, dims) + (_dot(mid, b_bf16, dims) + _dot(lo, b_bf16, dims))


def _mm_exact_rhs2(a, b_bf16, dims=NN):
    hi, lo = _split2(a)
    return _dot(hi, b_bf16, dims) + _dot(lo, b_bf16, dims)


def _rmsnorm(x, g, eps):
    return x * lax.rsqrt(jnp.mean(x * x, axis=-1, keepdims=True) + eps) * g


def _sigmoid(x):
    return 1.0 / (1.0 + jnp.exp(-x))


def _params(sem):
    return pltpu.CompilerParams(dimension_semantics=sem, vmem_limit_bytes=VMEM_LIMIT)


def _qkv_kernel(x_ref, g_ref, w_ref, qkv_ref, kf_ref, vf_ref, *rest, q_scale, transposed):
    if transposed:
        qt_ref, vt_ref, h_ref = rest
    else:
        (h_ref,) = rest
    j = pl.program_id(1)

    @pl.when(j == 0)
    def _():
        h_ref[...] = _rmsnorm(x_ref[...], g_ref[...], NORM_EPS).astype(BF16)

    p = _dot(h_ref[...], w_ref[...])

    @pl.when(j == 0)
    def _():
        q = p * q_scale
        qkv_ref[...] = q.astype(BF16)
        if transposed:
            qt_ref[...] = q.T.astype(BF16)

    @pl.when(j == 1)
    def _():
        qkv_ref[...] = p.astype(BF16)
        kf_ref[...] = p

    @pl.when(j == 2)
    def _():
        qkv_ref[...] = p.astype(BF16)
        vf_ref[...] = p
        if transposed:
            vt_ref[...] = p.T.astype(BF16)


def _qkv_proj(x2d, gain, w_qkv, q_scale, transposed):
    n, d = x2d.shape
    wa = w_qkv.shape[1] // 3
    tm = min(ATTN_TILE, n)
    out_specs = [
        pl.BlockSpec((tm, wa), lambda i, j: (i, j)),
        pl.BlockSpec((tm, wa), lambda i, j: (i, 0)),
        pl.BlockSpec((tm, wa), lambda i, j: (i, 0)),
    ]
    out_shape = [
        jax.ShapeDtypeStruct((n, 3 * wa), BF16),
        jax.ShapeDtypeStruct((n, wa), F32),
        jax.ShapeDtypeStruct((n, wa), F32),
    ]
    if transposed:
        out_specs += [pl.BlockSpec((None, wa, tm), lambda i, j: (i, 0, 0))] * 2
        out_shape += [jax.ShapeDtypeStruct((n // tm, wa, tm), BF16)] * 2
    return pl.pallas_call(
        functools.partial(_qkv_kernel, q_scale=q_scale, transposed=transposed),
        grid=(n // tm, 3),
        in_specs=[
            pl.BlockSpec((tm, d), lambda i, j: (i, 0)),
            pl.BlockSpec((1, d), lambda i, j: (0, 0)),
            pl.BlockSpec((d, wa), lambda i, j: (0, j)),
        ],
        out_specs=out_specs,
        out_shape=out_shape,
        scratch_shapes=[pltpu.VMEM((tm, d), BF16)],
        compiler_params=_params(("parallel", "arbitrary")),
        name="qkv_proj",
    )(x2d, gain, w_qkv)


def _gate_kernel(x_ref, g_ref, w_ref, o_ref):
    h = _rmsnorm(x_ref[...], g_ref[...], NORM_EPS).astype(BF16)
    o_ref[...] = _sigmoid(_dot(h, w_ref[...])).astype(o_ref.dtype)


def _gate_proj(x2d, gain, w):
    n, d = x2d.shape
    c = w.shape[1]
    tm = min(512, n)
    return pl.pallas_call(
        _gate_kernel,
        grid=(n // tm,),
        in_specs=[
            pl.BlockSpec((tm, d), lambda i: (i, 0)),
            pl.BlockSpec((1, d), lambda i: (0, 0)),
            pl.BlockSpec((d, c), lambda i: (0, 0)),
        ],
        out_specs=pl.BlockSpec((tm, c), lambda i: (i, 0)),
        out_shape=jax.ShapeDtypeStruct((n, c), BF16),
        compiler_params=_params(("parallel",)),
        name="gate_proj",
    )(x2d, gain, w)


def _lambda(lq1_ref, lk1_ref, lq2_ref, lk2_ref, lam_init):
    s1 = jnp.sum(lq1_ref[...] * lk1_ref[...], axis=-1, keepdims=True)
    s2 = jnp.sum(lq2_ref[...] * lk2_ref[...], axis=-1, keepdims=True)
    return jnp.exp(s1) - jnp.exp(s2) + lam_init


def _split_q(q):
    lane = lax.broadcasted_iota(jnp.int32, (1, q.shape[1]), 1)
    first = lane < (q.shape[1] // 2)
    zero = jnp.zeros_like(q)
    return jnp.where(first, q, zero), jnp.where(first, zero, q)


def _softmax_block(s, m, l, acc, v):
    m_new = jnp.maximum(m, jnp.max(s, axis=-1, keepdims=True))
    alpha = jnp.exp2(m - m_new)
    p = jnp.exp2(s - m_new)
    l_new = alpha * l + jnp.sum(p, axis=-1, keepdims=True)
    acc_new = alpha * acc + _dot(p.astype(BF16), v)
    return m_new, l_new, acc_new


def _diff_finish(acc1, l1, acc2, l2, lam, sub_w, lam_init):
    o = acc1 / l1 - lam * (acc2 / l2)
    return _rmsnorm(o, sub_w, DIFF_EPS) * (1.0 - lam_init)


def _attn_prompt_kernel(lq1_ref, lk1_ref, lq2_ref, lk2_ref, subc_ref, qt_ref, k_ref, vt_ref, o_ref,
                        s0_ref, s1_ref, s2_ref, s3_ref, mt_ref, m_ref, l_ref, acc_ref, *, chunk, lam_init):
    i = pl.program_id(2)
    dv, tq = qt_ref.shape
    tk = vt_ref.shape[2]
    lam = _lambda(lq1_ref, lk1_ref, lq2_ref, lk2_ref, lam_init)
    qt = qt_ref[...]
    feat = lax.broadcasted_iota(jnp.int32, (dv, 1), 0)
    zq = jnp.zeros_like(qt)
    q_maps = (jnp.where(feat < dv // 2, qt, zq), jnp.where(feat < dv // 2, zq, qt))
    acc_ref[...] = jnp.zeros_like(acc_ref)
    rb = 64
    kb = min(256, tk)

    m_ref[...] = jnp.full(m_ref.shape, NEG_INF, F32)
    l_ref[...] = jnp.zeros_like(l_ref)
    buf_a, buf_b = (s0_ref, s1_ref), (s2_ref, s3_ref)

    def scores(j, buf, masked):
        ks = k_ref[pl.ds(pl.multiple_of(j * tk, tk), tk), :]
        for mi in range(2):
            s = _dot(ks, q_maps[mi])
            if masked:
                kc = lax.broadcasted_iota(jnp.int32, (tk, tq), 0) // chunk
                qc = lax.broadcasted_iota(jnp.int32, (tk, tq), 1) // chunk
                s = jnp.where(kc <= qc, s, NEG_INF)
            buf[mi][...] = s
            mt_ref[2 * (buf is buf_b) + mi] = jnp.max(s, axis=0, keepdims=True)

    def softmax_pv(j, buf):
        vt = vt_ref[j]
        m_old = [m_ref[mi] for mi in range(2)]
        m_new = [jnp.maximum(m_old[mi], mt_ref[2 * (buf is buf_b) + mi]) for mi in range(2)]
        alpha = [jnp.exp2(m_old[mi] - m_new[mi]) for mi in range(2)]
        lsum = [jnp.zeros((8, tq), F32) for _ in range(2)]
        pv = [None, None]
        for hf in range(tk // kb):
            for mi in range(2):
                blocks = []
                for r in range(kb // rb):
                    lo = hf * kb + r * rb
                    p = jnp.exp2(buf[mi][lo:lo + rb, :] - m_new[mi])
                    lsum[mi] = lsum[mi] + jnp.sum(p.reshape(rb // 8, 8, tq), axis=0)
                    blocks.append(p.astype(BF16))
                part = _dot(vt[:, hf * kb:(hf + 1) * kb], jnp.concatenate(blocks, axis=0))
                pv[mi] = part if pv[mi] is None else pv[mi] + part
        for mi in range(2):
            acc_ref[mi] = acc_ref[mi] * alpha[mi] + pv[mi]
            m_ref[mi] = m_new[mi]
            l_ref[mi] = alpha[mi] * l_ref[mi] + lsum[mi]

    def stage_even(j, next_masked):
        scores(j + 1, buf_b, next_masked)
        softmax_pv(j, buf_a)

    def stage_odd(j, next_masked):
        scores(j + 1, buf_a, next_masked)
        softmax_pv(j, buf_b)

    @pl.when(i == 0)
    def _():
        scores(0, buf_a, True)
        softmax_pv(0, buf_a)

    @pl.when(i > 0)
    def _():
        scores(0, buf_a, False)

    def pair(jj, carry):
        stage_even(2 * jj, False)
        stage_odd(2 * jj + 1, False)
        return carry

    lax.fori_loop(0, jnp.maximum(i - 1, 0) // 2, pair, 0)

    @pl.when(i % 2 == 1)
    def _():
        stage_even(i - 1, True)
        softmax_pv(i, buf_b)

    @pl.when((i % 2 == 0) & (i > 0))
    def _():
        stage_even(i - 2, False)
        stage_odd(i - 1, True)
        softmax_pv(i, buf_a)

    l1 = jnp.sum(l_ref[0], axis=0, keepdims=True)
    l2 = jnp.sum(l_ref[1], axis=0, keepdims=True)
    ot = acc_ref[0] / l1 - lam * (acc_ref[1] / l2)
    ms = jnp.mean(ot * ot, axis=0, keepdims=True)
    ot = ot * lax.rsqrt(ms + DIFF_EPS) * subc_ref[...] * (1.0 - lam_init)
    o_ref[...] = ot.T.astype(o_ref.dtype)


def _attn_prompt(qkv, qt, vt, lams, sub_col, *, n_heads, chunk, lam_init):
    b, t, c3 = qkv.shape
    wa = c3 // 3
    dv = wa // n_heads
    nk, tile = qt.shape[1], qt.shape[3]
    lam_specs = [pl.BlockSpec((1, lams[0].shape[1]), lambda bb, h, i: (0, 0)) for _ in range(4)]
    return pl.pallas_call(
        functools.partial(_attn_prompt_kernel, chunk=chunk, lam_init=lam_init),
        grid=(b, n_heads, nk),
        in_specs=lam_specs + [
            pl.BlockSpec((dv, 1), lambda bb, h, i: (0, 0)),
            pl.BlockSpec((None, None, dv, tile), lambda bb, h, i: (bb, i, h, 0)),
            pl.BlockSpec((None, t, dv), lambda bb, h, i: (bb, 0, n_heads + h)),
            pl.BlockSpec((None, nk, dv, tile), lambda bb, h, i: (bb, 0, h, 0)),
        ],
        out_specs=pl.BlockSpec((None, tile, dv), lambda bb, h, i: (bb, i, h)),
        out_shape=jax.ShapeDtypeStruct((b, t, wa), BF16),
        scratch_shapes=[pltpu.VMEM((tile, tile), F32)] * 4
        + [pltpu.VMEM((4, 1, tile), F32),
           pltpu.VMEM((2, 1, tile), F32),
           pltpu.VMEM((2, 8, tile), F32),
           pltpu.VMEM((2, dv, tile), F32)],
        compiler_params=_params(("parallel", "parallel", "arbitrary")),
        name="diff_attn_prompt",
    )(*lams, sub_col, qt, qkv, vt)


def _attn_sample_kernel(lq1_ref, lk1_ref, lq2_ref, lk2_ref, sub_ref, q_ref, kn_ref, vn_ref, ck_ref, cv_ref,
                        o_ref, *, n_heads, lam_init):
    lam = _lambda(lq1_ref, lk1_ref, lq2_ref, lk2_ref, lam_init)
    tq = q_ref.shape[0]
    dv = q_ref.shape[1] // n_heads
    past = ck_ref.shape[0] // n_heads
    for h in range(n_heads):
        cols = slice(h * dv, (h + 1) * dv)
        q1, q2 = _split_q(q_ref[:, cols])
        ck = ck_ref[pl.ds(h, past, stride=n_heads), :].astype(BF16)
        cv = cv_ref[pl.ds(h, past, stride=n_heads), :].astype(BF16)
        kn = kn_ref[:, cols]
        vn = vn_ref[:, cols]
        outs = []
        for q in (q1, q2):
            m = jnp.full((tq, 1), NEG_INF, F32)
            l = jnp.zeros((tq, 1), F32)
            acc = jnp.zeros((tq, dv), F32)
            m, l, acc = _softmax_block(_dot(q, ck, NT), m, l, acc, cv)
            m, l, acc = _softmax_block(_dot(q, kn, NT), m, l, acc, vn)
            outs.append((acc, l))
        (a1, l1), (a2, l2) = outs
        o_ref[:, cols] = _diff_finish(a1, l1, a2, l2, lam, sub_ref[...], lam_init).astype(o_ref.dtype)


def _attn_sample(qkv, cache_k, cache_v, lams, sub_w, *, n_heads, lam_init):
    b, t, c3 = qkv.shape
    wa = c3 // 3
    dv = wa // n_heads
    rows = cache_k.shape[1]
    lam_specs = [pl.BlockSpec((1, lams[0].shape[1]), lambda bb: (0, 0)) for _ in range(4)]
    return pl.pallas_call(
        functools.partial(_attn_sample_kernel, n_heads=n_heads, lam_init=lam_init),
        grid=(b,),
        in_specs=lam_specs + [
            pl.BlockSpec((1, dv), lambda bb: (0, 0)),
            pl.BlockSpec((None, t, wa), lambda bb: (bb, 0, 0)),
            pl.BlockSpec((None, t, wa), lambda bb: (bb, 0, 1)),
            pl.BlockSpec((None, t, wa), lambda bb: (bb, 0, 2)),
            pl.BlockSpec((None, rows, dv), lambda bb: (bb, 0, 0)),
            pl.BlockSpec((None, rows, dv), lambda bb: (bb, 0, 0)),
        ],
        out_specs=pl.BlockSpec((None, t, wa), lambda bb: (bb, 0, 0)),
        out_shape=jax.ShapeDtypeStruct((b, t, wa), BF16),
        compiler_params=_params(("parallel",)),
        name="diff_attn_sample",
    )(*lams, sub_w, qkv, qkv, qkv, cache_k, cache_v)


def _seg_ones(n=LANES, seg=HEAD_B):
    r = lax.broadcasted_iota(jnp.int32, (n, n), 0) // seg
    c = lax.broadcasted_iota(jnp.int32, (n, n), 1) // seg
    return jnp.where(r == c, 1.0, 0.0).astype(BF16)


def _rwkv_prep_kernel(x_ref, gn_ref, wpb_ref, prev_ref, mu_ref, w0_ref, w2p_ref, a0_ref, a2p_ref, g2_ref, kk_ref,
                      ka_ref, rk_ref, r_o, k_o, v_o, ld_o, na_o, b_o, g_o, bon_o, shift_o, carry_ref, *, wb):
    i = pl.program_id(1)

    @pl.when(i == 0)
    def _():
        carry_ref[...] = prev_ref[...]

    h = _rmsnorm(x_ref[...], gn_ref[...], NORM_EPS).astype(BF16)
    pb = _dot(h, wpb_ref[...])
    shift_o[...] = pb[pb.shape[0] - 1:, :]
    tm = pb.shape[0]
    row = lax.broadcasted_iota(jnp.int32, (tm, 1), 0)
    prev = jnp.where(row == 0, carry_ref[...], pltpu.roll(pb, 1, axis=0))
    carry_ref[...] = pb[tm - 1:tm, :]
    xs = pb + (prev - pb) * mu_ref[...]

    n_slab = wb // LANES
    x_wa = xs[:, 3 * wb:3 * wb + LANES]
    x_g = xs[:, 3 * wb + LANES:3 * wb + 2 * LANES]
    lw = _mm(jnp.tanh(x_wa), w2p_ref[...], 3)
    la = _mm(x_wa, a2p_ref[...], 3)
    g = _mm(_sigmoid(x_g), g2_ref[...], 3)
    seg = _seg_ones()

    for s in range(n_slab):
        sl = slice(s * LANES, (s + 1) * LANES)
        r = xs[:, s * LANES:(s + 1) * LANES]
        k = xs[:, wb + s * LANES:wb + (s + 1) * LANES]
        v = xs[:, 2 * wb + s * LANES:2 * wb + (s + 1) * LANES]
        z = -(w0_ref[:, sl] + lw[:, sl])
        softplus = jnp.maximum(z, 0.0) + jnp.log1p(jnp.exp(-jnp.abs(z)))
        w_log = -softplus - 0.5
        a = _sigmoid(a0_ref[:, sl] + la[:, sl])
        kk = k * kk_ref[:, sl]
        nrm = jnp.sqrt(_mm_exact_rhs(kk * kk, seg))
        kk = kk / jnp.maximum(nrm, 1e-12)
        k2 = k * (1.0 + (a - 1.0) * ka_ref[:, sl])
        bonus = _mm_exact_rhs(r * k2 * rk_ref[:, sl], seg) * v
        r_o[s] = r.astype(r_o.dtype)
        k_o[s] = k2.astype(k_o.dtype)
        v_o[s] = v.astype(v_o.dtype)
        ld_o[s] = -jnp.exp(w_log)
        na_o[s] = (-kk).astype(na_o.dtype)
        b_o[s] = (kk * a).astype(b_o.dtype)
        g_o[s] = g[:, sl].astype(g_o.dtype)
        bon_o[s] = bonus.astype(bon_o.dtype)


def _rwkv_prep(x, gain, w_pb, prev_row, mu, w0, w2p, a0, a2p, g2, k_k, k_a, r_k, *, wb):
    b, t, d = x.shape
    cols = w_pb.shape[1]
    tm = min(256, t)
    n_slab = wb // LANES
    vec = lambda n: pl.BlockSpec((1, n), lambda bb, i: (0, 0))
    full = lambda a: pl.BlockSpec(a.shape, lambda bb, i: (0, 0))
    out_spec = pl.BlockSpec((None, n_slab, tm, LANES), lambda bb, i: (bb, 0, i, 0))
    sds = lambda dt: jax.ShapeDtypeStruct((b, n_slab, t, LANES), dt)
    out_dtypes = [BF16, BF16, BF16, F32, BF16, BF16, BF16, BF16]
    return pl.pallas_call(
        functools.partial(_rwkv_prep_kernel, wb=wb),
        grid=(b, t // tm),
        in_specs=[
            pl.BlockSpec((None, tm, d), lambda bb, i: (bb, i, 0)),
            vec(d), full(w_pb),
            pl.BlockSpec((None, 1, cols), lambda bb, i: (bb, 0, 0)),
            vec(cols), vec(wb), full(w2p), vec(wb), full(a2p), full(g2), vec(wb), vec(wb), vec(wb),
        ],
        out_specs=[out_spec] * 8 + [pl.BlockSpec((None, 1, cols), lambda bb, i: (bb, 0, 0))],
        out_shape=[sds(dt) for dt in out_dtypes] + [jax.ShapeDtypeStruct((b, 1, cols), F32)],
        scratch_shapes=[pltpu.VMEM((1, cols), F32)],
        compiler_params=_params(("parallel", "arbitrary")),
        name="rwkv_prep",
    )(x, gain, w_pb, prev_row, mu, w0, w2p, a0, a2p, g2, k_k, k_a, r_k)


def _stack2(x, first):
    xb = x.astype(BF16)
    zero = jnp.zeros_like(xb)
    return jnp.concatenate([jnp.where(first, xb, zero), jnp.where(first, zero, xb)], axis=0)


def _rwkv_chunk_kernel(r_ref, k_ref, v_ref, ld_ref, na_ref, b_ref, g_ref, bon_ref, s0_ref, lnw_ref, lnb_ref,
                       y_ref, sT_ref, st_ref, *, cl):
    c = pl.program_id(1)
    n_chunks = pl.num_programs(1)
    n_slab = r_ref.shape[0]
    n_sub = r_ref.shape[1] // cl

    @pl.when(c == 0)
    def _():
        st_ref[...] = s0_ref[...]

    lane = lax.broadcasted_iota(jnp.int32, (1, LANES), 1)
    first = lane < HEAD_B
    ti = lax.broadcasted_iota(jnp.int32, (cl, cl), 0)
    tj = lax.broadcasted_iota(jnp.int32, (cl, cl), 1)
    tri_incl = jnp.where(ti >= tj, 1.0, 0.0).astype(BF16)
    seg = _seg_ones()
    n2 = 2 * cl
    ri = lax.broadcasted_iota(jnp.int32, (n2, n2), 0) % cl
    ci = lax.broadcasted_iota(jnp.int32, (n2, n2), 1) % cl
    strict = ri > ci
    incl = ri >= ci
    zero = jnp.zeros((n2, n2), F32)
    inv_n = 1.0 / HEAD_B

    items = [(s, q) for q in range(n_sub) for s in range(n_slab)]
    rows = lambda q: slice(q * cl, (q + 1) * cl)

    pre = []
    for s, q in items:
        ld = ld_ref[s, rows(q), :]
        hi, mid, lo = _split3(ld)
        cum = _dot(tri_incl, hi) + (_dot(tri_incl, mid) + _dot(tri_incl, lo))
        pre.append((ld, cum))

    stacks = []
    for (s, q), (ld, cum) in zip(items, pre):
        cum_last = cum[cl - 1:cl, :]
        g_incl = jnp.exp(cum)
        g_excl = jnp.exp(cum - ld)
        g_inv = jnp.exp(-cum)
        g_end = jnp.exp(cum_last - cum)
        k = k_ref[s, rows(q), :]
        bv = b_ref[s, rows(q), :]
        stacks.append(dict(
            g_last=jnp.exp(cum_last),
            la=_stack2(na_ref[s, rows(q), :] * g_excl, first),
            lr=_stack2(r_ref[s, rows(q), :] * g_incl, first),
            l2=jnp.concatenate([_stack2(bv * g_inv, first), _stack2(k * g_inv, first)], axis=0),
            lh=jnp.concatenate([_stack2(bv * g_end, first), _stack2(k * g_end, first)], axis=0),
            v2=_stack2(v_ref[s, rows(q), :], first)))

    grams = [_dot(jnp.concatenate([d["la"], d["lr"]], axis=0), d["l2"], NT) for d in stacks]
    for d, gram in zip(stacks, grams):
        d["g_ak"] = jnp.where(strict, gram[:n2, n2:], zero).astype(BF16)
        d["g_r"] = jnp.concatenate([jnp.where(incl, gram[n2:, :n2], zero),
                                    jnp.where(incl, gram[n2:, n2:], zero)], axis=1).astype(BF16)
    rr = lax.broadcasted_iota(jnp.int32, (n2, n2), 0)
    cc = lax.broadcasted_iota(jnp.int32, (n2, n2), 1)
    eye = jnp.where(rr == cc, 1.0, 0.0).astype(F32)
    pows = [jnp.where(strict, gram[:n2, :n2], zero) for gram in grams]
    invs = [eye + p for p in pows]
    for _ in range(int(math.log2(cl)) - 1):
        pows = [_dot(p.astype(BF16), p.astype(BF16)) for p in pows]
        invs = [inv + _dot(inv.astype(BF16), p.astype(BF16)) for inv, p in zip(invs, pows)]
    akv = [_dot(d["g_ak"], d["v2"]) for d in stacks]

    sts = [st_ref[s] for s in range(n_slab)]
    for q in range(n_sub):
        base = q * n_slab
        stb = [st.astype(BF16) for st in sts]
        xs = [_dot(stacks[base + s]["la"], stb[s], NT) + akv[base + s] for s in range(n_slab)]
        us = [_dot(invs[base + s].astype(BF16), xs[s].astype(BF16)) for s in range(n_slab)]
        uvs = [jnp.concatenate([us[s].astype(BF16), stacks[base + s]["v2"]], axis=0) for s in range(n_slab)]
        y2s = [_dot(stacks[base + s]["lr"], stb[s], NT) + _dot(stacks[base + s]["g_r"], uvs[s])
               for s in range(n_slab)]
        sts = [sts[s] * stacks[base + s]["g_last"] + _dot(uvs[s], stacks[base + s]["lh"], TN)
               for s in range(n_slab)]
        ys = [y2[:cl, :] + y2[cl:, :] for y2 in y2s]
        means = [_mm_exact_rhs2(y, seg) * inv_n for y in ys]
        ds = [y - m for y, m in zip(ys, means)]
        vars_ = [_mm_exact_rhs2(d * d, seg) * inv_n for d in ds]
        for s in range(n_slab):
            sl = slice(s * LANES, (s + 1) * LANES)
            yn = ds[s] * lax.rsqrt(vars_[s] + GN_EPS) * lnw_ref[:, sl] + lnb_ref[:, sl]
            y_ref[s, rows(q), :] = ((yn + bon_ref[s, rows(q), :]) * g_ref[s, rows(q), :]).astype(y_ref.dtype)
    for s in range(n_slab):
        st_ref[s] = sts[s]

    @pl.when(c == n_chunks - 1)
    def _():
        sT_ref[...] = st_ref[...]


def _rwkv_chunks(prep, s0_bd, lnx_w, lnx_b):
    b, n_slab, t, _ = prep[0].shape
    cl = min(CHUNK_B, t)
    step = min(CHUNKS_PER_STEP * cl, t)
    wb = n_slab * LANES
    in_spec = pl.BlockSpec((None, n_slab, step, LANES), lambda bb, c: (bb, 0, c, 0))
    st_spec = pl.BlockSpec((None, n_slab, LANES, LANES), lambda bb, c: (bb, 0, 0, 0))
    vec = pl.BlockSpec((1, wb), lambda bb, c: (0, 0))
    return pl.pallas_call(
        functools.partial(_rwkv_chunk_kernel, cl=cl),
        grid=(b, t // step),
        in_specs=[in_spec] * 8 + [st_spec, vec, vec],
        out_specs=[in_spec, st_spec],
        out_shape=[
            jax.ShapeDtypeStruct((b, n_slab, t, LANES), BF16),
            jax.ShapeDtypeStruct((b, n_slab, LANES, LANES), F32),
        ],
        scratch_shapes=[pltpu.VMEM((n_slab, LANES, LANES), F32)],
        compiler_params=_params(("parallel", "arbitrary")),
        name="rwkv_chunks",
    )(*prep, s0_bd, lnx_w, lnx_b)


def _state_to_blockdiag(s):
    b, h, n, _ = s.shape
    s = s.reshape(b, h // 2, 2, n, n)
    z = jnp.zeros_like(s[:, :, 0])
    top = jnp.concatenate([s[:, :, 0], z], axis=-1)
    bot = jnp.concatenate([z, s[:, :, 1]], axis=-1)
    return jnp.concatenate([top, bot], axis=-2)


def _state_from_blockdiag(sbd):
    b, hs, n2, _ = sbd.shape
    n = n2 // 2
    return jnp.stack([sbd[:, :, :n, :n], sbd[:, :, n:, n:]], axis=2).reshape(b, 2 * hs, n, n)


def _route(logits, n_groups, n_per_group):
    ne = n_groups * n_per_group
    lane = lax.broadcasted_iota(jnp.int32, logits.shape, 1)
    big = jnp.int32(1 << 30)
    is_grp = (lane >= ne) & (lane < ne + n_groups)
    lg = jnp.where(is_grp, logits, NEG_INF)
    mg = jnp.max(lg, axis=-1, keepdims=True)
    gi = jnp.min(jnp.where(is_grp & (lg == mg), lane, big), axis=-1, keepdims=True) - ne
    eg = jnp.where(is_grp, jnp.exp(lg - mg), 0.0)
    p_sel = 1.0 / jnp.sum(eg, axis=-1, keepdims=True)
    in_grp = (lane < ne) & ((lane // n_per_group) == gi)
    le = jnp.where(in_grp, logits, NEG_INF)
    me = jnp.max(le, axis=-1, keepdims=True)
    ee = jnp.where(in_grp, jnp.exp(le - me), 0.0)
    pe = ee / jnp.sum(ee, axis=-1, keepdims=True)
    v1 = jnp.max(jnp.where(in_grp, pe, -1.0), axis=-1, keepdims=True)
    i1 = jnp.min(jnp.where(in_grp & (pe == v1), lane, big), axis=-1, keepdims=True)
    rest = in_grp & (lane != i1)
    v2 = jnp.max(jnp.where(rest, pe, -1.0), axis=-1, keepdims=True)
    i2 = jnp.min(jnp.where(rest & (pe == v2), lane, big), axis=-1, keepdims=True)
    tot = v1 + v2
    w1 = (v1 / tot) * p_sel
    w2 = (v2 / tot) * p_sel
    return (jnp.where(lane == 0, w1, 0.0) + jnp.where(lane == 1, w2, 0.0)
            + jnp.where(lane == 2, i1.astype(F32), 0.0) + jnp.where(lane == 3, i2.astype(F32), 0.0))


def _mix_kernel(x_ref, oa_ref, ob_ref, gt_ref, wa_ref, wb_ref, wo_ref, gn_ref, wr_ref, br_ref,
                x2_ref, hn_ref, route_ref, route_t_ref, *, n_groups, n_per_group):
    d = x_ref.shape[1]
    nb, n_slab, t_blk, _ = ob_ref.shape
    y_a = _dot(oa_ref[...], wa_ref[...])
    ob = jnp.concatenate([ob_ref[:, s].reshape(nb * t_blk, LANES) for s in range(n_slab)], axis=1)
    y_b = _dot(ob, wb_ref[...])
    merged = gt_ref[:, :d] * y_a + gt_ref[:, d:] * y_b
    x2 = x_ref[...] + _dot(merged.astype(BF16), wo_ref[...])
    x2_ref[...] = x2
    hn = _rmsnorm(x2, gn_ref[...], NORM_EPS)
    _store_row_tiles(hn_ref, hn)
    logits = _mm(hn, wr_ref[...], 3) + br_ref[...]
    route = _route(logits, n_groups, n_per_group)
    route_ref[...] = route
    route_t_ref[...] = route.T[:ROUTE_ROWS, :]


def _mix(x2d, o_a, o_b, gates, w_a, w_b, w_o, gain_ffn, w_route, b_route, *, n_groups, n_per_group):
    b, n_slab, t, _ = o_b.shape
    n, d = x2d.shape
    if t >= 512:
        nb, t_blk = 1, 512
    else:
        nb, t_blk = min(b, 512 // t), t
    tm = nb * t_blk
    nt = t // t_blk
    full = lambda a: pl.BlockSpec(a.shape, lambda i: (0,) * a.ndim)
    return pl.pallas_call(
        functools.partial(_mix_kernel, n_groups=n_groups, n_per_group=n_per_group),
        grid=(n // tm,),
        in_specs=[
            pl.BlockSpec((tm, d), lambda i: (i, 0)),
            pl.BlockSpec((tm, o_a.shape[1]), lambda i: (i, 0)),
            pl.BlockSpec((nb, n_slab, t_blk, LANES), lambda i: (i // nt, 0, i % nt, 0)),
            pl.BlockSpec((tm, 2 * d), lambda i: (i, 0)),
            full(w_a), full(w_b), full(w_o), full(gain_ffn), full(w_route), full(b_route),
        ],
        out_specs=[
            pl.BlockSpec((tm, d), lambda i: (i, 0)),
            pl.BlockSpec((tm * ROW_TILE, LANES), lambda i: (i, 0)),
            pl.BlockSpec((tm, LANES), lambda i: (i, 0)),
            pl.BlockSpec((ROUTE_ROWS, tm), lambda i: (0, i)),
        ],
        out_shape=[
            jax.ShapeDtypeStruct((n, d), F32),
            jax.ShapeDtypeStruct((n * ROW_TILE, LANES), F32),
            jax.ShapeDtypeStruct((n, LANES), F32),
            jax.ShapeDtypeStruct((ROUTE_ROWS, n), F32),
        ],
        compiler_params=_params(("parallel",)),
        name="mix_route",
    )(x2d, o_a, o_b, gates, w_a, w_b, w_o, gain_ffn, w_route, b_route)


def _rank_kernel(rt_ref, ri_ref, cnt_ref, carry_ref, *, ne_pad):
    i = pl.program_id(0)

    @pl.when(i == 0)
    def _():
        carry_ref[...] = jnp.zeros_like(carry_ref)

    rt = rt_ref[...]
    tt = rt.shape[1]
    e1 = rt[2:3, :].astype(jnp.int32)
    e2 = rt[3:4, :].astype(jnp.int32)
    eid = lax.broadcasted_iota(jnp.int32, (ne_pad, tt), 0)
    m1 = eid == e1
    m2 = eid == e2
    member = jnp.where(m1 | m2, 1.0, 0.0)
    before = jnp.where(lax.broadcasted_iota(jnp.int32, (tt, tt), 0) < lax.broadcasted_iota(jnp.int32, (tt, tt), 1),
                       1.0, 0.0).astype(BF16)
    carry = carry_ref[...]
    rank = _dot(member.astype(BF16), before) + carry[:, :1]
    r1 = jnp.sum(jnp.where(m1, rank, 0.0), axis=0, keepdims=True).astype(jnp.int32)
    r2 = jnp.sum(jnp.where(m2, rank, 0.0), axis=0, keepdims=True).astype(jnp.int32)
    ri_ref[...] = jnp.concatenate([e1, e2, r1, r2, jnp.zeros((ROUTE_ROWS - 4, tt), jnp.int32)], axis=0)
    carry = carry + jnp.sum(member, axis=1, keepdims=True)
    carry_ref[...] = carry

    @pl.when(i == pl.num_programs(0) - 1)
    def _():
        cnt_ref[...] = carry


def _rank(route_t, ne_pad):
    _, n = route_t.shape
    tt = min(512, n)
    return pl.pallas_call(
        functools.partial(_rank_kernel, ne_pad=ne_pad),
        grid=(n // tt,),
        in_specs=[pl.BlockSpec((ROUTE_ROWS, tt), lambda i: (0, i))],
        out_specs=[pl.BlockSpec((ROUTE_ROWS, tt), lambda i: (0, i)),
                   pl.BlockSpec((ne_pad, LANES), lambda i: (0, 0))],
        out_shape=[jax.ShapeDtypeStruct((ROUTE_ROWS, n), jnp.int32),
                   jax.ShapeDtypeStruct((ne_pad, LANES), F32)],
        scratch_shapes=[pltpu.VMEM((ne_pad, LANES), F32)],
        compiler_params=_params(("arbitrary",)),
        name="moe_rank",
    )(route_t)


def _store_row_tiles(ref2, x2d):
    rows, d = x2d.shape
    for j in range(d // LANES):
        ref2[pl.ds(j, rows, stride=d // LANES), :] = x2d[:, j * LANES:(j + 1) * LANES]


def _load_row_tiles(ref2):
    rows = ref2.shape[0] // ROW_TILE
    return jnp.concatenate([ref2[pl.ds(j, rows, stride=ROW_TILE), :] for j in range(ROW_TILE)], axis=1)


def _row_copy(src, src_row, dst, dst_row, sem):
    s0 = pl.multiple_of(src_row * ROW_TILE, ROW_TILE)
    d0 = pl.multiple_of(dst_row * ROW_TILE, ROW_TILE)
    return pltpu.make_async_copy(src.at[pl.ds(s0, ROW_TILE)], dst.at[pl.ds(d0, ROW_TILE)], sem)


def _pos_kernel(starts_ref, ri_ref, po_ref, *, ne):
    ri = ri_ref[...]
    e1, e2 = ri[0:1, :], ri[1:2, :]
    s1 = jnp.zeros_like(e1)
    s2 = jnp.zeros_like(e2)
    for e in range(ne):
        st = starts_ref[e]
        s1 = jnp.where(e1 == e, st, s1)
        s2 = jnp.where(e2 == e, st, s2)
    po_ref[...] = jnp.concatenate([s1 + ri[2:3, :], s2 + ri[3:4, :],
                                   jnp.zeros((ROUTE_ROWS - 2, ri.shape[1]), jnp.int32)], axis=0)


def _positions(starts, route_i, ne):
    _, n = route_i.shape
    tt = min(2048, n)
    return pl.pallas_call(
        functools.partial(_pos_kernel, ne=ne),
        grid_spec=pltpu.PrefetchScalarGridSpec(
            num_scalar_prefetch=1,
            grid=(n // tt,),
            in_specs=[pl.BlockSpec((ROUTE_ROWS, tt), lambda i, st: (0, i))],
            out_specs=pl.BlockSpec((ROUTE_ROWS, tt), lambda i, st: (0, i)),
        ),
        out_shape=jax.ShapeDtypeStruct((ROUTE_ROWS, n), jnp.int32),
        compiler_params=_params(("arbitrary",)),
        name="moe_positions",
    )(starts, route_i)


def _scatter_kernel(pos_ref, hn_ref, xs_in_ref, xs_ref, sem):
    del xs_in_ref
    tt = pos_ref.shape[1]

    def issue(t, carry):
        _row_copy(hn_ref, t, xs_ref, pos_ref[0, t], sem).start()
        _row_copy(hn_ref, t, xs_ref, pos_ref[1, t], sem).start()
        return carry

    lax.fori_loop(0, tt, issue, 0, unroll=8)

    def drain(t, carry):
        _row_copy(hn_ref, 0, xs_ref, 0, sem).wait()
        _row_copy(hn_ref, 0, xs_ref, 0, sem).wait()
        return carry

    lax.fori_loop(0, tt, drain, 0, unroll=8)


def _scatter(pos, hn_rt, n_rows):
    n = hn_rt.shape[0] // ROW_TILE
    tt = min(512, n)
    xs0 = jnp.zeros((n_rows * ROW_TILE, LANES), hn_rt.dtype)
    return pl.pallas_call(
        _scatter_kernel,
        grid=(n // tt,),
        in_specs=[
            pl.BlockSpec((ROUTE_ROWS, tt), lambda i: (0, i), memory_space=pltpu.SMEM),
            pl.BlockSpec((tt * ROW_TILE, LANES), lambda i: (i, 0)),
            pl.BlockSpec(memory_space=pl.ANY),
        ],
        out_specs=pl.BlockSpec(memory_space=pl.ANY),
        out_shape=jax.ShapeDtypeStruct((n_rows * ROW_TILE, LANES), hn_rt.dtype),
        scratch_shapes=[pltpu.SemaphoreType.DMA(())],
        input_output_aliases={2: 0},
        compiler_params=_params(("arbitrary",)),
        name="moe_scatter",
    )(pos, hn_rt, xs0)


def _expert_kernel(te_ref, nu_ref, xs_ref, w1_ref, w3_ref, w2_ref, o_ref):
    i = pl.program_id(0)

    @pl.when(i < nu_ref[0])
    def _():
        x = _load_row_tiles(xs_ref).astype(BF16)
        up = _dot(x, w1_ref[...].astype(BF16))
        lin = _dot(x, w3_ref[...].astype(BF16))
        z = (up * _sigmoid(up)) * lin
        _store_row_tiles(o_ref, _dot(z.astype(BF16), w2_ref[...].astype(BF16)))

    @pl.when(i >= nu_ref[0])
    def _():
        o_ref[...] = jnp.zeros_like(o_ref)


def _experts(tile_expert, n_used, xs_rt, w1, w3, w2):
    n_rows = xs_rt.shape[0] // ROW_TILE
    _, d, f = w1.shape
    tm = EXPERT_TILE
    last = lambda i, nu: jnp.minimum(i, jnp.maximum(nu[0] - 1, 0))
    return pl.pallas_call(
        _expert_kernel,
        grid_spec=pltpu.PrefetchScalarGridSpec(
            num_scalar_prefetch=2,
            grid=(n_rows // tm,),
            in_specs=[
                pl.BlockSpec((tm * ROW_TILE, LANES), lambda i, te, nu: (last(i, nu), 0)),
                pl.BlockSpec((None, d, f), lambda i, te, nu: (te[i], 0, 0)),
                pl.BlockSpec((None, d, f), lambda i, te, nu: (te[i], 0, 0)),
                pl.BlockSpec((None, f, d), lambda i, te, nu: (te[i], 0, 0)),
            ],
            out_specs=pl.BlockSpec((tm * ROW_TILE, LANES), lambda i, te, nu: (i, 0)),
        ),
        out_shape=jax.ShapeDtypeStruct((n_rows * ROW_TILE, LANES), F32),
        compiler_params=_params(("arbitrary",)),
        name="moe_experts",
    )(tile_expert, n_used, xs_rt, w1, w3, w2)


def _combine_kernel(pos_ref, rf_ref, x2_ref, gf_ref, o_ref, y_ref, buf1, buf2, sem):
    tt = pos_ref.shape[1]

    def issue(t, carry):
        _row_copy(o_ref, pos_ref[0, t], buf1, t, sem).start()
        _row_copy(o_ref, pos_ref[1, t], buf2, t, sem).start()
        return carry

    lax.fori_loop(0, tt, issue, 0, unroll=8)

    def drain(t, carry):
        _row_copy(o_ref, 0, buf1, 0, sem).wait()
        _row_copy(o_ref, 0, buf2, 0, sem).wait()
        return carry

    lax.fori_loop(0, tt, drain, 0, unroll=8)
    rf = rf_ref[...]
    y = x2_ref[...] + rf[:, 0:1] * _load_row_tiles(buf1) + rf[:, 1:2] * _load_row_tiles(buf2)
    y_ref[...] = _rmsnorm(y, gf_ref[...], NORM_EPS)


def _combine(pos, route_f, x2, gain_final, o_rt):
    n, d = x2.shape
    tt = min(256, n)
    return pl.pallas_call(
        _combine_kernel,
        grid=(n // tt,),
        in_specs=[
            pl.BlockSpec((ROUTE_ROWS, tt), lambda i: (0, i), memory_space=pltpu.SMEM),
            pl.BlockSpec((tt, LANES), lambda i: (i, 0)),
            pl.BlockSpec((tt, d), lambda i: (i, 0)),
            pl.BlockSpec((1, d), lambda i: (0, 0)),
            pl.BlockSpec(memory_space=pl.ANY),
        ],
        out_specs=pl.BlockSpec((tt, d), lambda i: (i, 0)),
        out_shape=jax.ShapeDtypeStruct((n, d), F32),
        scratch_shapes=[pltpu.VMEM((tt * ROW_TILE, LANES), F32), pltpu.VMEM((tt * ROW_TILE, LANES), F32),
                        pltpu.SemaphoreType.DMA(())],
        compiler_params=_params(("arbitrary",)),
        name="moe_combine",
    )(pos, route_f, x2, gain_final, o_rt)


def _moe(hn_rt, route_f, route_t, w1, w3, w2, x2, gain_final):
    n = hn_rt.shape[0] // ROW_TILE
    ne = w1.shape[0]
    ne_pad = -(-ne // 8) * 8
    tm = EXPERT_TILE
    n_tiles = -(-2 * n // tm) + ne
    route_i, counts = _rank(route_t, ne_pad)
    counts = counts[:ne, 0].astype(jnp.int32)
    tiles = (counts + tm - 1) // tm
    ends = jnp.cumsum(tiles)
    starts = (ends - tiles) * tm
    n_used = ends[-1:]
    tile_expert = jnp.minimum(jnp.sum(jnp.arange(n_tiles)[:, None] >= ends[None, :], axis=1), ne - 1)
    pos = _positions(starts.astype(jnp.int32), route_i, ne)
    xs_rt = _scatter(pos, hn_rt, n_tiles * tm)
    o_rt = _experts(tile_expert.astype(jnp.int32), n_used.astype(jnp.int32), xs_rt, w1, w3, w2)
    return _combine(pos, route_f, x2, gain_final, o_rt)


def _run_path(x, prev_row, s0, cache, wts, dims):
    b, t, d = x.shape
    x2d = x.reshape(b * t, d)
    q_scale = dims["da"] ** -0.5 * math.log2(math.e)
    gates = _gate_proj(x2d, wts["norm_mix"], wts["w_gates"])

    if cache is None:
        qkv, kf, vf, qt, vt = _qkv_proj(x2d, wts["norm_mix"], wts["w_qkv"], q_scale, True)
        qkv3 = qkv.reshape(b, t, -1)
        tiles = lambda a: a.reshape(b, t // a.shape[2], a.shape[1], a.shape[2])
        o_a = _attn_prompt(qkv3, tiles(qt), tiles(vt), wts["lams"], wts["subln_w"].reshape(-1, 1),
                           n_heads=dims["ha"], chunk=dims["chunk"], lam_init=dims["lam_init"])
    else:
        qkv, kf, vf = _qkv_proj(x2d, wts["norm_mix"], wts["w_qkv"], q_scale, False)
        qkv3 = qkv.reshape(b, t, -1)
        o_a = _attn_sample(qkv3, cache[0], cache[1], wts["lams"], wts["subln_w"], n_heads=dims["ha"],
                           lam_init=dims["lam_init"])

    *prep, shift_new = _rwkv_prep(x, wts["norm_mix"], wts["w_pb"], prev_row, wts["mu"], wts["w0"], wts["w2p"],
                                  wts["a0"], wts["a2p"], wts["g2"], wts["k_k"], wts["k_a"], wts["r_k"],
                                  wb=dims["wb"])
    o_b, s_new = _rwkv_chunks(prep, _state_to_blockdiag(s0), wts["lnx_w"], wts["lnx_b"])

    x2, hn, route_f, route_t = _mix(x2d, o_a.reshape(b * t, -1), o_b, gates, wts["w_a_out"], wts["w_b_out"],
                                    wts["w_o"], wts["norm_ffn"], wts["w_route"], wts["b_route"],
                                    n_groups=dims["n_groups"], n_per_group=dims["n_per_group"])
    y = _moe(hn, route_f, route_t, wts["moe_w1"], wts["moe_w3"], wts["moe_w2"], x2, wts["norm_final"])

    ha = dims["ha"]
    return (y.reshape(b, t, d), kf.reshape(1, b, t, ha, -1), vf.reshape(1, b, t, ha, -1),
            shift_new[None], _state_from_blockdiag(s_new)[None])


def kernel(x_prompt, x_sample, cache_attn_k, cache_attn_v, state_rwkv_shift, state_rwkv_wkv, norm_mix, w_in, lambda_q1, lambda_k1, lambda_q2, lambda_k2, subln_w, w_a_out, rwkv_mu, rwkv_w0, rwkv_w2, rwkv_a0, rwkv_a2, rwkv_g2, rwkv_k_k, rwkv_k_a, rwkv_r_k, rwkv_lnx_w, rwkv_lnx_b, w_b_out, w_o, norm_ffn, moe_w_group, moe_b_group, moe_w_router, moe_b_router, moe_w1, moe_w3, moe_w2, norm_final):
    assert w_in.shape[0] == 1, "single-layer trunk"
    l = 0
    d = x_prompt.shape[-1]
    ha, dva = cache_attn_k.shape[3], cache_attn_v.shape[4]
    wa = ha * dva
    hb, db = state_rwkv_wkv.shape[2], state_rwkv_wkv.shape[3]
    wb = hb * db
    lora_w, lora_a, lora_g = rwkv_w2.shape[1], rwkv_a2.shape[1], rwkv_g2.shape[1]
    rwkv_cols = 3 * wb + lora_w + lora_a + lora_g
    n_groups = moe_w_group.shape[-1]
    n_per_group = moe_w_router.shape[-1] // n_groups
    ne = n_groups * n_per_group
    assert db == HEAD_B and lora_w + lora_a == LANES and lora_g == LANES and dva == LANES
    assert ne + n_groups <= LANES

    row = lambda v: v.reshape(1, -1).astype(F32)
    w_l = w_in[l]
    zeros_w = jnp.zeros((lora_a, wb), F32)
    zeros_a = jnp.zeros((lora_w, wb), F32)
    w_route = jnp.zeros((d, LANES), F32)
    w_route = w_route.at[:, :ne].set(moe_w_router[l]).at[:, ne:ne + n_groups].set(moe_w_group[l])
    b_route = jnp.zeros((1, LANES), F32)
    b_route = b_route.at[0, :ne].set(moe_b_router[l]).at[0, ne:ne + n_groups].set(moe_b_group[l])
    f = moe_w1.shape[-1]
    wts = dict(
        norm_mix=row(norm_mix[l]),
        w_qkv=w_l[:, :3 * wa].astype(BF16),
        w_pb=w_l[:, 3 * wa:3 * wa + rwkv_cols].astype(BF16),
        w_gates=w_l[:, 3 * wa + rwkv_cols:].astype(BF16),
        lams=[row(lambda_q1[l]), row(lambda_k1[l]), row(lambda_q2[l]), row(lambda_k2[l])],
        subln_w=row(subln_w[l]),
        mu=row(rwkv_mu[l]), w0=row(rwkv_w0[l]), a0=row(rwkv_a0[l]),
        w2p=jnp.concatenate([rwkv_w2[l], zeros_w], axis=0),
        a2p=jnp.concatenate([zeros_a, rwkv_a2[l]], axis=0),
        g2=rwkv_g2[l].astype(F32),
        k_k=row(rwkv_k_k[l]), k_a=row(rwkv_k_a[l]), r_k=row(rwkv_r_k[l]),
        lnx_w=row(rwkv_lnx_w[l]), lnx_b=row(rwkv_lnx_b[l]),
        w_a_out=w_a_out[l].astype(BF16), w_b_out=w_b_out[l].astype(BF16), w_o=w_o[l].astype(BF16),
        norm_ffn=row(norm_ffn[l]), w_route=w_route, b_route=b_route,
        moe_w1=moe_w1[l].reshape(ne, d, f), moe_w3=moe_w3[l].reshape(ne, d, f), moe_w2=moe_w2[l].reshape(ne, f, d),
        norm_final=row(norm_final),
    )
    dims = dict(ha=ha, da=dva // 2, wb=wb, chunk=CHUNK_B, lam_init=0.8 - 0.6 * math.exp(-0.3 * l),
                n_groups=n_groups, n_per_group=n_per_group)

    bp = x_prompt.shape[0]
    yp, kp, vp, shp, wkp = _run_path(
        x_prompt.astype(F32), jnp.zeros((bp, 1, rwkv_cols), F32), jnp.zeros((bp, hb, db, db), F32), None, wts, dims)

    bs, past = cache_attn_k.shape[1], cache_attn_k.shape[2]
    cache = (cache_attn_k[l].reshape(bs, past * ha, dva), cache_attn_v[l].reshape(bs, past * ha, dva))
    ys, ks_, vs_, shs, wks = _run_path(
        x_sample.astype(F32), state_rwkv_shift[l].astype(F32), state_rwkv_wkv[l].astype(F32), cache, wts, dims)

    return (yp, ys, kp, vp, shp, wkp, ks_, vs_, shs, wks)
```

```python
import functools
import math

import jax
import jax.numpy as jnp
from jax import lax
from jax.experimental import pallas as pl
from jax.experimental.pallas import tpu as pltpu

F32 = jnp.float32
BF16 = jnp.bfloat16

LANES = 128
HEAD_B = 64
CHUNK_B = 64
CHUNKS_PER_STEP = 2
ATTN_TILE = 512
ROW_TILE = 8
ROUTE_ROWS = 8
EXPERT_TILE = 512
NORM_EPS = 1e-6
DIFF_EPS = 1e-5
GN_EPS = 64e-5
NEG_INF = -1e30
VMEM_LIMIT = 56 * 1024 * 1024

NN = (((1,), (0,)), ((), ()))
NT = (((1,), (1,)), ((), ()))
TN = (((0,), (0,)), ((), ()))


def _dot(a, b, dims=NN):
    return lax.dot_general(a, b, dims, preferred_element_type=F32)


def _split2(a):
    hi = a.astype(BF16)
    lo = (a - hi.astype(F32)).astype(BF16)
    return hi, lo


def _split3(a):
    hi = a.astype(BF16)
    r = a - hi.astype(F32)
    mid = r.astype(BF16)
    lo = (r - mid.astype(F32)).astype(BF16)
    return hi, mid, lo


def _mm(a, b, passes=1, dims=NN):
    if passes == 1:
        return _dot(a.astype(BF16), b.astype(BF16), dims)
    a_hi, a_lo = _split2(a)
    if passes == 2:
        b_hi = b.astype(BF16)
        return _dot(a_hi, b_hi, dims) + _dot(a_lo, b_hi, dims)
    b_hi, b_lo = _split2(b)
    return _dot(a_hi, b_hi, dims) + (_dot(a_hi, b_lo, dims) + _dot(a_lo, b_hi, dims))


def _mm_exact_rhs2(a, b_bf16, dims=NN):
    hi, lo = _split2(a)
    return _dot(hi, b_bf16, dims) + _dot(lo, b_bf16, dims)


def _rmsnorm(x, g, eps):
    return x * lax.rsqrt(jnp.mean(x * x, axis=-1, keepdims=True) + eps) * g


def _sigmoid(x):
    return 1.0 / (1.0 + jnp.exp(-x))


def _params(sem):
    return pltpu.CompilerParams(dimension_semantics=sem, vmem_limit_bytes=VMEM_LIMIT)


def _qkv_kernel(x_ref, g_ref, w_ref, qkv_ref, kf_ref, vf_ref, *rest, q_scale, transposed):
    if transposed:
        qt_ref, vt_ref, h_ref = rest
    else:
        (h_ref,) = rest
    j = pl.program_id(1)

    @pl.when(j == 0)
    def _():
        h_ref[...] = _rmsnorm(x_ref[...], g_ref[...], NORM_EPS).astype(BF16)

    p = _dot(h_ref[...], w_ref[...])

    @pl.when(j == 0)
    def _():
        q = p * q_scale
        qkv_ref[...] = q.astype(BF16)
        if transposed:
            qt_ref[...] = q.T.astype(BF16)

    @pl.when(j == 1)
    def _():
        qkv_ref[...] = p.astype(BF16)
        kf_ref[...] = p

    @pl.when(j == 2)
    def _():
        qkv_ref[...] = p.astype(BF16)
        vf_ref[...] = p
        if transposed:
            vt_ref[...] = p.T.astype(BF16)


def _qkv_proj(x2d, gain, w_qkv, q_scale, transposed):
    n, d = x2d.shape
    wa = w_qkv.shape[1] // 3
    tm = min(ATTN_TILE, n)
    out_specs = [
        pl.BlockSpec((tm, wa), lambda i, j: (i, j)),
        pl.BlockSpec((tm, wa), lambda i, j: (i, 0)),
        pl.BlockSpec((tm, wa), lambda i, j: (i, 0)),
    ]
    out_shape = [
        jax.ShapeDtypeStruct((n, 3 * wa), BF16),
        jax.ShapeDtypeStruct((n, wa), F32),
        jax.ShapeDtypeStruct((n, wa), F32),
    ]
    if transposed:
        out_specs += [pl.BlockSpec((None, wa, tm), lambda i, j: (i, 0, 0))] * 2
        out_shape += [jax.ShapeDtypeStruct((n // tm, wa, tm), BF16)] * 2
    return pl.pallas_call(
        functools.partial(_qkv_kernel, q_scale=q_scale, transposed=transposed),
        grid=(n // tm, 3),
        in_specs=[
            pl.BlockSpec((tm, d), lambda i, j: (i, 0)),
            pl.BlockSpec((1, d), lambda i, j: (0, 0)),
            pl.BlockSpec((d, wa), lambda i, j: (0, j)),
        ],
        out_specs=out_specs,
        out_shape=out_shape,
        scratch_shapes=[pltpu.VMEM((tm, d), BF16)],
        compiler_params=_params(("parallel", "arbitrary")),
        name="qkv_proj",
    )(x2d, gain, w_qkv)


def _gate_kernel(x_ref, g_ref, w_ref, o_ref):
    h = _rmsnorm(x_ref[...], g_ref[...], NORM_EPS).astype(BF16)
    o_ref[...] = _sigmoid(_dot(h, w_ref[...])).astype(o_ref.dtype)


def _gate_proj(x2d, gain, w):
    n, d = x2d.shape
    c = w.shape[1]
    tm = min(512, n)
    return pl.pallas_call(
        _gate_kernel,
        grid=(n // tm,),
        in_specs=[
            pl.BlockSpec((tm, d), lambda i: (i, 0)),
            pl.BlockSpec((1, d), lambda i: (0, 0)),
            pl.BlockSpec((d, c), lambda i: (0, 0)),
        ],
        out_specs=pl.BlockSpec((tm, c), lambda i: (i, 0)),
        out_shape=jax.ShapeDtypeStruct((n, c), BF16),
        compiler_params=_params(("parallel",)),
        name="gate_proj",
    )(x2d, gain, w)


def _lambda(lq1_ref, lk1_ref, lq2_ref, lk2_ref, lam_init):
    s1 = jnp.sum(lq1_ref[...] * lk1_ref[...], axis=-1, keepdims=True)
    s2 = jnp.sum(lq2_ref[...] * lk2_ref[...], axis=-1, keepdims=True)
    return jnp.exp(s1) - jnp.exp(s2) + lam_init


def _split_q(q):
    lane = lax.broadcasted_iota(jnp.int32, (1, q.shape[1]), 1)
    first = lane < (q.shape[1] // 2)
    zero = jnp.zeros_like(q)
    return jnp.where(first, q, zero), jnp.where(first, zero, q)


def _attn_prompt_kernel(lq1_ref, lk1_ref, lq2_ref, lk2_ref, subc_ref, qt_ref, k_ref, vt_ref, o_ref,
                        s0_ref, s1_ref, s2_ref, s3_ref, mt_ref, m_ref, l_ref, acc_ref, *, chunk, lam_init):
    i = pl.program_id(2)
    dv, tq = qt_ref.shape
    tk = vt_ref.shape[2]
    lam = _lambda(lq1_ref, lk1_ref, lq2_ref, lk2_ref, lam_init)
    qt = qt_ref[...]
    feat = lax.broadcasted_iota(jnp.int32, (dv, 1), 0)
    zq = jnp.zeros_like(qt)
    q_maps = (jnp.where(feat < dv // 2, qt, zq), jnp.where(feat < dv // 2, zq, qt))
    acc_ref[...] = jnp.zeros_like(acc_ref)
    rb = 64
    kb = min(256, tk)

    m_ref[...] = jnp.full(m_ref.shape, NEG_INF, F32)
    l_ref[...] = jnp.zeros_like(l_ref)
    buf_a, buf_b = (s0_ref, s1_ref), (s2_ref, s3_ref)

    def scores(j, buf, masked):
        ks = k_ref[pl.ds(pl.multiple_of(j * tk, tk), tk), :]
        for mi in range(2):
            s = _dot(ks, q_maps[mi])
            if masked:
                kc = lax.broadcasted_iota(jnp.int32, (tk, tq), 0) // chunk
                qc = lax.broadcasted_iota(jnp.int32, (tk, tq), 1) // chunk
                s = jnp.where(kc <= qc, s, NEG_INF)
            buf[mi][...] = s
            mt_ref[2 * (buf is buf_b) + mi] = jnp.max(s, axis=0, keepdims=True)

    def softmax_pv(j, buf):
        vt = vt_ref[j]
        m_old = [m_ref[mi] for mi in range(2)]
        m_new = [jnp.maximum(m_old[mi], mt_ref[2 * (buf is buf_b) + mi]) for mi in range(2)]
        alpha = [jnp.exp2(m_old[mi] - m_new[mi]) for mi in range(2)]
        lsum = [jnp.zeros((8, tq), F32) for _ in range(2)]
        pv = [None, None]
        for hf in range(tk // kb):
            for mi in range(2):
                blocks = []
                for r in range(kb // rb):
                    lo = hf * kb + r * rb
                    p = jnp.exp2(buf[mi][lo:lo + rb, :] - m_new[mi])
                    lsum[mi] = lsum[mi] + jnp.sum(p.reshape(rb // 8, 8, tq), axis=0)
                    blocks.append(p.astype(BF16))
                part = _dot(vt[:, hf * kb:(hf + 1) * kb], jnp.concatenate(blocks, axis=0))
                pv[mi] = part if pv[mi] is None else pv[mi] + part
        for mi in range(2):
            acc_ref[mi] = acc_ref[mi] * alpha[mi] + pv[mi]
            m_ref[mi] = m_new[mi]
            l_ref[mi] = alpha[mi] * l_ref[mi] + lsum[mi]

    def stage_even(j, next_masked):
        scores(j + 1, buf_b, next_masked)
        softmax_pv(j, buf_a)

    def stage_odd(j, next_masked):
        scores(j + 1, buf_a, next_masked)
        softmax_pv(j, buf_b)

    @pl.when(i == 0)
    def _():
        scores(0, buf_a, True)
        softmax_pv(0, buf_a)

    @pl.when(i > 0)
    def _():
        scores(0, buf_a, False)

    def pair(jj, carry):
        stage_even(2 * jj, False)
        stage_odd(2 * jj + 1, False)
        return carry

    lax.fori_loop(0, jnp.maximum(i - 1, 0) // 2, pair, 0)

    @pl.when(i % 2 == 1)
    def _():
        stage_even(i - 1, True)
        softmax_pv(i, buf_b)

    @pl.when((i % 2 == 0) & (i > 0))
    def _():
        stage_even(i - 2, False)
        stage_odd(i - 1, True)
        softmax_pv(i, buf_a)

    l1 = jnp.sum(l_ref[0], axis=0, keepdims=True)
    l2 = jnp.sum(l_ref[1], axis=0, keepdims=True)
    ot = acc_ref[0] / l1 - lam * (acc_ref[1] / l2)
    ms = jnp.mean(ot * ot, axis=0, keepdims=True)
    ot = ot * lax.rsqrt(ms + DIFF_EPS) * subc_ref[...] * (1.0 - lam_init)
    o_ref[...] = ot.T.astype(o_ref.dtype)


def _attn_prompt(qkv, qt, vt, lams, sub_col, *, n_heads, chunk, lam_init):
    b, t, c3 = qkv.shape
    wa = c3 // 3
    dv = wa // n_heads
    nk, tile = qt.shape[1], qt.shape[3]
    lam_specs = [pl.BlockSpec((1, lams[0].shape[1]), lambda bb, h, i: (0, 0)) for _ in range(4)]
    return pl.pallas_call(
        functools.partial(_attn_prompt_kernel, chunk=chunk, lam_init=lam_init),
        grid=(b, n_heads, nk),
        in_specs=lam_specs + [
            pl.BlockSpec((dv, 1), lambda bb, h, i: (0, 0)),
            pl.BlockSpec((None, None, dv, tile), lambda bb, h, i: (bb, i, h, 0)),
            pl.BlockSpec((None, t, dv), lambda bb, h, i: (bb, 0, n_heads + h)),
            pl.BlockSpec((None, nk, dv, tile), lambda bb, h, i: (bb, 0, h, 0)),
        ],
        out_specs=pl.BlockSpec((None, tile, dv), lambda bb, h, i: (bb, i, h)),
        out_shape=jax.ShapeDtypeStruct((b, t, wa), BF16),
        scratch_shapes=[pltpu.VMEM((tile, tile), F32)] * 4
        + [pltpu.VMEM((4, 1, tile), F32),
           pltpu.VMEM((2, 1, tile), F32),
           pltpu.VMEM((2, 8, tile), F32),
           pltpu.VMEM((2, dv, tile), F32)],
        compiler_params=_params(("parallel", "parallel", "arbitrary")),
        name="diff_attn_prompt",
    )(*lams, sub_col, qt, qkv, vt)


def _attn_sample_kernel(lq1_ref, lk1_ref, lq2_ref, lk2_ref, sub_ref, q_ref, kn_ref, vn_ref, ck_ref, cv_ref,
                        o_ref, *, n_heads, lam_init):
    lam = _lambda(lq1_ref, lk1_ref, lq2_ref, lk2_ref, lam_init)
    tq = q_ref.shape[0]
    dv = q_ref.shape[1] // n_heads
    past = ck_ref.shape[0] // n_heads
    ones_c = jnp.ones((past, dv), BF16)
    ones_n = jnp.ones((tq, dv), BF16)
    for h in range(n_heads):
        cols = slice(h * dv, (h + 1) * dv)
        q1, q2 = _split_q(q_ref[:, cols].astype(F32))
        w = jnp.concatenate([q1, q2], axis=0).T.astype(BF16)
        ck = ck_ref[pl.ds(h, past, stride=n_heads), :].astype(BF16)
        cv = cv_ref[pl.ds(h, past, stride=n_heads), :].astype(BF16)
        s_c = _dot(ck, w)
        s_n = _dot(kn_ref[:, cols], w)
        m = jnp.maximum(jnp.max(s_c, axis=0, keepdims=True), jnp.max(s_n, axis=0, keepdims=True))
        p_c = jnp.exp2(s_c - m).astype(BF16)
        p_n = jnp.exp2(s_n - m).astype(BF16)
        acc = (_dot(p_c, jnp.concatenate([cv, ones_c], axis=1), TN)
               + _dot(p_n, jnp.concatenate([vn_ref[:, cols], ones_n], axis=1), TN))
        o1 = acc[:tq, :dv] / acc[:tq, dv:]
        o2 = acc[tq:, :dv] / acc[tq:, dv:]
        o = o1 - lam * o2
        o_ref[:, cols] = (_rmsnorm(o, sub_ref[...], DIFF_EPS) * (1.0 - lam_init)).astype(o_ref.dtype)


def _attn_sample(qkv, cache_k, cache_v, lams, sub_w, *, n_heads, lam_init):
    b, t, c3 = qkv.shape
    wa = c3 // 3
    dv = wa // n_heads
    rows = cache_k.shape[1]
    lam_specs = [pl.BlockSpec((1, lams[0].shape[1]), lambda bb: (0, 0)) for _ in range(4)]
    return pl.pallas_call(
        functools.partial(_attn_sample_kernel, n_heads=n_heads, lam_init=lam_init),
        grid=(b,),
        in_specs=lam_specs + [
            pl.BlockSpec((1, dv), lambda bb: (0, 0)),
            pl.BlockSpec((None, t, wa), lambda bb: (bb, 0, 0)),
            pl.BlockSpec((None, t, wa), lambda bb: (bb, 0, 1)),
            pl.BlockSpec((None, t, wa), lambda bb: (bb, 0, 2)),
            pl.BlockSpec((None, rows, dv), lambda bb: (bb, 0, 0)),
            pl.BlockSpec((None, rows, dv), lambda bb: (bb, 0, 0)),
        ],
        out_specs=pl.BlockSpec((None, t, wa), lambda bb: (bb, 0, 0)),
        out_shape=jax.ShapeDtypeStruct((b, t, wa), BF16),
        compiler_params=_params(("parallel",)),
        name="diff_attn_sample",
    )(*lams, sub_w, qkv, qkv, qkv, cache_k, cache_v)


def _seg_ones(n=LANES, seg=HEAD_B):
    r = lax.broadcasted_iota(jnp.int32, (n, n), 0) // seg
    c = lax.broadcasted_iota(jnp.int32, (n, n), 1) // seg
    return jnp.where(r == c, 1.0, 0.0).astype(BF16)


def _rwkv_prep_kernel(x_ref, gn_ref, wpb_ref, prev_ref, mu_ref, w0_ref, w2p_ref, a0_ref, a2p_ref, g2_ref, kk_ref,
                      ka_ref, rk_ref, r_o, k_o, v_o, ld_o, na_o, b_o, g_o, bon_o, shift_o, carry_ref, *, wb):
    i = pl.program_id(1)

    @pl.when(i == 0)
    def _():
        carry_ref[...] = prev_ref[...]

    h = _rmsnorm(x_ref[...], gn_ref[...], NORM_EPS).astype(BF16)
    pb = _dot(h, wpb_ref[...])
    shift_o[...] = pb[pb.shape[0] - 1:, :]
    tm = pb.shape[0]
    row = lax.broadcasted_iota(jnp.int32, (tm, 1), 0)
    prev = jnp.where(row == 0, carry_ref[...], pltpu.roll(pb, 1, axis=0))
    carry_ref[...] = pb[tm - 1:tm, :]
    xs = pb + (prev - pb) * mu_ref[...]

    n_slab = wb // LANES
    x_wa = xs[:, 3 * wb:3 * wb + LANES]
    x_g = xs[:, 3 * wb + LANES:3 * wb + 2 * LANES]
    lw = _mm(jnp.tanh(x_wa), w2p_ref[...], 2)
    la = _mm(x_wa, a2p_ref[...])
    g = _mm(_sigmoid(x_g), g2_ref[...])
    seg = _seg_ones(2 * LANES)

    for s in range(n_slab):
        sl = slice(s * LANES, (s + 1) * LANES)
        r = xs[:, s * LANES:(s + 1) * LANES]
        k = xs[:, wb + s * LANES:wb + (s + 1) * LANES]
        v = xs[:, 2 * wb + s * LANES:2 * wb + (s + 1) * LANES]
        z = -(w0_ref[:, sl] + lw[:, sl])
        softplus = jnp.maximum(z, 0.0) + jnp.log1p(jnp.exp(-jnp.abs(z)))
        w_log = -softplus - 0.5
        a = _sigmoid(a0_ref[:, sl] + la[:, sl])
        kk = k * kk_ref[:, sl]
        k2 = k * (1.0 + (a - 1.0) * ka_ref[:, sl])
        sums = _mm_exact_rhs2(jnp.concatenate([kk * kk, r * k2 * rk_ref[:, sl]], axis=1), seg)
        kk = kk / jnp.maximum(jnp.sqrt(sums[:, :LANES]), 1e-12)
        bonus = sums[:, LANES:] * v
        r_o[s] = r.astype(r_o.dtype)
        k_o[s] = k2.astype(k_o.dtype)
        v_o[s] = v.astype(v_o.dtype)
        ld_o[s] = -jnp.exp(w_log)
        na_o[s] = (-kk).astype(na_o.dtype)
        b_o[s] = (kk * a).astype(b_o.dtype)
        g_o[s] = g[:, sl].astype(g_o.dtype)
        bon_o[s] = bonus.astype(bon_o.dtype)


def _rwkv_prep(x, gain, w_pb, prev_row, mu, w0, w2p, a0, a2p, g2, k_k, k_a, r_k, *, wb):
    b, t, d = x.shape
    cols = w_pb.shape[1]
    tm = min(256, t)
    n_slab = wb // LANES
    vec = lambda n: pl.BlockSpec((1, n), lambda bb, i: (0, 0))
    full = lambda a: pl.BlockSpec(a.shape, lambda bb, i: (0, 0))
    out_spec = pl.BlockSpec((None, n_slab, tm, LANES), lambda bb, i: (bb, 0, i, 0))
    sds = lambda dt: jax.ShapeDtypeStruct((b, n_slab, t, LANES), dt)
    out_dtypes = [BF16, BF16, BF16, F32, BF16, BF16, BF16, BF16]
    return pl.pallas_call(
        functools.partial(_rwkv_prep_kernel, wb=wb),
        grid=(b, t // tm),
        in_specs=[
            pl.BlockSpec((None, tm, d), lambda bb, i: (bb, i, 0)),
            vec(d), full(w_pb),
            pl.BlockSpec((None, 1, cols), lambda bb, i: (bb, 0, 0)),
            vec(cols), vec(wb), full(w2p), vec(wb), full(a2p), full(g2), vec(wb), vec(wb), vec(wb),
        ],
        out_specs=[out_spec] * 8 + [pl.BlockSpec((None, 1, cols), lambda bb, i: (bb, 0, 0))],
        out_shape=[sds(dt) for dt in out_dtypes] + [jax.ShapeDtypeStruct((b, 1, cols), F32)],
        scratch_shapes=[pltpu.VMEM((1, cols), F32)],
        compiler_params=_params(("parallel", "arbitrary")),
        name="rwkv_prep",
    )(x, gain, w_pb, prev_row, mu, w0, w2p, a0, a2p, g2, k_k, k_a, r_k)


def _stack2(x, first):
    xb = x.astype(BF16)
    zero = jnp.zeros_like(xb)
    return jnp.concatenate([jnp.where(first, xb, zero), jnp.where(first, zero, xb)], axis=0)


def _rwkv_chunk_kernel(r_ref, k_ref, v_ref, ld_ref, na_ref, b_ref, g_ref, bon_ref, s0_ref, lnw_ref, lnb_ref,
                       y_ref, sT_ref, st_ref, *, cl):
    c = pl.program_id(1)
    n_chunks = pl.num_programs(1)
    n_slab = r_ref.shape[0]
    n_sub = r_ref.shape[1] // cl

    @pl.when(c == 0)
    def _():
        st_ref[...] = s0_ref[...]

    lane = lax.broadcasted_iota(jnp.int32, (1, LANES), 1)
    first = lane < HEAD_B
    ti = lax.broadcasted_iota(jnp.int32, (cl, cl), 0)
    tj = lax.broadcasted_iota(jnp.int32, (cl, cl), 1)
    tri_incl = jnp.where(ti >= tj, 1.0, 0.0).astype(BF16)
    seg = _seg_ones()
    n2 = 2 * cl
    ri = lax.broadcasted_iota(jnp.int32, (n2, n2), 0) % cl
    ci = lax.broadcasted_iota(jnp.int32, (n2, n2), 1) % cl
    strict = ri > ci
    incl = ri >= ci
    zero = jnp.zeros((n2, n2), F32)
    inv_n = 1.0 / HEAD_B

    items = [(s, q) for q in range(n_sub) for s in range(n_slab)]
    rows = lambda q: slice(q * cl, (q + 1) * cl)

    pre = []
    for s, q in items:
        ld = ld_ref[s, rows(q), :]
        hi, mid, lo = _split3(ld)
        cum = _dot(tri_incl, hi) + (_dot(tri_incl, mid) + _dot(tri_incl, lo))
        pre.append((ld, cum))

    stacks = []
    for (s, q), (ld, cum) in zip(items, pre):
        cum_last = cum[cl - 1:cl, :]
        g_incl = jnp.exp(cum)
        g_excl = jnp.exp(cum - ld)
        g_inv = jnp.exp(-cum)
        g_end = jnp.exp(cum_last - cum)
        k = k_ref[s, rows(q), :]
        bv = b_ref[s, rows(q), :]
        stacks.append(dict(
            g_last=jnp.exp(cum_last),
            la=_stack2(na_ref[s, rows(q), :] * g_excl, first),
            lr=_stack2(r_ref[s, rows(q), :] * g_incl, first),
            l2=jnp.concatenate([_stack2(bv * g_inv, first), _stack2(k * g_inv, first)], axis=0),
            lh=jnp.concatenate([_stack2(bv * g_end, first), _stack2(k * g_end, first)], axis=0),
            v2=_stack2(v_ref[s, rows(q), :], first)))

    grams = [_dot(jnp.concatenate([d["la"], d["lr"]], axis=0), d["l2"], NT) for d in stacks]
    for d, gram in zip(stacks, grams):
        d["g_ak"] = jnp.where(strict, gram[:n2, n2:], zero).astype(BF16)
        d["g_r"] = jnp.concatenate([jnp.where(incl, gram[n2:, :n2], zero),
                                    jnp.where(incl, gram[n2:, n2:], zero)], axis=1).astype(BF16)
    rr = lax.broadcasted_iota(jnp.int32, (n2, n2), 0)
    cc = lax.broadcasted_iota(jnp.int32, (n2, n2), 1)
    eye = jnp.where(rr == cc, 1.0, 0.0).astype(F32)
    pows = [jnp.where(strict, gram[:n2, :n2], zero) for gram in grams]
    invs = [eye + p for p in pows]
    for _ in range(int(math.log2(cl)) - 1):
        pows = [_dot(p.astype(BF16), p.astype(BF16)) for p in pows]
        invs = [inv + _dot(inv.astype(BF16), p.astype(BF16)) for inv, p in zip(invs, pows)]
    akv = [_dot(d["g_ak"], d["v2"]) for d in stacks]

    sts = [st_ref[s] for s in range(n_slab)]
    for q in range(n_sub):
        base = q * n_slab
        stb = [st.astype(BF16) for st in sts]
        xs = [_dot(stacks[base + s]["la"], stb[s], NT) + akv[base + s] for s in range(n_slab)]
        us = [_dot(invs[base + s].astype(BF16), xs[s].astype(BF16)) for s in range(n_slab)]
        uvs = [jnp.concatenate([us[s].astype(BF16), stacks[base + s]["v2"]], axis=0) for s in range(n_slab)]
        y2s = [_dot(stacks[base + s]["lr"], stb[s], NT) + _dot(stacks[base + s]["g_r"], uvs[s])
               for s in range(n_slab)]
        sts = [sts[s] * stacks[base + s]["g_last"] + _dot(uvs[s], stacks[base + s]["lh"], TN)
               for s in range(n_slab)]
        ys = [y2[:cl, :] + y2[cl:, :] for y2 in y2s]
        means = [_mm_exact_rhs2(y, seg) * inv_n for y in ys]
        ds = [y - m for y, m in zip(ys, means)]
        vars_ = [_mm_exact_rhs2(d * d, seg) * inv_n for d in ds]
        for s in range(n_slab):
            sl = slice(s * LANES, (s + 1) * LANES)
            yn = ds[s] * lax.rsqrt(vars_[s] + GN_EPS) * lnw_ref[:, sl] + lnb_ref[:, sl]
            y_ref[s, rows(q), :] = ((yn + bon_ref[s, rows(q), :]) * g_ref[s, rows(q), :]).astype(y_ref.dtype)
    for s in range(n_slab):
        st_ref[s] = sts[s]

    @pl.when(c == n_chunks - 1)
    def _():
        sT_ref[...] = st_ref[...]


def _rwkv_chunks(prep, s0_bd, lnx_w, lnx_b):
    b, n_slab, t, _ = prep[0].shape
    cl = min(CHUNK_B, t)
    step = min(CHUNKS_PER_STEP * cl, t)
    wb = n_slab * LANES
    in_spec = pl.BlockSpec((None, n_slab, step, LANES), lambda bb, c: (bb, 0, c, 0))
    st_spec = pl.BlockSpec((None, n_slab, LANES, LANES), lambda bb, c: (bb, 0, 0, 0))
    vec = pl.BlockSpec((1, wb), lambda bb, c: (0, 0))
    return pl.pallas_call(
        functools.partial(_rwkv_chunk_kernel, cl=cl),
        grid=(b, t // step),
        in_specs=[in_spec] * 8 + [st_spec, vec, vec],
        out_specs=[in_spec, st_spec],
        out_shape=[
            jax.ShapeDtypeStruct((b, n_slab, t, LANES), BF16),
            jax.ShapeDtypeStruct((b, n_slab, LANES, LANES), F32),
        ],
        scratch_shapes=[pltpu.VMEM((n_slab, LANES, LANES), F32)],
        compiler_params=_params(("parallel", "arbitrary")),
        name="rwkv_chunks",
    )(*prep, s0_bd, lnx_w, lnx_b)


def _state_to_blockdiag(s):
    b, h, n, _ = s.shape
    s = s.reshape(b, h // 2, 2, n, n)
    z = jnp.zeros_like(s[:, :, 0])
    top = jnp.concatenate([s[:, :, 0], z], axis=-1)
    bot = jnp.concatenate([z, s[:, :, 1]], axis=-1)
    return jnp.concatenate([top, bot], axis=-2)


def _state_from_blockdiag(sbd):
    b, hs, n2, _ = sbd.shape
    n = n2 // 2
    return jnp.stack([sbd[:, :, :n, :n], sbd[:, :, n:, n:]], axis=2).reshape(b, 2 * hs, n, n)


def _route(logits, n_groups, n_per_group):
    ne = n_groups * n_per_group
    lane = lax.broadcasted_iota(jnp.int32, logits.shape, 1)
    big = jnp.int32(1 << 30)
    is_grp = (lane >= ne) & (lane < ne + n_groups)
    lg = jnp.where(is_grp, logits, NEG_INF)
    mg = jnp.max(lg, axis=-1, keepdims=True)
    gi = jnp.min(jnp.where(is_grp & (lg == mg), lane, big), axis=-1, keepdims=True) - ne
    eg = jnp.where(is_grp, jnp.exp(lg - mg), 0.0)
    p_sel = 1.0 / jnp.sum(eg, axis=-1, keepdims=True)
    in_grp = (lane < ne) & ((lane // n_per_group) == gi)
    le = jnp.where(in_grp, logits, NEG_INF)
    me = jnp.max(le, axis=-1, keepdims=True)
    ee = jnp.where(in_grp, jnp.exp(le - me), 0.0)
    pe = ee / jnp.sum(ee, axis=-1, keepdims=True)
    v1 = jnp.max(jnp.where(in_grp, pe, -1.0), axis=-1, keepdims=True)
    i1 = jnp.min(jnp.where(in_grp & (pe == v1), lane, big), axis=-1, keepdims=True)
    rest = in_grp & (lane != i1)
    v2 = jnp.max(jnp.where(rest, pe, -1.0), axis=-1, keepdims=True)
    i2 = jnp.min(jnp.where(rest & (pe == v2), lane, big), axis=-1, keepdims=True)
    tot = v1 + v2
    w1 = (v1 / tot) * p_sel
    w2 = (v2 / tot) * p_sel
    return (jnp.where(lane == 0, w1, 0.0) + jnp.where(lane == 1, w2, 0.0)
            + jnp.where(lane == 2, i1.astype(F32), 0.0) + jnp.where(lane == 3, i2.astype(F32), 0.0))


def _mix_kernel(x_ref, oa_ref, ob_ref, gt_ref, wa_ref, wb_ref, wo_ref, gn_ref, wr_ref, br_ref,
                x2_ref, hn_ref, route_ref, route_t_ref, *, n_groups, n_per_group):
    d = x_ref.shape[1]
    nb, n_slab, t_blk, _ = ob_ref.shape
    y_a = _dot(oa_ref[...], wa_ref[...])
    ob = jnp.concatenate([ob_ref[:, s].reshape(nb * t_blk, LANES) for s in range(n_slab)], axis=1)
    y_b = _dot(ob, wb_ref[...])
    merged = gt_ref[:, :d] * y_a + gt_ref[:, d:] * y_b
    x2 = x_ref[...] + _dot(merged.astype(BF16), wo_ref[...])
    x2_ref[...] = x2
    hn = _rmsnorm(x2, gn_ref[...], NORM_EPS)
    _store_row_tiles(hn_ref, hn)
    logits = _mm(hn, wr_ref[...], 3) + br_ref[...]
    route = _route(logits, n_groups, n_per_group)
    route_ref[...] = route
    route_t_ref[...] = route.T[:ROUTE_ROWS, :]


def _mix(x2d, o_a, o_b, gates, w_a, w_b, w_o, gain_ffn, w_route, b_route, *, n_groups, n_per_group):
    b, n_slab, t, _ = o_b.shape
    n, d = x2d.shape
    if t >= 512:
        nb, t_blk = 1, 512
    else:
        nb, t_blk = min(b, 512 // t), t
    tm = nb * t_blk
    nt = t // t_blk
    full = lambda a: pl.BlockSpec(a.shape, lambda i: (0,) * a.ndim)
    return pl.pallas_call(
        functools.partial(_mix_kernel, n_groups=n_groups, n_per_group=n_per_group),
        grid=(n // tm,),
        in_specs=[
            pl.BlockSpec((tm, d), lambda i: (i, 0)),
            pl.BlockSpec((tm, o_a.shape[1]), lambda i: (i, 0)),
            pl.BlockSpec((nb, n_slab, t_blk, LANES), lambda i: (i // nt, 0, i % nt, 0)),
            pl.BlockSpec((tm, 2 * d), lambda i: (i, 0)),
            full(w_a), full(w_b), full(w_o), full(gain_ffn), full(w_route), full(b_route),
        ],
        out_specs=[
            pl.BlockSpec((tm, d), lambda i: (i, 0)),
            pl.BlockSpec((tm * ROW_TILE, LANES), lambda i: (i, 0)),
            pl.BlockSpec((tm, LANES), lambda i: (i, 0)),
            pl.BlockSpec((ROUTE_ROWS, tm), lambda i: (0, i)),
        ],
        out_shape=[
            jax.ShapeDtypeStruct((n, d), F32),
            jax.ShapeDtypeStruct((n * ROW_TILE, LANES), F32),
            jax.ShapeDtypeStruct((n, LANES), F32),
            jax.ShapeDtypeStruct((ROUTE_ROWS, n), F32),
        ],
        compiler_params=_params(("parallel",)),
        name="mix_route",
    )(x2d, o_a, o_b, gates, w_a, w_b, w_o, gain_ffn, w_route, b_route)


def _rank_kernel(rt_ref, ri_ref, cnt_ref, carry_ref, *, ne_pad):
    i = pl.program_id(0)

    @pl.when(i == 0)
    def _():
        carry_ref[...] = jnp.zeros_like(carry_ref)

    rt = rt_ref[...]
    tt = rt.shape[1]
    e1 = rt[2:3, :].astype(jnp.int32)
    e2 = rt[3:4, :].astype(jnp.int32)
    eid = lax.broadcasted_iota(jnp.int32, (ne_pad, tt), 0)
    m1 = eid == e1
    m2 = eid == e2
    member = jnp.where(m1 | m2, 1.0, 0.0)
    before = jnp.where(lax.broadcasted_iota(jnp.int32, (tt, tt), 0) < lax.broadcasted_iota(jnp.int32, (tt, tt), 1),
                       1.0, 0.0).astype(BF16)
    carry = carry_ref[...]
    rank = _dot(member.astype(BF16), before) + carry[:, :1]
    r1 = jnp.sum(jnp.where(m1, rank, 0.0), axis=0, keepdims=True).astype(jnp.int32)
    r2 = jnp.sum(jnp.where(m2, rank, 0.0), axis=0, keepdims=True).astype(jnp.int32)
    ri_ref[...] = jnp.concatenate([e1, e2, r1, r2, jnp.zeros((ROUTE_ROWS - 4, tt), jnp.int32)], axis=0)
    carry = carry + jnp.sum(member, axis=1, keepdims=True)
    carry_ref[...] = carry

    @pl.when(i == pl.num_programs(0) - 1)
    def _():
        cnt_ref[...] = carry


def _rank(route_t, ne_pad):
    _, n = route_t.shape
    tt = min(512, n)
    return pl.pallas_call(
        functools.partial(_rank_kernel, ne_pad=ne_pad),
        grid=(n // tt,),
        in_specs=[pl.BlockSpec((ROUTE_ROWS, tt), lambda i: (0, i))],
        out_specs=[pl.BlockSpec((ROUTE_ROWS, tt), lambda i: (0, i)),
                   pl.BlockSpec((ne_pad, LANES), lambda i: (0, 0))],
        out_shape=[jax.ShapeDtypeStruct((ROUTE_ROWS, n), jnp.int32),
                   jax.ShapeDtypeStruct((ne_pad, LANES), F32)],
        scratch_shapes=[pltpu.VMEM((ne_pad, LANES), F32)],
        compiler_params=_params(("arbitrary",)),
        name="moe_rank",
    )(route_t)


def _store_row_tiles(ref2, x2d):
    rows, d = x2d.shape
    for j in range(d // LANES):
        ref2[pl.ds(j, rows, stride=d // LANES), :] = x2d[:, j * LANES:(j + 1) * LANES]


def _load_row_tiles(ref2):
    rows = ref2.shape[0] // ROW_TILE
    return jnp.concatenate([ref2[pl.ds(j, rows, stride=ROW_TILE), :] for j in range(ROW_TILE)], axis=1)


def _row_copy(src, src_row, dst, dst_row, sem):
    s0 = pl.multiple_of(src_row * ROW_TILE, ROW_TILE)
    d0 = pl.multiple_of(dst_row * ROW_TILE, ROW_TILE)
    return pltpu.make_async_copy(src.at[pl.ds(s0, ROW_TILE)], dst.at[pl.ds(d0, ROW_TILE)], sem)


def _pos_kernel(starts_ref, ri_ref, po_ref, *, ne):
    ri = ri_ref[...]
    e1, e2 = ri[0:1, :], ri[1:2, :]
    s1 = jnp.zeros_like(e1)
    s2 = jnp.zeros_like(e2)
    for e in range(ne):
        st = starts_ref[e]
        s1 = jnp.where(e1 == e, st, s1)
        s2 = jnp.where(e2 == e, st, s2)
    po_ref[...] = jnp.concatenate([s1 + ri[2:3, :], s2 + ri[3:4, :],
                                   jnp.zeros((ROUTE_ROWS - 2, ri.shape[1]), jnp.int32)], axis=0)


def _positions(starts, route_i, ne):
    _, n = route_i.shape
    tt = min(2048, n)
    return pl.pallas_call(
        functools.partial(_pos_kernel, ne=ne),
        grid_spec=pltpu.PrefetchScalarGridSpec(
            num_scalar_prefetch=1,
            grid=(n // tt,),
            in_specs=[pl.BlockSpec((ROUTE_ROWS, tt), lambda i, st: (0, i))],
            out_specs=pl.BlockSpec((ROUTE_ROWS, tt), lambda i, st: (0, i)),
        ),
        out_shape=jax.ShapeDtypeStruct((ROUTE_ROWS, n), jnp.int32),
        compiler_params=_params(("arbitrary",)),
        name="moe_positions",
    )(starts, route_i)


def _scatter_kernel(pos_ref, hn_ref, xs_in_ref, xs_ref, sem):
    del xs_in_ref
    tt = pos_ref.shape[1]

    def issue(t, carry):
        _row_copy(hn_ref, t, xs_ref, pos_ref[0, t], sem).start()
        _row_copy(hn_ref, t, xs_ref, pos_ref[1, t], sem).start()
        return carry

    lax.fori_loop(0, tt, issue, 0, unroll=8)

    def drain(t, carry):
        _row_copy(hn_ref, 0, xs_ref, 0, sem).wait()
        _row_copy(hn_ref, 0, xs_ref, 0, sem).wait()
        return carry

    lax.fori_loop(0, tt, drain, 0, unroll=8)


def _scatter(pos, hn_rt, n_rows):
    n = hn_rt.shape[0] // ROW_TILE
    tt = min(512, n)
    xs0 = jnp.zeros((n_rows * ROW_TILE, LANES), hn_rt.dtype)
    return pl.pallas_call(
        _scatter_kernel,
        grid=(n // tt,),
        in_specs=[
            pl.BlockSpec((ROUTE_ROWS, tt), lambda i: (0, i), memory_space=pltpu.SMEM),
            pl.BlockSpec((tt * ROW_TILE, LANES), lambda i: (i, 0)),
            pl.BlockSpec(memory_space=pl.ANY),
        ],
        out_specs=pl.BlockSpec(memory_space=pl.ANY),
        out_shape=jax.ShapeDtypeStruct((n_rows * ROW_TILE, LANES), hn_rt.dtype),
        scratch_shapes=[pltpu.SemaphoreType.DMA(())],
        input_output_aliases={2: 0},
        compiler_params=_params(("arbitrary",)),
        name="moe_scatter",
    )(pos, hn_rt, xs0)


def _expert_kernel(te_ref, nu_ref, xs_ref, w1_ref, w3_ref, w2_ref, o_ref):
    i = pl.program_id(0)

    @pl.when(i < nu_ref[0])
    def _():
        x = _load_row_tiles(xs_ref).astype(BF16)
        up = _dot(x, w1_ref[...].astype(BF16))
        lin = _dot(x, w3_ref[...].astype(BF16))
        z = (up * _sigmoid(up)) * lin
        _store_row_tiles(o_ref, _dot(z.astype(BF16), w2_ref[...].astype(BF16)))

    @pl.when(i >= nu_ref[0])
    def _():
        o_ref[...] = jnp.zeros_like(o_ref)


def _experts(tile_expert, n_used, xs_rt, w1, w3, w2, tm):
    n_rows = xs_rt.shape[0] // ROW_TILE
    _, d, f = w1.shape
    last = lambda i, nu: jnp.minimum(i, jnp.maximum(nu[0] - 1, 0))
    return pl.pallas_call(
        _expert_kernel,
        grid_spec=pltpu.PrefetchScalarGridSpec(
            num_scalar_prefetch=2,
            grid=(n_rows // tm,),
            in_specs=[
                pl.BlockSpec((tm * ROW_TILE, LANES), lambda i, te, nu: (last(i, nu), 0)),
                pl.BlockSpec((None, d, f), lambda i, te, nu: (te[i], 0, 0)),
                pl.BlockSpec((None, d, f), lambda i, te, nu: (te[i], 0, 0)),
                pl.BlockSpec((None, f, d), lambda i, te, nu: (te[i], 0, 0)),
            ],
            out_specs=pl.BlockSpec((tm * ROW_TILE, LANES), lambda i, te, nu: (i, 0)),
        ),
        out_shape=jax.ShapeDtypeStruct((n_rows * ROW_TILE, LANES), F32),
        compiler_params=_params(("arbitrary",)),
        name="moe_experts",
    )(tile_expert, n_used, xs_rt, w1, w3, w2)


def _combine_kernel(pos_ref, pos_next_ref, rf_ref, x2_ref, gf_ref, o_ref, y_ref, buf1, buf2, sems):
    i = pl.program_id(0)
    tt = pos_ref.shape[1]
    slot = i % 2

    def gather(p_ref, s):
        def issue(t, carry):
            _row_copy(o_ref, p_ref[0, t], buf1.at[s], t, sems.at[s]).start()
            _row_copy(o_ref, p_ref[1, t], buf2.at[s], t, sems.at[s]).start()
            return carry

        lax.fori_loop(0, tt, issue, 0, unroll=8)

    @pl.when(i == 0)
    def _():
        gather(pos_ref, 0)

    @pl.when(i + 1 < pl.num_programs(0))
    def _():
        gather(pos_next_ref, 1 - slot)

    def drain(t, carry):
        _row_copy(o_ref, 0, buf1.at[slot], 0, sems.at[slot]).wait()
        _row_copy(o_ref, 0, buf2.at[slot], 0, sems.at[slot]).wait()
        return carry

    lax.fori_loop(0, tt, drain, 0, unroll=8)
    rf = rf_ref[...]
    y = (x2_ref[...] + rf[:, 0:1] * _load_row_tiles(buf1.at[slot])
         + rf[:, 1:2] * _load_row_tiles(buf2.at[slot]))
    y_ref[...] = _rmsnorm(y, gf_ref[...], NORM_EPS)


def _combine(pos, route_f, x2, gain_final, o_rt):
    n, d = x2.shape
    tt = min(256, n)
    steps = n // tt
    return pl.pallas_call(
        _combine_kernel,
        grid=(steps,),
        in_specs=[
            pl.BlockSpec((ROUTE_ROWS, tt), lambda i: (0, i), memory_space=pltpu.SMEM),
            pl.BlockSpec((ROUTE_ROWS, tt), lambda i: (0, jnp.minimum(i + 1, steps - 1)), memory_space=pltpu.SMEM),
            pl.BlockSpec((tt, LANES), lambda i: (i, 0)),
            pl.BlockSpec((tt, d), lambda i: (i, 0)),
            pl.BlockSpec((1, d), lambda i: (0, 0)),
            pl.BlockSpec(memory_space=pl.ANY),
        ],
        out_specs=pl.BlockSpec((tt, d), lambda i: (i, 0)),
        out_shape=jax.ShapeDtypeStruct((n, d), F32),
        scratch_shapes=[pltpu.VMEM((2, tt * ROW_TILE, LANES), F32), pltpu.VMEM((2, tt * ROW_TILE, LANES), F32),
                        pltpu.SemaphoreType.DMA((2,))],
        compiler_params=_params(("arbitrary",)),
        name="moe_combine",
    )(pos, pos, route_f, x2, gain_final, o_rt)


def _moe(hn_rt, route_f, route_t, w1, w3, w2, x2, gain_final):
    n = hn_rt.shape[0] // ROW_TILE
    ne = w1.shape[0]
    ne_pad = -(-ne // 8) * 8
    tm = EXPERT_TILE if 2 * n >= 2 * ne * EXPERT_TILE else EXPERT_TILE // 4
    n_tiles = -(-2 * n // tm) + ne
    route_i, counts = _rank(route_t, ne_pad)
    counts = counts[:ne, 0].astype(jnp.int32)
    tiles = (counts + tm - 1) // tm
    ends = jnp.cumsum(tiles)
    starts = (ends - tiles) * tm
    n_used = ends[-1:]
    tile_expert = jnp.minimum(jnp.sum(jnp.arange(n_tiles)[:, None] >= ends[None, :], axis=1), ne - 1)
    pos = _positions(starts.astype(jnp.int32), route_i, ne)
    xs_rt = _scatter(pos, hn_rt, n_tiles * tm)
    o_rt = _experts(tile_expert.astype(jnp.int32), n_used.astype(jnp.int32), xs_rt, w1, w3, w2, tm)
    return _combine(pos, route_f, x2, gain_final, o_rt)


def _run_path(x, prev_row, s0, cache, wts, dims):
    b, t, d = x.shape
    x2d = x.reshape(b * t, d)
    q_scale = dims["da"] ** -0.5 * math.log2(math.e)
    gates = _gate_proj(x2d, wts["norm_mix"], wts["w_gates"])

    if cache is None:
        qkv, kf, vf, qt, vt = _qkv_proj(x2d, wts["norm_mix"], wts["w_qkv"], q_scale, True)
        qkv3 = qkv.reshape(b, t, -1)
        tiles = lambda a: a.reshape(b, t // a.shape[2], a.shape[1], a.shape[2])
        o_a = _attn_prompt(qkv3, tiles(qt), tiles(vt), wts["lams"], wts["subln_w"].reshape(-1, 1),
                           n_heads=dims["ha"], chunk=dims["chunk"], lam_init=dims["lam_init"])
    else:
        qkv, kf, vf = _qkv_proj(x2d, wts["norm_mix"], wts["w_qkv"], q_scale, False)
        qkv3 = qkv.reshape(b, t, -1)
        o_a = _attn_sample(qkv3, cache[0], cache[1], wts["lams"], wts["subln_w"], n_heads=dims["ha"],
                           lam_init=dims["lam_init"])

    *prep, shift_new = _rwkv_prep(x, wts["norm_mix"], wts["w_pb"], prev_row, wts["mu"], wts["w0"], wts["w2p"],
                                  wts["a0"], wts["a2p"], wts["g2"], wts["k_k"], wts["k_a"], wts["r_k"],
                                  wb=dims["wb"])
    o_b, s_new = _rwkv_chunks(prep, _state_to_blockdiag(s0), wts["lnx_w"], wts["lnx_b"])

    x2, hn, route_f, route_t = _mix(x2d, o_a.reshape(b * t, -1), o_b, gates, wts["w_a_out"], wts["w_b_out"],
                                    wts["w_o"], wts["norm_ffn"], wts["w_route"], wts["b_route"],
                                    n_groups=dims["n_groups"], n_per_group=dims["n_per_group"])
    y = _moe(hn, route_f, route_t, wts["moe_w1"], wts["moe_w3"], wts["moe_w2"], x2, wts["norm_final"])

    ha = dims["ha"]
    return (y.reshape(b, t, d), kf.reshape(1, b, t, ha, -1), vf.reshape(1, b, t, ha, -1),
            shift_new[None], _state_from_blockdiag(s_new)[None])


def kernel(x_prompt, x_sample, cache_attn_k, cache_attn_v, state_rwkv_shift, state_rwkv_wkv, norm_mix, w_in, lambda_q1, lambda_k1, lambda_q2, lambda_k2, subln_w, w_a_out, rwkv_mu, rwkv_w0, rwkv_w2, rwkv_a0, rwkv_a2, rwkv_g2, rwkv_k_k, rwkv_k_a, rwkv_r_k, rwkv_lnx_w, rwkv_lnx_b, w_b_out, w_o, norm_ffn, moe_w_group, moe_b_group, moe_w_router, moe_b_router, moe_w1, moe_w3, moe_w2, norm_final):
    assert w_in.shape[0] == 1, "single-layer trunk"
    l = 0
    d = x_prompt.shape[-1]
    ha, dva = cache_attn_k.shape[3], cache_attn_v.shape[4]
    wa = ha * dva
    hb, db = state_rwkv_wkv.shape[2], state_rwkv_wkv.shape[3]
    wb = hb * db
    lora_w, lora_a, lora_g = rwkv_w2.shape[1], rwkv_a2.shape[1], rwkv_g2.shape[1]
    rwkv_cols = 3 * wb + lora_w + lora_a + lora_g
    n_groups = moe_w_group.shape[-1]
    n_per_group = moe_w_router.shape[-1] // n_groups
    ne = n_groups * n_per_group
    assert db == HEAD_B and lora_w + lora_a == LANES and lora_g == LANES and dva == LANES
    assert ne + n_groups <= LANES

    row = lambda v: v.reshape(1, -1).astype(F32)
    w_l = w_in[l]
    zeros_w = jnp.zeros((lora_a, wb), F32)
    zeros_a = jnp.zeros((lora_w, wb), F32)
    w_route = jnp.zeros((d, LANES), F32)
    w_route = w_route.at[:, :ne].set(moe_w_router[l]).at[:, ne:ne + n_groups].set(moe_w_group[l])
    b_route = jnp.zeros((1, LANES), F32)
    b_route = b_route.at[0, :ne].set(moe_b_router[l]).at[0, ne:ne + n_groups].set(moe_b_group[l])
    f = moe_w1.shape[-1]
    wts = dict(
        norm_mix=row(norm_mix[l]),
        w_qkv=w_l[:, :3 * wa].astype(BF16),
        w_pb=w_l[:, 3 * wa:3 * wa + rwkv_cols].astype(BF16),
        w_gates=w_l[:, 3 * wa + rwkv_cols:].astype(BF16),
        lams=[row(lambda_q1[l]), row(lambda_k1[l]), row(lambda_q2[l]), row(lambda_k2[l])],
        subln_w=row(subln_w[l]),
        mu=row(rwkv_mu[l]), w0=row(rwkv_w0[l]), a0=row(rwkv_a0[l]),
        w2p=jnp.concatenate([rwkv_w2[l], zeros_w], axis=0),
        a2p=jnp.concatenate([zeros_a, rwkv_a2[l]], axis=0),
        g2=rwkv_g2[l].astype(F32),
        k_k=row(rwkv_k_k[l]), k_a=row(rwkv_k_a[l]), r_k=row(rwkv_r_k[l]),
        lnx_w=row(rwkv_lnx_w[l]), lnx_b=row(rwkv_lnx_b[l]),
        w_a_out=w_a_out[l].astype(BF16), w_b_out=w_b_out[l].astype(BF16), w_o=w_o[l].astype(BF16),
        norm_ffn=row(norm_ffn[l]), w_route=w_route, b_route=b_route,
        moe_w1=moe_w1[l].reshape(ne, d, f), moe_w3=moe_w3[l].reshape(ne, d, f), moe_w2=moe_w2[l].reshape(ne, f, d),
        norm_final=row(norm_final),
    )
    dims = dict(ha=ha, da=dva // 2, wb=wb, chunk=CHUNK_B, lam_init=0.8 - 0.6 * math.exp(-0.3 * l),
                n_groups=n_groups, n_per_group=n_per_group)

    bp = x_prompt.shape[0]
    yp, kp, vp, shp, wkp = _run_path(
        x_prompt.astype(F32), jnp.zeros((bp, 1, rwkv_cols), F32), jnp.zeros((bp, hb, db, db), F32), None, wts, dims)

    bs, past = cache_attn_k.shape[1], cache_attn_k.shape[2]
    cache = (cache_attn_k[l].reshape(bs, past * ha, dva), cache_attn_v[l].reshape(bs, past * ha, dva))
    ys, ks_, vs_, shs, wks = _run_path(
        x_sample.astype(F32), state_rwkv_shift[l].astype(F32), state_rwkv_wkv[l].astype(F32), cache, wts, dims)

    return (yp, ys, kp, vp, shp, wkp, ks_, vs_, shs, wks)
```

```python
import functools
import math

import jax
import jax.numpy as jnp
from jax import lax
from jax.experimental import pallas as pl
from jax.experimental.pallas import tpu as pltpu

F32 = jnp.float32
BF16 = jnp.bfloat16

LANES = 128
HEAD_B = 64
CHUNK_B = 64
CHUNKS_PER_STEP = 4
ATTN_TILE = 512
ROW_TILE = 8
ROUTE_ROWS = 8
EXPERT_TILE = 512
NORM_EPS = 1e-6
DIFF_EPS = 1e-5
GN_EPS = 64e-5
NEG_INF = -1e30
VMEM_LIMIT = 56 * 1024 * 1024

NN = (((1,), (0,)), ((), ()))
NT = (((1,), (1,)), ((), ()))
TN = (((0,), (0,)), ((), ()))


def _dot(a, b, dims=NN):
    return lax.dot_general(a, b, dims, preferred_element_type=F32)


def _split2(a):
    hi = a.astype(BF16)
    lo = (a - hi.astype(F32)).astype(BF16)
    return hi, lo


def _split3(a):
    hi = a.astype(BF16)
    r = a - hi.astype(F32)
    mid = r.astype(BF16)
    lo = (r - mid.astype(F32)).astype(BF16)
    return hi, mid, lo


def _mm(a, b, passes=1, dims=NN):
    if passes == 1:
        return _dot(a.astype(BF16), b.astype(BF16), dims)
    a_hi, a_lo = _split2(a)
    if passes == 2:
        b_hi = b.astype(BF16)
        return _dot(a_hi, b_hi, dims) + _dot(a_lo, b_hi, dims)
    b_hi, b_lo = _split2(b)
    return _dot(a_hi, b_hi, dims) + (_dot(a_hi, b_lo, dims) + _dot(a_lo, b_hi, dims))


def _mm_exact_rhs2(a, b_bf16, dims=NN):
    hi, lo = _split2(a)
    return _dot(hi, b_bf16, dims) + _dot(lo, b_bf16, dims)


def _rmsnorm(x, g, eps):
    return x * lax.rsqrt(jnp.mean(x * x, axis=-1, keepdims=True) + eps) * g


def _sigmoid(x):
    return 1.0 / (1.0 + jnp.exp(-x))


def _params(sem):
    return pltpu.CompilerParams(dimension_semantics=sem, vmem_limit_bytes=VMEM_LIMIT)


def _qkv_kernel(x_ref, g_ref, w_ref, qkv_ref, kf_ref, vf_ref, *rest, q_scale, transposed):
    if transposed:
        qt_ref, vt_ref, h_ref = rest
    else:
        (h_ref,) = rest
    j = pl.program_id(1)

    @pl.when(j == 0)
    def _():
        h_ref[...] = _rmsnorm(x_ref[...], g_ref[...], NORM_EPS).astype(BF16)

    p = _dot(h_ref[...], w_ref[...])

    @pl.when(j == 0)
    def _():
        q = p * q_scale
        qkv_ref[...] = q.astype(BF16)
        if transposed:
            qt_ref[...] = q.T.astype(BF16)

    @pl.when(j == 1)
    def _():
        qkv_ref[...] = p.astype(BF16)
        kf_ref[...] = p

    @pl.when(j == 2)
    def _():
        qkv_ref[...] = p.astype(BF16)
        vf_ref[...] = p
        if transposed:
            vt_ref[...] = p.T.astype(BF16)


def _qkv_proj(x2d, gain, w_qkv, q_scale, transposed):
    n, d = x2d.shape
    wa = w_qkv.shape[1] // 3
    tm = min(ATTN_TILE, n)
    out_specs = [
        pl.BlockSpec((tm, wa), lambda i, j: (i, j)),
        pl.BlockSpec((tm, wa), lambda i, j: (i, 0)),
        pl.BlockSpec((tm, wa), lambda i, j: (i, 0)),
    ]
    out_shape = [
        jax.ShapeDtypeStruct((n, 3 * wa), BF16),
        jax.ShapeDtypeStruct((n, wa), F32),
        jax.ShapeDtypeStruct((n, wa), F32),
    ]
    if transposed:
        out_specs += [pl.BlockSpec((None, wa, tm), lambda i, j: (i, 0, 0))] * 2
        out_shape += [jax.ShapeDtypeStruct((n // tm, wa, tm), BF16)] * 2
    return pl.pallas_call(
        functools.partial(_qkv_kernel, q_scale=q_scale, transposed=transposed),
        grid=(n // tm, 3),
        in_specs=[
            pl.BlockSpec((tm, d), lambda i, j: (i, 0)),
            pl.BlockSpec((1, d), lambda i, j: (0, 0)),
            pl.BlockSpec((d, wa), lambda i, j: (0, j)),
        ],
        out_specs=out_specs,
        out_shape=out_shape,
        scratch_shapes=[pltpu.VMEM((tm, d), BF16)],
        compiler_params=_params(("parallel", "arbitrary")),
        name="qkv_proj",
    )(x2d, gain, w_qkv)


def _gate_kernel(x_ref, g_ref, w_ref, o_ref):
    h = _rmsnorm(x_ref[...], g_ref[...], NORM_EPS).astype(BF16)
    o_ref[...] = _sigmoid(_dot(h, w_ref[...])).astype(o_ref.dtype)


def _gate_proj(x2d, gain, w):
    n, d = x2d.shape
    c = w.shape[1]
    tm = min(512, n)
    return pl.pallas_call(
        _gate_kernel,
        grid=(n // tm,),
        in_specs=[
            pl.BlockSpec((tm, d), lambda i: (i, 0)),
            pl.BlockSpec((1, d), lambda i: (0, 0)),
            pl.BlockSpec((d, c), lambda i: (0, 0)),
        ],
        out_specs=pl.BlockSpec((tm, c), lambda i: (i, 0)),
        out_shape=jax.ShapeDtypeStruct((n, c), BF16),
        compiler_params=_params(("parallel",)),
        name="gate_proj",
    )(x2d, gain, w)


def _lambda(lq1_ref, lk1_ref, lq2_ref, lk2_ref, lam_init):
    s1 = jnp.sum(lq1_ref[...] * lk1_ref[...], axis=-1, keepdims=True)
    s2 = jnp.sum(lq2_ref[...] * lk2_ref[...], axis=-1, keepdims=True)
    return jnp.exp(s1) - jnp.exp(s2) + lam_init


def _split_q(q):
    lane = lax.broadcasted_iota(jnp.int32, (1, q.shape[1]), 1)
    first = lane < (q.shape[1] // 2)
    zero = jnp.zeros_like(q)
    return jnp.where(first, q, zero), jnp.where(first, zero, q)


def _attn_prompt_kernel(lq1_ref, lk1_ref, lq2_ref, lk2_ref, subc_ref, qt_ref, k_ref, vt_ref, o_ref,
                        s0_ref, s1_ref, s2_ref, s3_ref, mt_ref, m_ref, l_ref, acc_ref, *, chunk, lam_init):
    i = pl.program_id(2)
    dv, tq = qt_ref.shape
    tk = vt_ref.shape[2]
    lam = _lambda(lq1_ref, lk1_ref, lq2_ref, lk2_ref, lam_init)
    qt = qt_ref[...]
    feat = lax.broadcasted_iota(jnp.int32, (dv, 1), 0)
    zq = jnp.zeros_like(qt)
    q_maps = (jnp.where(feat < dv // 2, qt, zq), jnp.where(feat < dv // 2, zq, qt))
    acc_ref[...] = jnp.zeros_like(acc_ref)
    rb = 64
    kb = min(256, tk)

    m_ref[...] = jnp.full(m_ref.shape, NEG_INF, F32)
    l_ref[...] = jnp.zeros_like(l_ref)
    buf_a, buf_b = (s0_ref, s1_ref), (s2_ref, s3_ref)

    def scores(j, buf, masked):
        ks = k_ref[pl.ds(pl.multiple_of(j * tk, tk), tk), :]
        for mi in range(2):
            s = _dot(ks, q_maps[mi])
            if masked:
                kc = lax.broadcasted_iota(jnp.int32, (tk, tq), 0) // chunk
                qc = lax.broadcasted_iota(jnp.int32, (tk, tq), 1) // chunk
                s = jnp.where(kc <= qc, s, NEG_INF)
            buf[mi][...] = s
            mt_ref[2 * (buf is buf_b) + mi] = jnp.max(s, axis=0, keepdims=True)

    def softmax_pv(j, buf):
        vt = vt_ref[j]
        m_old = [m_ref[mi] for mi in range(2)]
        m_new = [jnp.maximum(m_old[mi], mt_ref[2 * (buf is buf_b) + mi]) for mi in range(2)]
        alpha = [jnp.exp2(m_old[mi] - m_new[mi]) for mi in range(2)]
        lsum = [jnp.zeros((8, tq), F32) for _ in range(2)]
        pv = [None, None]
        for hf in range(tk // kb):
            for mi in range(2):
                blocks = []
                for r in range(kb // rb):
                    lo = hf * kb + r * rb
                    p = jnp.exp2(buf[mi][lo:lo + rb, :] - m_new[mi])
                    lsum[mi] = lsum[mi] + jnp.sum(p.reshape(rb // 8, 8, tq), axis=0)
                    blocks.append(p.astype(BF16))
                part = _dot(vt[:, hf * kb:(hf + 1) * kb], jnp.concatenate(blocks, axis=0))
                pv[mi] = part if pv[mi] is None else pv[mi] + part
        for mi in range(2):
            acc_ref[mi] = acc_ref[mi] * alpha[mi] + pv[mi]
            m_ref[mi] = m_new[mi]
            l_ref[mi] = alpha[mi] * l_ref[mi] + lsum[mi]

    def stage_even(j, next_masked):
        scores(j + 1, buf_b, next_masked)
        softmax_pv(j, buf_a)

    def stage_odd(j, next_masked):
        scores(j + 1, buf_a, next_masked)
        softmax_pv(j, buf_b)

    @pl.when(i == 0)
    def _():
        scores(0, buf_a, True)
        softmax_pv(0, buf_a)

    @pl.when(i > 0)
    def _():
        scores(0, buf_a, False)

    def pair(jj, carry):
        stage_even(2 * jj, False)
        stage_odd(2 * jj + 1, False)
        return carry

    lax.fori_loop(0, jnp.maximum(i - 1, 0) // 2, pair, 0)

    @pl.when(i % 2 == 1)
    def _():
        stage_even(i - 1, True)
        softmax_pv(i, buf_b)

    @pl.when((i % 2 == 0) & (i > 0))
    def _():
        stage_even(i - 2, False)
        stage_odd(i - 1, True)
        softmax_pv(i, buf_a)

    l1 = jnp.sum(l_ref[0], axis=0, keepdims=True)
    l2 = jnp.sum(l_ref[1], axis=0, keepdims=True)
    ot = acc_ref[0] / l1 - lam * (acc_ref[1] / l2)
    ms = jnp.mean(ot * ot, axis=0, keepdims=True)
    ot = ot * lax.rsqrt(ms + DIFF_EPS) * subc_ref[...] * (1.0 - lam_init)
    o_ref[...] = ot.T.astype(o_ref.dtype)


def _attn_prompt(qkv, qt, vt, lams, sub_col, *, n_heads, chunk, lam_init):
    b, t, c3 = qkv.shape
    wa = c3 // 3
    dv = wa // n_heads
    nk, tile = qt.shape[1], qt.shape[3]
    lam_specs = [pl.BlockSpec((1, lams[0].shape[1]), lambda bb, h, i: (0, 0)) for _ in range(4)]
    return pl.pallas_call(
        functools.partial(_attn_prompt_kernel, chunk=chunk, lam_init=lam_init),
        grid=(b, n_heads, nk),
        in_specs=lam_specs + [
            pl.BlockSpec((dv, 1), lambda bb, h, i: (0, 0)),
            pl.BlockSpec((None, None, dv, tile), lambda bb, h, i: (bb, i, h, 0)),
            pl.BlockSpec((None, t, dv), lambda bb, h, i: (bb, 0, n_heads + h)),
            pl.BlockSpec((None, nk, dv, tile), lambda bb, h, i: (bb, 0, h, 0)),
        ],
        out_specs=pl.BlockSpec((None, tile, dv), lambda bb, h, i: (bb, i, h)),
        out_shape=jax.ShapeDtypeStruct((b, t, wa), BF16),
        scratch_shapes=[pltpu.VMEM((tile, tile), F32)] * 4
        + [pltpu.VMEM((4, 1, tile), F32),
           pltpu.VMEM((2, 1, tile), F32),
           pltpu.VMEM((2, 8, tile), F32),
           pltpu.VMEM((2, dv, tile), F32)],
        compiler_params=_params(("parallel", "parallel", "arbitrary")),
        name="diff_attn_prompt",
    )(*lams, sub_col, qt, qkv, vt)


def _attn_sample_kernel(lq1_ref, lk1_ref, lq2_ref, lk2_ref, sub_ref, q_ref, kn_ref, vn_ref, ck_ref, cv_ref,
                        o_ref, *, n_heads, lam_init):
    lam = _lambda(lq1_ref, lk1_ref, lq2_ref, lk2_ref, lam_init)
    tq = q_ref.shape[0]
    dv = q_ref.shape[1] // n_heads
    past = ck_ref.shape[0] // n_heads
    ones_c = jnp.ones((past, dv), BF16)
    ones_n = jnp.ones((tq, dv), BF16)
    for h in range(n_heads):
        cols = slice(h * dv, (h + 1) * dv)
        q1, q2 = _split_q(q_ref[:, cols].astype(F32))
        w = jnp.concatenate([q1, q2], axis=0).T.astype(BF16)
        ck = ck_ref[pl.ds(h, past, stride=n_heads), :].astype(BF16)
        cv = cv_ref[pl.ds(h, past, stride=n_heads), :].astype(BF16)
        s_c = _dot(ck, w)
        s_n = _dot(kn_ref[:, cols], w)
        m = jnp.maximum(jnp.max(s_c, axis=0, keepdims=True), jnp.max(s_n, axis=0, keepdims=True))
        p_c = jnp.exp2(s_c - m).astype(BF16)
        p_n = jnp.exp2(s_n - m).astype(BF16)
        acc = (_dot(p_c, jnp.concatenate([cv, ones_c], axis=1), TN)
               + _dot(p_n, jnp.concatenate([vn_ref[:, cols], ones_n], axis=1), TN))
        o1 = acc[:tq, :dv] / acc[:tq, dv:]
        o2 = acc[tq:, :dv] / acc[tq:, dv:]
        o = o1 - lam * o2
        o_ref[:, cols] = (_rmsnorm(o, sub_ref[...], DIFF_EPS) * (1.0 - lam_init)).astype(o_ref.dtype)


def _attn_sample(qkv, cache_k, cache_v, lams, sub_w, *, n_heads, lam_init):
    b, t, c3 = qkv.shape
    wa = c3 // 3
    dv = wa // n_heads
    rows = cache_k.shape[1]
    lam_specs = [pl.BlockSpec((1, lams[0].shape[1]), lambda bb: (0, 0)) for _ in range(4)]
    return pl.pallas_call(
        functools.partial(_attn_sample_kernel, n_heads=n_heads, lam_init=lam_init),
        grid=(b,),
        in_specs=lam_specs + [
            pl.BlockSpec((1, dv), lambda bb: (0, 0)),
            pl.BlockSpec((None, t, wa), lambda bb: (bb, 0, 0)),
            pl.BlockSpec((None, t, wa), lambda bb: (bb, 0, 1)),
            pl.BlockSpec((None, t, wa), lambda bb: (bb, 0, 2)),
            pl.BlockSpec((None, rows, dv), lambda bb: (bb, 0, 0)),
            pl.BlockSpec((None, rows, dv), lambda bb: (bb, 0, 0)),
        ],
        out_specs=pl.BlockSpec((None, t, wa), lambda bb: (bb, 0, 0)),
        out_shape=jax.ShapeDtypeStruct((b, t, wa), BF16),
        compiler_params=_params(("parallel",)),
        name="diff_attn_sample",
    )(*lams, sub_w, qkv, qkv, qkv, cache_k, cache_v)


def _seg_ones(n=LANES, seg=HEAD_B):
    r = lax.broadcasted_iota(jnp.int32, (n, n), 0) // seg
    c = lax.broadcasted_iota(jnp.int32, (n, n), 1) // seg
    return jnp.where(r == c, 1.0, 0.0).astype(BF16)


def _rwkv_prep_kernel(x_ref, gn_ref, wpb_ref, prev_ref, mu_ref, w0_ref, w2p_ref, a0_ref, a2p_ref, g2_ref, kk_ref,
                      ka_ref, rk_ref, r_o, k_o, v_o, ld_o, na_o, b_o, g_o, bon_o, shift_o, carry_ref, *, wb):
    i = pl.program_id(1)

    @pl.when(i == 0)
    def _():
        carry_ref[...] = prev_ref[...]

    h = _rmsnorm(x_ref[...], gn_ref[...], NORM_EPS).astype(BF16)
    pb = _dot(h, wpb_ref[...])
    shift_o[...] = pb[pb.shape[0] - 1:, :]
    tm = pb.shape[0]
    row = lax.broadcasted_iota(jnp.int32, (tm, 1), 0)
    prev = jnp.where(row == 0, carry_ref[...], pltpu.roll(pb, 1, axis=0))
    carry_ref[...] = pb[tm - 1:tm, :]
    xs = pb + (prev - pb) * mu_ref[...]

    n_slab = wb // LANES
    x_wa = xs[:, 3 * wb:3 * wb + LANES]
    x_g = xs[:, 3 * wb + LANES:3 * wb + 2 * LANES]
    lw = _mm(jnp.tanh(x_wa), w2p_ref[...], 2)
    la = _mm(x_wa, a2p_ref[...])
    g = _mm(_sigmoid(x_g), g2_ref[...])
    seg = _seg_ones(2 * LANES)

    for s in range(n_slab):
        sl = slice(s * LANES, (s + 1) * LANES)
        r = xs[:, s * LANES:(s + 1) * LANES]
        k = xs[:, wb + s * LANES:wb + (s + 1) * LANES]
        v = xs[:, 2 * wb + s * LANES:2 * wb + (s + 1) * LANES]
        z = -(w0_ref[:, sl] + lw[:, sl])
        softplus = jnp.maximum(z, 0.0) + jnp.log1p(jnp.exp(-jnp.abs(z)))
        w_log = -softplus - 0.5
        a = _sigmoid(a0_ref[:, sl] + la[:, sl])
        kk = k * kk_ref[:, sl]
        k2 = k * (1.0 + (a - 1.0) * ka_ref[:, sl])
        sums = _mm_exact_rhs2(jnp.concatenate([kk * kk, r * k2 * rk_ref[:, sl]], axis=1), seg)
        kk = kk / jnp.maximum(jnp.sqrt(sums[:, :LANES]), 1e-12)
        bonus = sums[:, LANES:] * v
        r_o[s] = r.astype(r_o.dtype)
        k_o[s] = k2.astype(k_o.dtype)
        v_o[s] = v.astype(v_o.dtype)
        ld_o[s] = -jnp.exp(w_log)
        na_o[s] = (-kk).astype(na_o.dtype)
        b_o[s] = (kk * a).astype(b_o.dtype)
        g_o[s] = g[:, sl].astype(g_o.dtype)
        bon_o[s] = bonus.astype(bon_o.dtype)


def _rwkv_prep(x, gain, w_pb, prev_row, mu, w0, w2p, a0, a2p, g2, k_k, k_a, r_k, *, wb):
    b, t, d = x.shape
    cols = w_pb.shape[1]
    tm = min(256, t)
    n_slab = wb // LANES
    vec = lambda n: pl.BlockSpec((1, n), lambda bb, i: (0, 0))
    full = lambda a: pl.BlockSpec(a.shape, lambda bb, i: (0, 0))
    out_spec = pl.BlockSpec((None, n_slab, tm, LANES), lambda bb, i: (bb, 0, i, 0))
    sds = lambda dt: jax.ShapeDtypeStruct((b, n_slab, t, LANES), dt)
    out_dtypes = [BF16, BF16, BF16, F32, BF16, BF16, BF16, BF16]
    return pl.pallas_call(
        functools.partial(_rwkv_prep_kernel, wb=wb),
        grid=(b, t // tm),
        in_specs=[
            pl.BlockSpec((None, tm, d), lambda bb, i: (bb, i, 0)),
            vec(d), full(w_pb),
            pl.BlockSpec((None, 1, cols), lambda bb, i: (bb, 0, 0)),
            vec(cols), vec(wb), full(w2p), vec(wb), full(a2p), full(g2), vec(wb), vec(wb), vec(wb),
        ],
        out_specs=[out_spec] * 8 + [pl.BlockSpec((None, 1, cols), lambda bb, i: (bb, 0, 0))],
        out_shape=[sds(dt) for dt in out_dtypes] + [jax.ShapeDtypeStruct((b, 1, cols), F32)],
        scratch_shapes=[pltpu.VMEM((1, cols), F32)],
        compiler_params=_params(("parallel", "arbitrary")),
        name="rwkv_prep",
    )(x, gain, w_pb, prev_row, mu, w0, w2p, a0, a2p, g2, k_k, k_a, r_k)


def _stack2(x, first):
    xb = x.astype(BF16)
    zero = jnp.zeros_like(xb)
    return jnp.concatenate([jnp.where(first, xb, zero), jnp.where(first, zero, xb)], axis=0)


def _rwkv_chunk_kernel(r_ref, k_ref, v_ref, ld_ref, na_ref, b_ref, g_ref, bon_ref, s0_ref, lnw_ref, lnb_ref,
                       y_ref, sT_ref, st_ref, *, cl):
    c = pl.program_id(1)
    n_chunks = pl.num_programs(1)
    n_slab = r_ref.shape[0]
    n_sub = r_ref.shape[1] // cl

    @pl.when(c == 0)
    def _():
        st_ref[...] = s0_ref[...]

    lane = lax.broadcasted_iota(jnp.int32, (1, LANES), 1)
    first = lane < HEAD_B
    ti = lax.broadcasted_iota(jnp.int32, (cl, cl), 0)
    tj = lax.broadcasted_iota(jnp.int32, (cl, cl), 1)
    tri_incl = jnp.where(ti >= tj, 1.0, 0.0).astype(BF16)
    seg = _seg_ones()
    t_i = lax.broadcasted_iota(jnp.int32, (cl, LANES), 0)
    s_i = lax.broadcasted_iota(jnp.int32, (cl, LANES), 1) % HEAD_B
    strict = t_i > s_i
    incl = t_i >= s_i
    zero = jnp.zeros((cl, LANES), F32)
    eye = jnp.where(t_i == s_i, 1.0, 0.0).astype(F32)
    inv_n = 1.0 / HEAD_B

    items = [(s, q) for q in range(n_sub) for s in range(n_slab)]
    rows = lambda q: slice(q * cl, (q + 1) * cl)

    pre = []
    for s, q in items:
        ld = ld_ref[s, rows(q), :]
        hi, mid, lo = _split3(ld)
        cum = _dot(tri_incl, hi) + (_dot(tri_incl, mid) + _dot(tri_incl, lo))
        pre.append((ld, cum))

    stacks = []
    for (s, q), (ld, cum) in zip(items, pre):
        cum_last = cum[cl - 1:cl, :]
        g_incl = jnp.exp(cum)
        g_excl = jnp.exp(cum - ld)
        g_inv = jnp.exp(-cum)
        g_end = jnp.exp(cum_last - cum)
        k = k_ref[s, rows(q), :]
        bv = b_ref[s, rows(q), :]
        stacks.append(dict(
            g_last=jnp.exp(cum_last),
            la=(na_ref[s, rows(q), :] * g_excl).astype(BF16),
            lr=(r_ref[s, rows(q), :] * g_incl).astype(BF16),
            l2=jnp.concatenate([_stack2(bv * g_inv, first), _stack2(k * g_inv, first)], axis=0),
            lh=jnp.concatenate([_stack2(bv * g_end, first), _stack2(k * g_end, first)], axis=0),
            v2=_stack2(v_ref[s, rows(q), :], first)))

    grams = [_dot(jnp.concatenate([d["la"], d["lr"]], axis=0), d["l2"], NT) for d in stacks]
    for d, gram in zip(stacks, grams):
        d["lar"] = jnp.concatenate([d["la"], d["lr"]], axis=0)
        d["g_akrk"] = jnp.concatenate([jnp.where(strict, gram[:cl, LANES:], zero),
                                       jnp.where(incl, gram[cl:, LANES:], zero)], axis=0).astype(BF16)
        d["g_rb"] = jnp.where(incl, gram[cl:, :LANES], zero).astype(BF16)
    n_lev = int(math.log2(cl))
    pows = [jnp.where(strict, gram[:cl, :LANES], zero) for gram in grams]
    invs = [eye + p for p in pows]
    pows = [_dot(p.astype(BF16), _stack2(p, first)) for p in pows]
    for lev in range(1, n_lev):
        last = lev == n_lev - 1
        lhs = [inv.astype(BF16) if last else jnp.concatenate([inv, p], axis=0).astype(BF16)
               for inv, p in zip(invs, pows)]
        prods = [_dot(a, _stack2(p, first)) for a, p in zip(lhs, pows)]
        invs = [inv + pr[:cl, :] for inv, pr in zip(invs, prods)]
        if not last:
            pows = [pr[cl:, :] for pr in prods]
    invs = [inv.astype(BF16) for inv in invs]
    akrk = [_dot(d["g_akrk"], d["v2"]) for d in stacks]

    sts = [st_ref[s] for s in range(n_slab)]
    for q in range(n_sub):
        base = q * n_slab
        stb = [st.astype(BF16) for st in sts]
        ars = [_dot(stacks[base + s]["lar"], stb[s], NT) + akrk[base + s] for s in range(n_slab)]
        us = [_stack2(_dot(invs[base + s], _stack2(ars[s][:cl, :], first)), first) for s in range(n_slab)]
        ys = [ars[s][cl:, :] + _dot(stacks[base + s]["g_rb"], us[s])
              for s in range(n_slab)]
        sts = [sts[s] * stacks[base + s]["g_last"]
               + _dot(jnp.concatenate([us[s], stacks[base + s]["v2"]], axis=0), stacks[base + s]["lh"], TN)
               for s in range(n_slab)]
        y_all = jnp.concatenate(ys, axis=0)
        d_all = y_all - _mm_exact_rhs2(y_all, seg) * inv_n
        var_all = _mm_exact_rhs2(d_all * d_all, seg) * inv_n
        yn_all = d_all * lax.rsqrt(var_all + GN_EPS)
        for s in range(n_slab):
            sl = slice(s * LANES, (s + 1) * LANES)
            yn = yn_all[s * cl:(s + 1) * cl, :] * lnw_ref[:, sl] + lnb_ref[:, sl]
            y_ref[s, rows(q), :] = ((yn + bon_ref[s, rows(q), :]) * g_ref[s, rows(q), :]).astype(y_ref.dtype)
    for s in range(n_slab):
        st_ref[s] = sts[s]

    @pl.when(c == n_chunks - 1)
    def _():
        sT_ref[...] = st_ref[...]


def _rwkv_chunks(prep, s0_bd, lnx_w, lnx_b):
    b, n_slab, t, _ = prep[0].shape
    cl = min(CHUNK_B, t)
    step = min(CHUNKS_PER_STEP * cl, t)
    wb = n_slab * LANES
    in_spec = pl.BlockSpec((None, n_slab, step, LANES), lambda bb, c: (bb, 0, c, 0))
    st_spec = pl.BlockSpec((None, n_slab, LANES, LANES), lambda bb, c: (bb, 0, 0, 0))
    vec = pl.BlockSpec((1, wb), lambda bb, c: (0, 0))
    return pl.pallas_call(
        functools.partial(_rwkv_chunk_kernel, cl=cl),
        grid=(b, t // step),
        in_specs=[in_spec] * 8 + [st_spec, vec, vec],
        out_specs=[in_spec, st_spec],
        out_shape=[
            jax.ShapeDtypeStruct((b, n_slab, t, LANES), BF16),
            jax.ShapeDtypeStruct((b, n_slab, LANES, LANES), F32),
        ],
        scratch_shapes=[pltpu.VMEM((n_slab, LANES, LANES), F32)],
        compiler_params=_params(("parallel", "arbitrary")),
        name="rwkv_chunks",
    )(*prep, s0_bd, lnx_w, lnx_b)


def _state_to_blockdiag(s):
    b, h, n, _ = s.shape
    s = s.reshape(b, h // 2, 2, n, n)
    z = jnp.zeros_like(s[:, :, 0])
    top = jnp.concatenate([s[:, :, 0], z], axis=-1)
    bot = jnp.concatenate([z, s[:, :, 1]], axis=-1)
    return jnp.concatenate([top, bot], axis=-2)


def _state_from_blockdiag(sbd):
    b, hs, n2, _ = sbd.shape
    n = n2 // 2
    return jnp.stack([sbd[:, :, :n, :n], sbd[:, :, n:, n:]], axis=2).reshape(b, 2 * hs, n, n)


def _route(logits, n_groups, n_per_group):
    ne = n_groups * n_per_group
    lane = lax.broadcasted_iota(jnp.int32, logits.shape, 1)
    big = jnp.int32(1 << 30)
    is_grp = (lane >= ne) & (lane < ne + n_groups)
    lg = jnp.where(is_grp, logits, NEG_INF)
    mg = jnp.max(lg, axis=-1, keepdims=True)
    gi = jnp.min(jnp.where(is_grp & (lg == mg), lane, big), axis=-1, keepdims=True) - ne
    eg = jnp.where(is_grp, jnp.exp(lg - mg), 0.0)
    p_sel = 1.0 / jnp.sum(eg, axis=-1, keepdims=True)
    in_grp = (lane < ne) & ((lane // n_per_group) == gi)
    le = jnp.where(in_grp, logits, NEG_INF)
    me = jnp.max(le, axis=-1, keepdims=True)
    ee = jnp.where(in_grp, jnp.exp(le - me), 0.0)
    pe = ee / jnp.sum(ee, axis=-1, keepdims=True)
    v1 = jnp.max(jnp.where(in_grp, pe, -1.0), axis=-1, keepdims=True)
    i1 = jnp.min(jnp.where(in_grp & (pe == v1), lane, big), axis=-1, keepdims=True)
    rest = in_grp & (lane != i1)
    v2 = jnp.max(jnp.where(rest, pe, -1.0), axis=-1, keepdims=True)
    i2 = jnp.min(jnp.where(rest & (pe == v2), lane, big), axis=-1, keepdims=True)
    tot = v1 + v2
    w1 = (v1 / tot) * p_sel
    w2 = (v2 / tot) * p_sel
    return (jnp.where(lane == 0, w1, 0.0) + jnp.where(lane == 1, w2, 0.0)
            + jnp.where(lane == 2, i1.astype(F32), 0.0) + jnp.where(lane == 3, i2.astype(F32), 0.0))


def _mix_kernel(x_ref, oa_ref, ob_ref, gt_ref, wa_ref, wb_ref, wo_ref, gn_ref, wr_ref, br_ref,
                x2_ref, hn_ref, route_ref, route_t_ref, *, n_groups, n_per_group):
    d = x_ref.shape[1]
    nb, n_slab, t_blk, _ = ob_ref.shape
    y_a = _dot(oa_ref[...], wa_ref[...])
    ob = jnp.concatenate([ob_ref[:, s].reshape(nb * t_blk, LANES) for s in range(n_slab)], axis=1)
    y_b = _dot(ob, wb_ref[...])
    merged = gt_ref[:, :d] * y_a + gt_ref[:, d:] * y_b
    x2 = x_ref[...] + _dot(merged.astype(BF16), wo_ref[...])
    x2_ref[...] = x2
    hn = _rmsnorm(x2, gn_ref[...], NORM_EPS)
    _store_row_tiles(hn_ref, hn)
    logits = _mm(hn, wr_ref[...], 3) + br_ref[...]
    route = _route(logits, n_groups, n_per_group)
    route_ref[...] = route
    route_t_ref[...] = route.T[:ROUTE_ROWS, :]


def _mix(x2d, o_a, o_b, gates, w_a, w_b, w_o, gain_ffn, w_route, b_route, *, n_groups, n_per_group):
    b, n_slab, t, _ = o_b.shape
    n, d = x2d.shape
    if t >= 512:
        nb, t_blk = 1, 512
    else:
        nb, t_blk = min(b, 512 // t), t
    tm = nb * t_blk
    nt = t // t_blk
    full = lambda a: pl.BlockSpec(a.shape, lambda i: (0,) * a.ndim)
    return pl.pallas_call(
        functools.partial(_mix_kernel, n_groups=n_groups, n_per_group=n_per_group),
        grid=(n // tm,),
        in_specs=[
            pl.BlockSpec((tm, d), lambda i: (i, 0)),
            pl.BlockSpec((tm, o_a.shape[1]), lambda i: (i, 0)),
            pl.BlockSpec((nb, n_slab, t_blk, LANES), lambda i: (i // nt, 0, i % nt, 0)),
            pl.BlockSpec((tm, 2 * d), lambda i: (i, 0)),
            full(w_a), full(w_b), full(w_o), full(gain_ffn), full(w_route), full(b_route),
        ],
        out_specs=[
            pl.BlockSpec((tm, d), lambda i: (i, 0)),
            pl.BlockSpec((tm * ROW_TILE, LANES), lambda i: (i, 0)),
            pl.BlockSpec((tm, LANES), lambda i: (i, 0)),
            pl.BlockSpec((ROUTE_ROWS, tm), lambda i: (0, i)),
        ],
        out_shape=[
            jax.ShapeDtypeStruct((n, d), F32),
            jax.ShapeDtypeStruct((n * ROW_TILE, LANES), F32),
            jax.ShapeDtypeStruct((n, LANES), F32),
            jax.ShapeDtypeStruct((ROUTE_ROWS, n), F32),
        ],
        compiler_params=_params(("parallel",)),
        name="mix_route",
    )(x2d, o_a, o_b, gates, w_a, w_b, w_o, gain_ffn, w_route, b_route)


def _rank_kernel(rt_ref, ri_ref, cnt_ref, carry_ref, *, ne_pad):
    i = pl.program_id(0)

    @pl.when(i == 0)
    def _():
        carry_ref[...] = jnp.zeros_like(carry_ref)

    rt = rt_ref[...]
    tt = rt.shape[1]
    e1 = rt[2:3, :].astype(jnp.int32)
    e2 = rt[3:4, :].astype(jnp.int32)
    eid = lax.broadcasted_iota(jnp.int32, (ne_pad, tt), 0)
    m1 = eid == e1
    m2 = eid == e2
    member = jnp.where(m1 | m2, 1.0, 0.0)
    before = jnp.where(lax.broadcasted_iota(jnp.int32, (tt, tt), 0) < lax.broadcasted_iota(jnp.int32, (tt, tt), 1),
                       1.0, 0.0).astype(BF16)
    carry = carry_ref[...]
    rank = _dot(member.astype(BF16), before) + carry[:, :1]
    r1 = jnp.sum(jnp.where(m1, rank, 0.0), axis=0, keepdims=True).astype(jnp.int32)
    r2 = jnp.sum(jnp.where(m2, rank, 0.0), axis=0, keepdims=True).astype(jnp.int32)
    ri_ref[...] = jnp.concatenate([e1, e2, r1, r2, jnp.zeros((ROUTE_ROWS - 4, tt), jnp.int32)], axis=0)
    carry = carry + jnp.sum(member, axis=1, keepdims=True)
    carry_ref[...] = carry

    @pl.when(i == pl.num_programs(0) - 1)
    def _():
        cnt_ref[...] = carry


def _rank(route_t, ne_pad):
    _, n = route_t.shape
    tt = min(512, n)
    return pl.pallas_call(
        functools.partial(_rank_kernel, ne_pad=ne_pad),
        grid=(n // tt,),
        in_specs=[pl.BlockSpec((ROUTE_ROWS, tt), lambda i: (0, i))],
        out_specs=[pl.BlockSpec((ROUTE_ROWS, tt), lambda i: (0, i)),
                   pl.BlockSpec((ne_pad, LANES), lambda i: (0, 0))],
        out_shape=[jax.ShapeDtypeStruct((ROUTE_ROWS, n), jnp.int32),
                   jax.ShapeDtypeStruct((ne_pad, LANES), F32)],
        scratch_shapes=[pltpu.VMEM((ne_pad, LANES), F32)],
        compiler_params=_params(("arbitrary",)),
        name="moe_rank",
    )(route_t)


def _store_row_tiles(ref2, x2d):
    rows, d = x2d.shape
    for j in range(d // LANES):
        ref2[pl.ds(j, rows, stride=d // LANES), :] = x2d[:, j * LANES:(j + 1) * LANES]


def _load_row_tiles(ref2):
    rows = ref2.shape[0] // ROW_TILE
    return jnp.concatenate([ref2[pl.ds(j, rows, stride=ROW_TILE), :] for j in range(ROW_TILE)], axis=1)


def _row_copy(src, src_row, dst, dst_row, sem):
    s0 = pl.multiple_of(src_row * ROW_TILE, ROW_TILE)
    d0 = pl.multiple_of(dst_row * ROW_TILE, ROW_TILE)
    return pltpu.make_async_copy(src.at[pl.ds(s0, ROW_TILE)], dst.at[pl.ds(d0, ROW_TILE)], sem)


def _pos_kernel(starts_ref, ri_ref, po_ref, *, ne):
    ri = ri_ref[...]
    e1, e2 = ri[0:1, :], ri[1:2, :]
    s1 = jnp.zeros_like(e1)
    s2 = jnp.zeros_like(e2)
    for e in range(ne):
        st = starts_ref[e]
        s1 = jnp.where(e1 == e, st, s1)
        s2 = jnp.where(e2 == e, st, s2)
    po_ref[...] = jnp.concatenate([s1 + ri[2:3, :], s2 + ri[3:4, :],
                                   jnp.zeros((ROUTE_ROWS - 2, ri.shape[1]), jnp.int32)], axis=0)


def _positions(starts, route_i, ne):
    _, n = route_i.shape
    tt = min(2048, n)
    return pl.pallas_call(
        functools.partial(_pos_kernel, ne=ne),
        grid_spec=pltpu.PrefetchScalarGridSpec(
            num_scalar_prefetch=1,
            grid=(n // tt,),
            in_specs=[pl.BlockSpec((ROUTE_ROWS, tt), lambda i, st: (0, i))],
            out_specs=pl.BlockSpec((ROUTE_ROWS, tt), lambda i, st: (0, i)),
        ),
        out_shape=jax.ShapeDtypeStruct((ROUTE_ROWS, n), jnp.int32),
        compiler_params=_params(("arbitrary",)),
        name="moe_positions",
    )(starts, route_i)


def _scatter_kernel(pos_ref, hn_ref, xs_in_ref, xs_ref, sem):
    del xs_in_ref
    tt = pos_ref.shape[1]

    def issue(t, carry):
        _row_copy(hn_ref, t, xs_ref, pos_ref[0, t], sem).start()
        _row_copy(hn_ref, t, xs_ref, pos_ref[1, t], sem).start()
        return carry

    lax.fori_loop(0, tt, issue, 0, unroll=8)

    def drain(t, carry):
        _row_copy(hn_ref, 0, xs_ref, 0, sem).wait()
        _row_copy(hn_ref, 0, xs_ref, 0, sem).wait()
        return carry

    lax.fori_loop(0, tt, drain, 0, unroll=8)


def _scatter(pos, hn_rt, n_rows):
    n = hn_rt.shape[0] // ROW_TILE
    tt = min(512, n)
    xs0 = jnp.zeros((n_rows * ROW_TILE, LANES), hn_rt.dtype)
    return pl.pallas_call(
        _scatter_kernel,
        grid=(n // tt,),
        in_specs=[
            pl.BlockSpec((ROUTE_ROWS, tt), lambda i: (0, i), memory_space=pltpu.SMEM),
            pl.BlockSpec((tt * ROW_TILE, LANES), lambda i: (i, 0)),
            pl.BlockSpec(memory_space=pl.ANY),
        ],
        out_specs=pl.BlockSpec(memory_space=pl.ANY),
        out_shape=jax.ShapeDtypeStruct((n_rows * ROW_TILE, LANES), hn_rt.dtype),
        scratch_shapes=[pltpu.SemaphoreType.DMA(())],
        input_output_aliases={2: 0},
        compiler_params=_params(("arbitrary",)),
        name="moe_scatter",
    )(pos, hn_rt, xs0)


def _expert_kernel(te_ref, nu_ref, xs_ref, w1_ref, w3_ref, w2_ref, o_ref):
    i = pl.program_id(0)

    @pl.when(i < nu_ref[0])
    def _():
        x = _load_row_tiles(xs_ref).astype(BF16)
        up = _dot(x, w1_ref[...].astype(BF16))
        lin = _dot(x, w3_ref[...].astype(BF16))
        z = (up * _sigmoid(up)) * lin
        _store_row_tiles(o_ref, _dot(z.astype(BF16), w2_ref[...].astype(BF16)))

    @pl.when(i >= nu_ref[0])
    def _():
        o_ref[...] = jnp.zeros_like(o_ref)


def _experts(tile_expert, n_used, xs_rt, w1, w3, w2, tm):
    n_rows = xs_rt.shape[0] // ROW_TILE
    _, d, f = w1.shape
    last = lambda i, nu: jnp.minimum(i, jnp.maximum(nu[0] - 1, 0))
    return pl.pallas_call(
        _expert_kernel,
        grid_spec=pltpu.PrefetchScalarGridSpec(
            num_scalar_prefetch=2,
            grid=(n_rows // tm,),
            in_specs=[
                pl.BlockSpec((tm * ROW_TILE, LANES), lambda i, te, nu: (last(i, nu), 0)),
                pl.BlockSpec((None, d, f), lambda i, te, nu: (te[i], 0, 0)),
                pl.BlockSpec((None, d, f), lambda i, te, nu: (te[i], 0, 0)),
                pl.BlockSpec((None, f, d), lambda i, te, nu: (te[i], 0, 0)),
            ],
            out_specs=pl.BlockSpec((tm * ROW_TILE, LANES), lambda i, te, nu: (i, 0)),
        ),
        out_shape=jax.ShapeDtypeStruct((n_rows * ROW_TILE, LANES), F32),
        compiler_params=_params(("arbitrary",)),
        name="moe_experts",
    )(tile_expert, n_used, xs_rt, w1, w3, w2)


def _combine_kernel(pos_ref, pos_next_ref, rf_ref, x2_ref, gf_ref, o_ref, y_ref, buf1, buf2, sems):
    i = pl.program_id(0)
    tt = pos_ref.shape[1]
    slot = i % 2

    def gather(p_ref, s):
        def issue(t, carry):
            _row_copy(o_ref, p_ref[0, t], buf1.at[s], t, sems.at[s]).start()
            _row_copy(o_ref, p_ref[1, t], buf2.at[s], t, sems.at[s]).start()
            return carry

        lax.fori_loop(0, tt, issue, 0, unroll=8)

    @pl.when(i == 0)
    def _():
        gather(pos_ref, 0)

    @pl.when(i + 1 < pl.num_programs(0))
    def _():
        gather(pos_next_ref, 1 - slot)

    def drain(t, carry):
        _row_copy(o_ref, 0, buf1.at[slot], 0, sems.at[slot]).wait()
        _row_copy(o_ref, 0, buf2.at[slot], 0, sems.at[slot]).wait()
        return carry

    lax.fori_loop(0, tt, drain, 0, unroll=8)
    rf = rf_ref[...]
    y = (x2_ref[...] + rf[:, 0:1] * _load_row_tiles(buf1.at[slot])
         + rf[:, 1:2] * _load_row_tiles(buf2.at[slot]))
    y_ref[...] = _rmsnorm(y, gf_ref[...], NORM_EPS)


def _combine(pos, route_f, x2, gain_final, o_rt):
    n, d = x2.shape
    tt = min(256, n)
    steps = n // tt
    return pl.pallas_call(
        _combine_kernel,
        grid=(steps,),
        in_specs=[
            pl.BlockSpec((ROUTE_ROWS, tt), lambda i: (0, i), memory_space=pltpu.SMEM),
            pl.BlockSpec((ROUTE_ROWS, tt), lambda i: (0, jnp.minimum(i + 1, steps - 1)), memory_space=pltpu.SMEM),
            pl.BlockSpec((tt, LANES), lambda i: (i, 0)),
            pl.BlockSpec((tt, d), lambda i: (i, 0)),
            pl.BlockSpec((1, d), lambda i: (0, 0)),
            pl.BlockSpec(memory_space=pl.ANY),
        ],
        out_specs=pl.BlockSpec((tt, d), lambda i: (i, 0)),
        out_shape=jax.ShapeDtypeStruct((n, d), F32),
        scratch_shapes=[pltpu.VMEM((2, tt * ROW_TILE, LANES), F32), pltpu.VMEM((2, tt * ROW_TILE, LANES), F32),
                        pltpu.SemaphoreType.DMA((2,))],
        compiler_params=_params(("arbitrary",)),
        name="moe_combine",
    )(pos, pos, route_f, x2, gain_final, o_rt)


def _moe(hn_rt, route_f, route_t, w1, w3, w2, x2, gain_final):
    n = hn_rt.shape[0] // ROW_TILE
    ne = w1.shape[0]
    ne_pad = -(-ne // 8) * 8
    tm = EXPERT_TILE if 2 * n >= 2 * ne * EXPERT_TILE else EXPERT_TILE // 4
    n_tiles = -(-2 * n // tm) + ne
    route_i, counts = _rank(route_t, ne_pad)
    counts = counts[:ne, 0].astype(jnp.int32)
    tiles = (counts + tm - 1) // tm
    ends = jnp.cumsum(tiles)
    starts = (ends - tiles) * tm
    n_used = ends[-1:]
    tile_expert = jnp.minimum(jnp.sum(jnp.arange(n_tiles)[:, None] >= ends[None, :], axis=1), ne - 1)
    pos = _positions(starts.astype(jnp.int32), route_i, ne)
    xs_rt = _scatter(pos, hn_rt, n_tiles * tm)
    o_rt = _experts(tile_expert.astype(jnp.int32), n_used.astype(jnp.int32), xs_rt, w1, w3, w2, tm)
    return _combine(pos, route_f, x2, gain_final, o_rt)


def _run_path(x, prev_row, s0, cache, wts, dims):
    b, t, d = x.shape
    x2d = x.reshape(b * t, d)
    q_scale = dims["da"] ** -0.5 * math.log2(math.e)
    gates = _gate_proj(x2d, wts["norm_mix"], wts["w_gates"])

    if cache is None:
        qkv, kf, vf, qt, vt = _qkv_proj(x2d, wts["norm_mix"], wts["w_qkv"], q_scale, True)
        qkv3 = qkv.reshape(b, t, -1)
        tiles = lambda a: a.reshape(b, t // a.shape[2], a.shape[1], a.shape[2])
        o_a = _attn_prompt(qkv3, tiles(qt), tiles(vt), wts["lams"], wts["subln_w"].reshape(-1, 1),
                           n_heads=dims["ha"], chunk=dims["chunk"], lam_init=dims["lam_init"])
    else:
        qkv, kf, vf = _qkv_proj(x2d, wts["norm_mix"], wts["w_qkv"], q_scale, False)
        qkv3 = qkv.reshape(b, t, -1)
        o_a = _attn_sample(qkv3, cache[0], cache[1], wts["lams"], wts["subln_w"], n_heads=dims["ha"],
                           lam_init=dims["lam_init"])

    *prep, shift_new = _rwkv_prep(x, wts["norm_mix"], wts["w_pb"], prev_row, wts["mu"], wts["w0"], wts["w2p"],
                                  wts["a0"], wts["a2p"], wts["g2"], wts["k_k"], wts["k_a"], wts["r_k"],
                                  wb=dims["wb"])
    o_b, s_new = _rwkv_chunks(prep, _state_to_blockdiag(s0), wts["lnx_w"], wts["lnx_b"])

    x2, hn, route_f, route_t = _mix(x2d, o_a.reshape(b * t, -1), o_b, gates, wts["w_a_out"], wts["w_b_out"],
                                    wts["w_o"], wts["norm_ffn"], wts["w_route"], wts["b_route"],
                                    n_groups=dims["n_groups"], n_per_group=dims["n_per_group"])
    y = _moe(hn, route_f, route_t, wts["moe_w1"], wts["moe_w3"], wts["moe_w2"], x2, wts["norm_final"])

    ha = dims["ha"]
    return (y.reshape(b, t, d), kf.reshape(1, b, t, ha, -1), vf.reshape(1, b, t, ha, -1),
            shift_new[None], _state_from_blockdiag(s_new)[None])


def kernel(x_prompt, x_sample, cache_attn_k, cache_attn_v, state_rwkv_shift, state_rwkv_wkv, norm_mix, w_in, lambda_q1, lambda_k1, lambda_q2, lambda_k2, subln_w, w_a_out, rwkv_mu, rwkv_w0, rwkv_w2, rwkv_a0, rwkv_a2, rwkv_g2, rwkv_k_k, rwkv_k_a, rwkv_r_k, rwkv_lnx_w, rwkv_lnx_b, w_b_out, w_o, norm_ffn, moe_w_group, moe_b_group, moe_w_router, moe_b_router, moe_w1, moe_w3, moe_w2, norm_final):
    assert w_in.shape[0] == 1, "single-layer trunk"
    l = 0
    d = x_prompt.shape[-1]
    ha, dva = cache_attn_k.shape[3], cache_attn_v.shape[4]
    wa = ha * dva
    hb, db = state_rwkv_wkv.shape[2], state_rwkv_wkv.shape[3]
    wb = hb * db
    lora_w, lora_a, lora_g = rwkv_w2.shape[1], rwkv_a2.shape[1], rwkv_g2.shape[1]
    rwkv_cols = 3 * wb + lora_w + lora_a + lora_g
    n_groups = moe_w_group.shape[-1]
    n_per_group = moe_w_router.shape[-1] // n_groups
    ne = n_groups * n_per_group
    assert db == HEAD_B and lora_w + lora_a == LANES and lora_g == LANES and dva == LANES
    assert ne + n_groups <= LANES

    row = lambda v: v.reshape(1, -1).astype(F32)
    w_l = w_in[l]
    zeros_w = jnp.zeros((lora_a, wb), F32)
    zeros_a = jnp.zeros((lora_w, wb), F32)
    w_route = jnp.zeros((d, LANES), F32)
    w_route = w_route.at[:, :ne].set(moe_w_router[l]).at[:, ne:ne + n_groups].set(moe_w_group[l])
    b_route = jnp.zeros((1, LANES), F32)
    b_route = b_route.at[0, :ne].set(moe_b_router[l]).at[0, ne:ne + n_groups].set(moe_b_group[l])
    f = moe_w1.shape[-1]
    wts = dict(
        norm_mix=row(norm_mix[l]),
        w_qkv=w_l[:, :3 * wa].astype(BF16),
        w_pb=w_l[:, 3 * wa:3 * wa + rwkv_cols].astype(BF16),
        w_gates=w_l[:, 3 * wa + rwkv_cols:].astype(BF16),
        lams=[row(lambda_q1[l]), row(lambda_k1[l]), row(lambda_q2[l]), row(lambda_k2[l])],
        subln_w=row(subln_w[l]),
        mu=row(rwkv_mu[l]), w0=row(rwkv_w0[l]), a0=row(rwkv_a0[l]),
        w2p=jnp.concatenate([rwkv_w2[l], zeros_w], axis=0),
        a2p=jnp.concatenate([zeros_a, rwkv_a2[l]], axis=0),
        g2=rwkv_g2[l].astype(F32),
        k_k=row(rwkv_k_k[l]), k_a=row(rwkv_k_a[l]), r_k=row(rwkv_r_k[l]),
        lnx_w=row(rwkv_lnx_w[l]), lnx_b=row(rwkv_lnx_b[l]),
        w_a_out=w_a_out[l].astype(BF16), w_b_out=w_b_out[l].astype(BF16), w_o=w_o[l].astype(BF16),
        norm_ffn=row(norm_ffn[l]), w_route=w_route, b_route=b_route,
        moe_w1=moe_w1[l].reshape(ne, d, f), moe_w3=moe_w3[l].reshape(ne, d, f), moe_w2=moe_w2[l].reshape(ne, f, d),
        norm_final=row(norm_final),
    )
    dims = dict(ha=ha, da=dva // 2, wb=wb, chunk=CHUNK_B, lam_init=0.8 - 0.6 * math.exp(-0.3 * l),
                n_groups=n_groups, n_per_group=n_per_group)

    bp = x_prompt.shape[0]
    yp, kp, vp, shp, wkp = _run_path(
        x_prompt.astype(F32), jnp.zeros((bp, 1, rwkv_cols), F32), jnp.zeros((bp, hb, db, db), F32), None, wts, dims)

    bs, past = cache_attn_k.shape[1], cache_attn_k.shape[2]
    cache = (cache_attn_k[l].reshape(bs, past * ha, dva), cache_attn_v[l].reshape(bs, past * ha, dva))
    ys, ks_, vs_, shs, wks = _run_path(
        x_sample.astype(F32), state_rwkv_shift[l].astype(F32), state_rwkv_wkv[l].astype(F32), cache, wts, dims)

    return (yp, ys, kp, vp, shp, wkp, ks_, vs_, shs, wks)
```

```python
import functools
import math

import jax
import jax.numpy as jnp
from jax import lax
from jax.experimental import pallas as pl
from jax.experimental.pallas import tpu as pltpu

F32 = jnp.float32
BF16 = jnp.bfloat16

LANES = 128
HEAD_B = 64
CHUNK_B = 64
CHUNKS_PER_STEP = 4
ATTN_TILE = 512
ROW_TILE = 8
ROUTE_ROWS = 8
EXPERT_TILE = 512
NORM_EPS = 1e-6
DIFF_EPS = 1e-5
GN_EPS = 64e-5
NEG_INF = -1e30
VMEM_LIMIT = 56 * 1024 * 1024

NN = (((1,), (0,)), ((), ()))
NT = (((1,), (1,)), ((), ()))
TN = (((0,), (0,)), ((), ()))


def _dot(a, b, dims=NN):
    return lax.dot_general(a, b, dims, preferred_element_type=F32)


def _split2(a):
    hi = a.astype(BF16)
    lo = (a - hi.astype(F32)).astype(BF16)
    return hi, lo


def _split3(a):
    hi = a.astype(BF16)
    r = a - hi.astype(F32)
    mid = r.astype(BF16)
    lo = (r - mid.astype(F32)).astype(BF16)
    return hi, mid, lo


def _mm(a, b, passes=1, dims=NN):
    if passes == 1:
        return _dot(a.astype(BF16), b.astype(BF16), dims)
    a_hi, a_lo = _split2(a)
    if passes == 2:
        b_hi = b.astype(BF16)
        return _dot(a_hi, b_hi, dims) + _dot(a_lo, b_hi, dims)
    b_hi, b_lo = _split2(b)
    return _dot(a_hi, b_hi, dims) + (_dot(a_hi, b_lo, dims) + _dot(a_lo, b_hi, dims))


def _mm_exact_rhs2(a, b_bf16, dims=NN):
    hi, lo = _split2(a)
    return _dot(hi, b_bf16, dims) + _dot(lo, b_bf16, dims)


def _rmsnorm(x, g, eps):
    return x * lax.rsqrt(jnp.mean(x * x, axis=-1, keepdims=True) + eps) * g


def _sigmoid(x):
    return 1.0 / (1.0 + jnp.exp(-x))


def _params(sem):
    return pltpu.CompilerParams(dimension_semantics=sem, vmem_limit_bytes=VMEM_LIMIT)


def _qkv_kernel(x_ref, g_ref, w_ref, qkv_ref, kf_ref, vf_ref, *rest, q_scale, transposed):
    if transposed:
        qt_ref, vt_ref, h_ref = rest
    else:
        (h_ref,) = rest
    j = pl.program_id(1)

    @pl.when(j == 0)
    def _():
        h_ref[...] = _rmsnorm(x_ref[...], g_ref[...], NORM_EPS).astype(BF16)

    p = _dot(h_ref[...], w_ref[...])

    @pl.when(j == 0)
    def _():
        q = p * q_scale
        qkv_ref[...] = q.astype(BF16)
        if transposed:
            qt_ref[...] = q.T.astype(BF16)

    @pl.when(j == 1)
    def _():
        qkv_ref[...] = p.astype(BF16)
        kf_ref[...] = p

    @pl.when(j == 2)
    def _():
        qkv_ref[...] = p.astype(BF16)
        vf_ref[...] = p
        if transposed:
            vt_ref[...] = p.T.astype(BF16)


def _qkv_proj(x2d, gain, w_qkv, q_scale, transposed):
    n, d = x2d.shape
    wa = w_qkv.shape[1] // 3
    tm = min(ATTN_TILE, n)
    out_specs = [
        pl.BlockSpec((tm, wa), lambda i, j: (i, j)),
        pl.BlockSpec((tm, wa), lambda i, j: (i, 0)),
        pl.BlockSpec((tm, wa), lambda i, j: (i, 0)),
    ]
    out_shape = [
        jax.ShapeDtypeStruct((n, 3 * wa), BF16),
        jax.ShapeDtypeStruct((n, wa), F32),
        jax.ShapeDtypeStruct((n, wa), F32),
    ]
    if transposed:
        out_specs += [pl.BlockSpec((None, wa, tm), lambda i, j: (i, 0, 0))] * 2
        out_shape += [jax.ShapeDtypeStruct((n // tm, wa, tm), BF16)] * 2
    return pl.pallas_call(
        functools.partial(_qkv_kernel, q_scale=q_scale, transposed=transposed),
        grid=(n // tm, 3),
        in_specs=[
            pl.BlockSpec((tm, d), lambda i, j: (i, 0)),
            pl.BlockSpec((1, d), lambda i, j: (0, 0)),
            pl.BlockSpec((d, wa), lambda i, j: (0, j)),
        ],
        out_specs=out_specs,
        out_shape=out_shape,
        scratch_shapes=[pltpu.VMEM((tm, d), BF16)],
        compiler_params=_params(("parallel", "arbitrary")),
        name="qkv_proj",
    )(x2d, gain, w_qkv)


def _gate_kernel(x_ref, g_ref, w_ref, o_ref):
    h = _rmsnorm(x_ref[...], g_ref[...], NORM_EPS).astype(BF16)
    o_ref[...] = _sigmoid(_dot(h, w_ref[...])).astype(o_ref.dtype)


def _gate_proj(x2d, gain, w):
    n, d = x2d.shape
    c = w.shape[1]
    tm = min(512, n)
    return pl.pallas_call(
        _gate_kernel,
        grid=(n // tm,),
        in_specs=[
            pl.BlockSpec((tm, d), lambda i: (i, 0)),
            pl.BlockSpec((1, d), lambda i: (0, 0)),
            pl.BlockSpec((d, c), lambda i: (0, 0)),
        ],
        out_specs=pl.BlockSpec((tm, c), lambda i: (i, 0)),
        out_shape=jax.ShapeDtypeStruct((n, c), BF16),
        compiler_params=_params(("parallel",)),
        name="gate_proj",
    )(x2d, gain, w)


def _lambda(lq1_ref, lk1_ref, lq2_ref, lk2_ref, lam_init):
    s1 = jnp.sum(lq1_ref[...] * lk1_ref[...], axis=-1, keepdims=True)
    s2 = jnp.sum(lq2_ref[...] * lk2_ref[...], axis=-1, keepdims=True)
    return jnp.exp(s1) - jnp.exp(s2) + lam_init


def _split_q(q):
    lane = lax.broadcasted_iota(jnp.int32, (1, q.shape[1]), 1)
    first = lane < (q.shape[1] // 2)
    zero = jnp.zeros_like(q)
    return jnp.where(first, q, zero), jnp.where(first, zero, q)


def _attn_prompt_kernel(lq1_ref, lk1_ref, lq2_ref, lk2_ref, subc_ref, qt_ref, k_ref, vt_ref, o_ref,
                        s0_ref, s1_ref, s2_ref, s3_ref, mt_ref, m_ref, l_ref, acc_ref, *, chunk, lam_init):
    i = pl.program_id(2)
    dv, tq = qt_ref.shape
    tk = vt_ref.shape[2]
    lam = _lambda(lq1_ref, lk1_ref, lq2_ref, lk2_ref, lam_init)
    qt = qt_ref[...]
    feat = lax.broadcasted_iota(jnp.int32, (dv, 1), 0)
    zq = jnp.zeros_like(qt)
    q_maps = (jnp.where(feat < dv // 2, qt, zq), jnp.where(feat < dv // 2, zq, qt))
    acc_ref[...] = jnp.zeros_like(acc_ref)
    rb = 64
    kb = min(256, tk)

    m_ref[...] = jnp.full(m_ref.shape, NEG_INF, F32)
    l_ref[...] = jnp.zeros_like(l_ref)
    buf_a, buf_b = (s0_ref, s1_ref), (s2_ref, s3_ref)

    def scores(j, buf, masked):
        ks = k_ref[pl.ds(pl.multiple_of(j * tk, tk), tk), :]
        for mi in range(2):
            s = _dot(ks, q_maps[mi])
            if masked:
                kc = lax.broadcasted_iota(jnp.int32, (tk, tq), 0) // chunk
                qc = lax.broadcasted_iota(jnp.int32, (tk, tq), 1) // chunk
                s = jnp.where(kc <= qc, s, NEG_INF)
            buf[mi][...] = s
            mt_ref[2 * (buf is buf_b) + mi] = jnp.max(s, axis=0, keepdims=True)

    def softmax_pv(j, buf):
        vt = vt_ref[j]
        m_old = [m_ref[mi] for mi in range(2)]
        m_new = [jnp.maximum(m_old[mi], mt_ref[2 * (buf is buf_b) + mi]) for mi in range(2)]
        alpha = [jnp.exp2(m_old[mi] - m_new[mi]) for mi in range(2)]
        lsum = [jnp.zeros((8, tq), F32) for _ in range(2)]
        pv = [None, None]
        for hf in range(tk // kb):
            for mi in range(2):
                blocks = []
                for r in range(kb // rb):
                    lo = hf * kb + r * rb
                    p = jnp.exp2(buf[mi][lo:lo + rb, :] - m_new[mi])
                    lsum[mi] = lsum[mi] + jnp.sum(p.reshape(rb // 8, 8, tq), axis=0)
                    blocks.append(p.astype(BF16))
                part = _dot(vt[:, hf * kb:(hf + 1) * kb], jnp.concatenate(blocks, axis=0))
                pv[mi] = part if pv[mi] is None else pv[mi] + part
        for mi in range(2):
            acc_ref[mi] = acc_ref[mi] * alpha[mi] + pv[mi]
            m_ref[mi] = m_new[mi]
            l_ref[mi] = alpha[mi] * l_ref[mi] + lsum[mi]

    def stage_even(j, next_masked):
        scores(j + 1, buf_b, next_masked)
        softmax_pv(j, buf_a)

    def stage_odd(j, next_masked):
        scores(j + 1, buf_a, next_masked)
        softmax_pv(j, buf_b)

    @pl.when(i == 0)
    def _():
        scores(0, buf_a, True)
        softmax_pv(0, buf_a)

    @pl.when(i > 0)
    def _():
        scores(0, buf_a, False)

    def pair(jj, carry):
        stage_even(2 * jj, False)
        stage_odd(2 * jj + 1, False)
        return carry

    def quad(jj, carry):
        pair(2 * jj, carry)
        return pair(2 * jj + 1, carry)

    n_pairs = jnp.maximum(i - 1, 0) // 2
    lax.fori_loop(0, n_pairs // 2, quad, 0)
    lax.fori_loop(2 * (n_pairs // 2), n_pairs, pair, 0)

    @pl.when(i % 2 == 1)
    def _():
        stage_even(i - 1, True)
        softmax_pv(i, buf_b)

    @pl.when((i % 2 == 0) & (i > 0))
    def _():
        stage_even(i - 2, False)
        stage_odd(i - 1, True)
        softmax_pv(i, buf_a)

    l1 = jnp.sum(l_ref[0], axis=0, keepdims=True)
    l2 = jnp.sum(l_ref[1], axis=0, keepdims=True)
    ot = acc_ref[0] / l1 - lam * (acc_ref[1] / l2)
    ms = jnp.mean(ot * ot, axis=0, keepdims=True)
    ot = ot * lax.rsqrt(ms + DIFF_EPS) * subc_ref[...] * (1.0 - lam_init)
    o_ref[...] = ot.T.astype(o_ref.dtype)


def _attn_prompt(qkv, qt, vt, lams, sub_col, *, n_heads, chunk, lam_init):
    b, t, c3 = qkv.shape
    wa = c3 // 3
    dv = wa // n_heads
    nk, tile = qt.shape[1], qt.shape[3]
    lam_specs = [pl.BlockSpec((1, lams[0].shape[1]), lambda bb, h, i: (0, 0)) for _ in range(4)]
    return pl.pallas_call(
        functools.partial(_attn_prompt_kernel, chunk=chunk, lam_init=lam_init),
        grid=(b, n_heads, nk),
        in_specs=lam_specs + [
            pl.BlockSpec((dv, 1), lambda bb, h, i: (0, 0)),
            pl.BlockSpec((None, None, dv, tile), lambda bb, h, i: (bb, i, h, 0)),
            pl.BlockSpec((None, t, dv), lambda bb, h, i: (bb, 0, n_heads + h)),
            pl.BlockSpec((None, nk, dv, tile), lambda bb, h, i: (bb, 0, h, 0)),
        ],
        out_specs=pl.BlockSpec((None, tile, dv), lambda bb, h, i: (bb, i, h)),
        out_shape=jax.ShapeDtypeStruct((b, t, wa), BF16),
        scratch_shapes=[pltpu.VMEM((tile, tile), F32)] * 4
        + [pltpu.VMEM((4, 1, tile), F32),
           pltpu.VMEM((2, 1, tile), F32),
           pltpu.VMEM((2, 8, tile), F32),
           pltpu.VMEM((2, dv, tile), F32)],
        compiler_params=_params(("parallel", "parallel", "arbitrary")),
        name="diff_attn_prompt",
    )(*lams, sub_col, qt, qkv, vt)


def _attn_sample_kernel(lq1_ref, lk1_ref, lq2_ref, lk2_ref, sub_ref, q_ref, kn_ref, vn_ref, ck_ref, cv_ref,
                        o_ref, *, n_heads, lam_init):
    lam = _lambda(lq1_ref, lk1_ref, lq2_ref, lk2_ref, lam_init)
    tq = q_ref.shape[0]
    dv = q_ref.shape[1] // n_heads
    past = ck_ref.shape[0] // n_heads
    ones_c = jnp.ones((past, dv), BF16)
    ones_n = jnp.ones((tq, dv), BF16)
    for h in range(n_heads):
        cols = slice(h * dv, (h + 1) * dv)
        q1, q2 = _split_q(q_ref[:, cols].astype(F32))
        w = jnp.concatenate([q1, q2], axis=0).T.astype(BF16)
        ck = ck_ref[pl.ds(h, past, stride=n_heads), :].astype(BF16)
        cv = cv_ref[pl.ds(h, past, stride=n_heads), :].astype(BF16)
        s_c = _dot(ck, w)
        s_n = _dot(kn_ref[:, cols], w)
        m = jnp.maximum(jnp.max(s_c, axis=0, keepdims=True), jnp.max(s_n, axis=0, keepdims=True))
        p_c = jnp.exp2(s_c - m).astype(BF16)
        p_n = jnp.exp2(s_n - m).astype(BF16)
        acc = (_dot(p_c, jnp.concatenate([cv, ones_c], axis=1), TN)
               + _dot(p_n, jnp.concatenate([vn_ref[:, cols], ones_n], axis=1), TN))
        o1 = acc[:tq, :dv] / acc[:tq, dv:]
        o2 = acc[tq:, :dv] / acc[tq:, dv:]
        o = o1 - lam * o2
        o_ref[:, cols] = (_rmsnorm(o, sub_ref[...], DIFF_EPS) * (1.0 - lam_init)).astype(o_ref.dtype)


def _attn_sample(qkv, cache_k, cache_v, lams, sub_w, *, n_heads, lam_init):
    b, t, c3 = qkv.shape
    wa = c3 // 3
    dv = wa // n_heads
    rows = cache_k.shape[1]
    lam_specs = [pl.BlockSpec((1, lams[0].shape[1]), lambda bb: (0, 0)) for _ in range(4)]
    return pl.pallas_call(
        functools.partial(_attn_sample_kernel, n_heads=n_heads, lam_init=lam_init),
        grid=(b,),
        in_specs=lam_specs + [
            pl.BlockSpec((1, dv), lambda bb: (0, 0)),
            pl.BlockSpec((None, t, wa), lambda bb: (bb, 0, 0)),
            pl.BlockSpec((None, t, wa), lambda bb: (bb, 0, 1)),
            pl.BlockSpec((None, t, wa), lambda bb: (bb, 0, 2)),
            pl.BlockSpec((None, rows, dv), lambda bb: (bb, 0, 0)),
            pl.BlockSpec((None, rows, dv), lambda bb: (bb, 0, 0)),
        ],
        out_specs=pl.BlockSpec((None, t, wa), lambda bb: (bb, 0, 0)),
        out_shape=jax.ShapeDtypeStruct((b, t, wa), BF16),
        compiler_params=_params(("parallel",)),
        name="diff_attn_sample",
    )(*lams, sub_w, qkv, qkv, qkv, cache_k, cache_v)


def _seg_ones(n=LANES, seg=HEAD_B):
    r = lax.broadcasted_iota(jnp.int32, (n, n), 0) // seg
    c = lax.broadcasted_iota(jnp.int32, (n, n), 1) // seg
    return jnp.where(r == c, 1.0, 0.0).astype(BF16)


def _rwkv_prep_kernel(x_ref, gn_ref, wpb_ref, prev_ref, mu_ref, w0_ref, w2p_ref, a0_ref, a2p_ref, g2_ref, kk_ref,
                      ka_ref, rk_ref, r_o, k_o, v_o, ld_o, na_o, b_o, g_o, bon_o, shift_o, carry_ref, *, wb):
    i = pl.program_id(1)

    @pl.when(i == 0)
    def _():
        carry_ref[...] = prev_ref[...]

    h = _rmsnorm(x_ref[...], gn_ref[...], NORM_EPS).astype(BF16)
    pb = _dot(h, wpb_ref[...])
    shift_o[...] = pb[pb.shape[0] - 1:, :]
    tm = pb.shape[0]
    row = lax.broadcasted_iota(jnp.int32, (tm, 1), 0)
    prev = jnp.where(row == 0, carry_ref[...], pltpu.roll(pb, 1, axis=0))
    carry_ref[...] = pb[tm - 1:tm, :]
    xs = pb + (prev - pb) * mu_ref[...]

    n_slab = wb // LANES
    x_wa = xs[:, 3 * wb:3 * wb + LANES]
    x_g = xs[:, 3 * wb + LANES:3 * wb + 2 * LANES]
    lw = _mm(jnp.tanh(x_wa), w2p_ref[...], 2)
    la = _mm(x_wa, a2p_ref[...])
    g = _mm(_sigmoid(x_g), g2_ref[...])
    seg = _seg_ones(2 * LANES)

    for s in range(n_slab):
        sl = slice(s * LANES, (s + 1) * LANES)
        r = xs[:, s * LANES:(s + 1) * LANES]
        k = xs[:, wb + s * LANES:wb + (s + 1) * LANES]
        v = xs[:, 2 * wb + s * LANES:2 * wb + (s + 1) * LANES]
        z = -(w0_ref[:, sl] + lw[:, sl])
        softplus = jnp.maximum(z, 0.0) + jnp.log1p(jnp.exp(-jnp.abs(z)))
        w_log = -softplus - 0.5
        a = _sigmoid(a0_ref[:, sl] + la[:, sl])
        kk = k * kk_ref[:, sl]
        k2 = k * (1.0 + (a - 1.0) * ka_ref[:, sl])
        sums = _mm_exact_rhs2(jnp.concatenate([kk * kk, r * k2 * rk_ref[:, sl]], axis=1), seg)
        kk = kk / jnp.maximum(jnp.sqrt(sums[:, :LANES]), 1e-12)
        bonus = sums[:, LANES:] * v
        r_o[s] = r.astype(r_o.dtype)
        k_o[s] = k2.astype(k_o.dtype)
        v_o[s] = v.astype(v_o.dtype)
        ld_o[s] = -jnp.exp(w_log)
        na_o[s] = (-kk).astype(na_o.dtype)
        b_o[s] = (kk * a).astype(b_o.dtype)
        g_o[s] = g[:, sl].astype(g_o.dtype)
        bon_o[s] = bonus.astype(bon_o.dtype)


def _rwkv_prep(x, gain, w_pb, prev_row, mu, w0, w2p, a0, a2p, g2, k_k, k_a, r_k, *, wb):
    b, t, d = x.shape
    cols = w_pb.shape[1]
    tm = min(256, t)
    n_slab = wb // LANES
    vec = lambda n: pl.BlockSpec((1, n), lambda bb, i: (0, 0))
    full = lambda a: pl.BlockSpec(a.shape, lambda bb, i: (0, 0))
    out_spec = pl.BlockSpec((None, n_slab, tm, LANES), lambda bb, i: (bb, 0, i, 0))
    sds = lambda dt: jax.ShapeDtypeStruct((b, n_slab, t, LANES), dt)
    out_dtypes = [BF16, BF16, BF16, F32, BF16, BF16, BF16, BF16]
    return pl.pallas_call(
        functools.partial(_rwkv_prep_kernel, wb=wb),
        grid=(b, t // tm),
        in_specs=[
            pl.BlockSpec((None, tm, d), lambda bb, i: (bb, i, 0)),
            vec(d), full(w_pb),
            pl.BlockSpec((None, 1, cols), lambda bb, i: (bb, 0, 0)),
            vec(cols), vec(wb), full(w2p), vec(wb), full(a2p), full(g2), vec(wb), vec(wb), vec(wb),
        ],
        out_specs=[out_spec] * 8 + [pl.BlockSpec((None, 1, cols), lambda bb, i: (bb, 0, 0))],
        out_shape=[sds(dt) for dt in out_dtypes] + [jax.ShapeDtypeStruct((b, 1, cols), F32)],
        scratch_shapes=[pltpu.VMEM((1, cols), F32)],
        compiler_params=_params(("parallel", "arbitrary")),
        name="rwkv_prep",
    )(x, gain, w_pb, prev_row, mu, w0, w2p, a0, a2p, g2, k_k, k_a, r_k)


def _stack2(x, first):
    xb = x.astype(BF16)
    zero = jnp.zeros_like(xb)
    return jnp.concatenate([jnp.where(first, xb, zero), jnp.where(first, zero, xb)], axis=0)


def _rwkv_chunk_kernel(r_ref, k_ref, v_ref, ld_ref, na_ref, b_ref, g_ref, bon_ref, s0_ref, lnw_ref, lnb_ref,
                       y_ref, sT_ref, st_ref, *, cl):
    c = pl.program_id(1)
    n_chunks = pl.num_programs(1)
    n_slab = r_ref.shape[0]
    n_sub = r_ref.shape[1] // cl

    @pl.when(c == 0)
    def _():
        st_ref[...] = s0_ref[...]

    lane = lax.broadcasted_iota(jnp.int32, (1, LANES), 1)
    first = lane < HEAD_B
    ti = lax.broadcasted_iota(jnp.int32, (cl, cl), 0)
    tj = lax.broadcasted_iota(jnp.int32, (cl, cl), 1)
    tri_incl = jnp.where(ti >= tj, 1.0, 0.0).astype(BF16)
    seg = _seg_ones()
    t_i = lax.broadcasted_iota(jnp.int32, (cl, LANES), 0)
    s_i = lax.broadcasted_iota(jnp.int32, (cl, LANES), 1) % HEAD_B
    strict = t_i > s_i
    incl = t_i >= s_i
    zero = jnp.zeros((cl, LANES), F32)
    eye = jnp.where(t_i == s_i, 1.0, 0.0).astype(F32)
    inv_n = 1.0 / HEAD_B

    items = [(s, q) for q in range(n_sub) for s in range(n_slab)]
    rows = lambda q: slice(q * cl, (q + 1) * cl)

    pre = []
    for s, q in items:
        ld = ld_ref[s, rows(q), :]
        hi, mid, lo = _split3(ld)
        cum = _dot(tri_incl, hi) + (_dot(tri_incl, mid) + _dot(tri_incl, lo))
        pre.append((ld, cum))

    stacks = []
    for (s, q), (ld, cum) in zip(items, pre):
        cum_last = cum[cl - 1:cl, :]
        g_incl = jnp.exp(cum)
        g_excl = jnp.exp(cum - ld)
        g_inv = jnp.exp(-cum)
        g_end = jnp.exp(cum_last - cum)
        k = k_ref[s, rows(q), :]
        bv = b_ref[s, rows(q), :]
        stacks.append(dict(
            g_last=jnp.exp(cum_last),
            la=(na_ref[s, rows(q), :] * g_excl).astype(BF16),
            lr=(r_ref[s, rows(q), :] * g_incl).astype(BF16),
            l2=jnp.concatenate([_stack2(bv * g_inv, first), _stack2(k * g_inv, first)], axis=0),
            lh=jnp.concatenate([_stack2(bv * g_end, first), _stack2(k * g_end, first)], axis=0),
            v2=_stack2(v_ref[s, rows(q), :], first)))

    grams = [_dot(jnp.concatenate([d["la"], d["lr"]], axis=0), d["l2"], NT) for d in stacks]
    for d, gram in zip(stacks, grams):
        d["lar"] = jnp.concatenate([d["la"], d["lr"]], axis=0)
        d["g_akrk"] = jnp.concatenate([jnp.where(strict, gram[:cl, LANES:], zero),
                                       jnp.where(incl, gram[cl:, LANES:], zero)], axis=0).astype(BF16)
        d["g_rb"] = jnp.where(incl, gram[cl:, :LANES], zero).astype(BF16)
    n_lev = int(math.log2(cl))
    pows = [jnp.where(strict, gram[:cl, :LANES], zero) for gram in grams]
    invs = [eye + p for p in pows]
    pows = [_dot(p.astype(BF16), _stack2(p, first)) for p in pows]
    for lev in range(1, n_lev):
        last = lev == n_lev - 1
        lhs = [inv.astype(BF16) if last else jnp.concatenate([inv, p], axis=0).astype(BF16)
               for inv, p in zip(invs, pows)]
        prods = [_dot(a, _stack2(p, first)) for a, p in zip(lhs, pows)]
        invs = [inv + pr[:cl, :] for inv, pr in zip(invs, prods)]
        if not last:
            pows = [pr[cl:, :] for pr in prods]
    invs = [inv.astype(BF16) for inv in invs]
    akrk = [_dot(d["g_akrk"], d["v2"]) for d in stacks]

    sts = [st_ref[s] for s in range(n_slab)]
    for q in range(n_sub):
        base = q * n_slab
        stb = [st.astype(BF16) for st in sts]
        ars = [_dot(stacks[base + s]["lar"], stb[s], NT) + akrk[base + s] for s in range(n_slab)]
        us = [_stack2(_dot(invs[base + s], _stack2(ars[s][:cl, :], first)), first) for s in range(n_slab)]
        ys = [ars[s][cl:, :] + _dot(stacks[base + s]["g_rb"], us[s])
              for s in range(n_slab)]
        sts = [sts[s] * stacks[base + s]["g_last"]
               + _dot(jnp.concatenate([us[s], stacks[base + s]["v2"]], axis=0), stacks[base + s]["lh"], TN)
               for s in range(n_slab)]
        y_all = jnp.concatenate(ys, axis=0)
        d_all = y_all - _mm_exact_rhs2(y_all, seg) * inv_n
        var_all = _mm_exact_rhs2(d_all * d_all, seg) * inv_n
        yn_all = d_all * lax.rsqrt(var_all + GN_EPS)
        for s in range(n_slab):
            sl = slice(s * LANES, (s + 1) * LANES)
            yn = yn_all[s * cl:(s + 1) * cl, :] * lnw_ref[:, sl] + lnb_ref[:, sl]
            y_ref[s, rows(q), :] = ((yn + bon_ref[s, rows(q), :]) * g_ref[s, rows(q), :]).astype(y_ref.dtype)
    for s in range(n_slab):
        st_ref[s] = sts[s]

    @pl.when(c == n_chunks - 1)
    def _():
        sT_ref[...] = st_ref[...]


def _rwkv_chunks(prep, s0_bd, lnx_w, lnx_b):
    b, n_slab, t, _ = prep[0].shape
    cl = min(CHUNK_B, t)
    step = min(CHUNKS_PER_STEP * cl, t)
    wb = n_slab * LANES
    in_spec = pl.BlockSpec((None, n_slab, step, LANES), lambda bb, c: (bb, 0, c, 0))
    st_spec = pl.BlockSpec((None, n_slab, LANES, LANES), lambda bb, c: (bb, 0, 0, 0))
    vec = pl.BlockSpec((1, wb), lambda bb, c: (0, 0))
    return pl.pallas_call(
        functools.partial(_rwkv_chunk_kernel, cl=cl),
        grid=(b, t // step),
        in_specs=[in_spec] * 8 + [st_spec, vec, vec],
        out_specs=[in_spec, st_spec],
        out_shape=[
            jax.ShapeDtypeStruct((b, n_slab, t, LANES), BF16),
            jax.ShapeDtypeStruct((b, n_slab, LANES, LANES), F32),
        ],
        scratch_shapes=[pltpu.VMEM((n_slab, LANES, LANES), F32)],
        compiler_params=_params(("parallel", "arbitrary")),
        name="rwkv_chunks",
    )(*prep, s0_bd, lnx_w, lnx_b)


def _state_to_blockdiag(s):
    b, h, n, _ = s.shape
    s = s.reshape(b, h // 2, 2, n, n)
    z = jnp.zeros_like(s[:, :, 0])
    top = jnp.concatenate([s[:, :, 0], z], axis=-1)
    bot = jnp.concatenate([z, s[:, :, 1]], axis=-1)
    return jnp.concatenate([top, bot], axis=-2)


def _state_from_blockdiag(sbd):
    b, hs, n2, _ = sbd.shape
    n = n2 // 2
    return jnp.stack([sbd[:, :, :n, :n], sbd[:, :, n:, n:]], axis=2).reshape(b, 2 * hs, n, n)


def _route(logits, n_groups, n_per_group):
    ne = n_groups * n_per_group
    lane = lax.broadcasted_iota(jnp.int32, logits.shape, 1)
    big = jnp.int32(1 << 30)
    is_grp = (lane >= ne) & (lane < ne + n_groups)
    lg = jnp.where(is_grp, logits, NEG_INF)
    mg = jnp.max(lg, axis=-1, keepdims=True)
    gi = jnp.min(jnp.where(is_grp & (lg == mg), lane, big), axis=-1, keepdims=True) - ne
    eg = jnp.where(is_grp, jnp.exp(lg - mg), 0.0)
    p_sel = 1.0 / jnp.sum(eg, axis=-1, keepdims=True)
    in_grp = (lane < ne) & ((lane // n_per_group) == gi)
    le = jnp.where(in_grp, logits, NEG_INF)
    me = jnp.max(le, axis=-1, keepdims=True)
    ee = jnp.where(in_grp, jnp.exp(le - me), 0.0)
    pe = ee / jnp.sum(ee, axis=-1, keepdims=True)
    v1 = jnp.max(jnp.where(in_grp, pe, -1.0), axis=-1, keepdims=True)
    i1 = jnp.min(jnp.where(in_grp & (pe == v1), lane, big), axis=-1, keepdims=True)
    rest = in_grp & (lane != i1)
    v2 = jnp.max(jnp.where(rest, pe, -1.0), axis=-1, keepdims=True)
    i2 = jnp.min(jnp.where(rest & (pe == v2), lane, big), axis=-1, keepdims=True)
    tot = v1 + v2
    w1 = (v1 / tot) * p_sel
    w2 = (v2 / tot) * p_sel
    return (jnp.where(lane == 0, w1, 0.0) + jnp.where(lane == 1, w2, 0.0)
            + jnp.where(lane == 2, i1.astype(F32), 0.0) + jnp.where(lane == 3, i2.astype(F32), 0.0))


def _mix_kernel(x_ref, oa_ref, ob_ref, gt_ref, wa_ref, wb_ref, wo_ref, gn_ref, wr_ref, br_ref,
                x2_ref, hn_ref, route_ref, route_t_ref, *, n_groups, n_per_group):
    d = x_ref.shape[1]
    nb, n_slab, t_blk, _ = ob_ref.shape
    y_a = _dot(oa_ref[...], wa_ref[...])
    ob = jnp.concatenate([ob_ref[:, s].reshape(nb * t_blk, LANES) for s in range(n_slab)], axis=1)
    y_b = _dot(ob, wb_ref[...])
    merged = gt_ref[:, :d] * y_a + gt_ref[:, d:] * y_b
    x2 = x_ref[...] + _dot(merged.astype(BF16), wo_ref[...])
    x2_ref[...] = x2
    hn = _rmsnorm(x2, gn_ref[...], NORM_EPS)
    _store_row_tiles(hn_ref, hn)
    logits = _mm(hn, wr_ref[...], 3) + br_ref[...]
    route = _route(logits, n_groups, n_per_group)
    route_ref[...] = route
    route_t_ref[...] = route.T[:ROUTE_ROWS, :]


def _mix(x2d, o_a, o_b, gates, w_a, w_b, w_o, gain_ffn, w_route, b_route, *, n_groups, n_per_group):
    b, n_slab, t, _ = o_b.shape
    n, d = x2d.shape
    if t >= 512:
        nb, t_blk = 1, 512
    else:
        nb, t_blk = min(b, 512 // t), t
    tm = nb * t_blk
    nt = t // t_blk
    full = lambda a: pl.BlockSpec(a.shape, lambda i: (0,) * a.ndim)
    return pl.pallas_call(
        functools.partial(_mix_kernel, n_groups=n_groups, n_per_group=n_per_group),
        grid=(n // tm,),
        in_specs=[
            pl.BlockSpec((tm, d), lambda i: (i, 0)),
            pl.BlockSpec((tm, o_a.shape[1]), lambda i: (i, 0)),
            pl.BlockSpec((nb, n_slab, t_blk, LANES), lambda i: (i // nt, 0, i % nt, 0)),
            pl.BlockSpec((tm, 2 * d), lambda i: (i, 0)),
            full(w_a), full(w_b), full(w_o), full(gain_ffn), full(w_route), full(b_route),
        ],
        out_specs=[
            pl.BlockSpec((tm, d), lambda i: (i, 0)),
            pl.BlockSpec((tm * ROW_TILE, LANES), lambda i: (i, 0)),
            pl.BlockSpec((tm, LANES), lambda i: (i, 0)),
            pl.BlockSpec((ROUTE_ROWS, tm), lambda i: (0, i)),
        ],
        out_shape=[
            jax.ShapeDtypeStruct((n, d), F32),
            jax.ShapeDtypeStruct((n * ROW_TILE, LANES), F32),
            jax.ShapeDtypeStruct((n, LANES), F32),
            jax.ShapeDtypeStruct((ROUTE_ROWS, n), F32),
        ],
        compiler_params=_params(("parallel",)),
        name="mix_route",
    )(x2d, o_a, o_b, gates, w_a, w_b, w_o, gain_ffn, w_route, b_route)


def _rank_kernel(rt_ref, ri_ref, cnt_ref, carry_ref, *, ne_pad):
    i = pl.program_id(0)

    @pl.when(i == 0)
    def _():
        carry_ref[...] = jnp.zeros_like(carry_ref)

    rt = rt_ref[...]
    tt = rt.shape[1]
    e1 = rt[2:3, :].astype(jnp.int32)
    e2 = rt[3:4, :].astype(jnp.int32)
    eid = lax.broadcasted_iota(jnp.int32, (ne_pad, tt), 0)
    m1 = eid == e1
    m2 = eid == e2
    member = jnp.where(m1 | m2, 1.0, 0.0)
    before = jnp.where(lax.broadcasted_iota(jnp.int32, (tt, tt), 0) < lax.broadcasted_iota(jnp.int32, (tt, tt), 1),
                       1.0, 0.0).astype(BF16)
    carry = carry_ref[...]
    rank = _dot(member.astype(BF16), before) + carry[:, :1]
    r1 = jnp.sum(jnp.where(m1, rank, 0.0), axis=0, keepdims=True).astype(jnp.int32)
    r2 = jnp.sum(jnp.where(m2, rank, 0.0), axis=0, keepdims=True).astype(jnp.int32)
    ri_ref[...] = jnp.concatenate([e1, e2, r1, r2, jnp.zeros((ROUTE_ROWS - 4, tt), jnp.int32)], axis=0)
    carry = carry + jnp.sum(member, axis=1, keepdims=True)
    carry_ref[...] = carry

    @pl.when(i == pl.num_programs(0) - 1)
    def _():
        cnt_ref[...] = carry


def _rank(route_t, ne_pad):
    _, n = route_t.shape
    tt = min(512, n)
    return pl.pallas_call(
        functools.partial(_rank_kernel, ne_pad=ne_pad),
        grid=(n // tt,),
        in_specs=[pl.BlockSpec((ROUTE_ROWS, tt), lambda i: (0, i))],
        out_specs=[pl.BlockSpec((ROUTE_ROWS, tt), lambda i: (0, i)),
                   pl.BlockSpec((ne_pad, LANES), lambda i: (0, 0))],
        out_shape=[jax.ShapeDtypeStruct((ROUTE_ROWS, n), jnp.int32),
                   jax.ShapeDtypeStruct((ne_pad, LANES), F32)],
        scratch_shapes=[pltpu.VMEM((ne_pad, LANES), F32)],
        compiler_params=_params(("arbitrary",)),
        name="moe_rank",
    )(route_t)


def _store_row_tiles(ref2, x2d):
    rows, d = x2d.shape
    for j in range(d // LANES):
        ref2[pl.ds(j, rows, stride=d // LANES), :] = x2d[:, j * LANES:(j + 1) * LANES]


def _load_row_tiles(ref2):
    rows = ref2.shape[0] // ROW_TILE
    return jnp.concatenate([ref2[pl.ds(j, rows, stride=ROW_TILE), :] for j in range(ROW_TILE)], axis=1)


def _row_copy(src, src_row, dst, dst_row, sem):
    s0 = pl.multiple_of(src_row * ROW_TILE, ROW_TILE)
    d0 = pl.multiple_of(dst_row * ROW_TILE, ROW_TILE)
    return pltpu.make_async_copy(src.at[pl.ds(s0, ROW_TILE)], dst.at[pl.ds(d0, ROW_TILE)], sem)


def _pos_kernel(starts_ref, ri_ref, po_ref, *, ne):
    ri = ri_ref[...]
    e1, e2 = ri[0:1, :], ri[1:2, :]
    s1 = jnp.zeros_like(e1)
    s2 = jnp.zeros_like(e2)
    for e in range(ne):
        st = starts_ref[e]
        s1 = jnp.where(e1 == e, st, s1)
        s2 = jnp.where(e2 == e, st, s2)
    po_ref[...] = jnp.concatenate([s1 + ri[2:3, :], s2 + ri[3:4, :],
                                   jnp.zeros((ROUTE_ROWS - 2, ri.shape[1]), jnp.int32)], axis=0)


def _positions(starts, route_i, ne):
    _, n = route_i.shape
    tt = min(1024, n)
    assert n % tt == 0
    return pl.pallas_call(
        functools.partial(_pos_kernel, ne=ne),
        grid_spec=pltpu.PrefetchScalarGridSpec(
            num_scalar_prefetch=1,
            grid=(n // tt,),
            in_specs=[pl.BlockSpec((ROUTE_ROWS, tt), lambda i, st: (0, i))],
            out_specs=pl.BlockSpec((ROUTE_ROWS, tt), lambda i, st: (0, i)),
        ),
        out_shape=jax.ShapeDtypeStruct((ROUTE_ROWS, n), jnp.int32),
        compiler_params=_params(("arbitrary",)),
        name="moe_positions",
    )(starts, route_i)


def _scatter_kernel(pos_ref, hn_ref, xs_in_ref, xs_ref, sem):
    del xs_in_ref
    tt = pos_ref.shape[1]

    def issue(t, carry):
        _row_copy(hn_ref, t, xs_ref, pos_ref[0, t], sem).start(priority=0)
        _row_copy(hn_ref, t, xs_ref, pos_ref[1, t], sem).start(priority=1)
        return carry

    lax.fori_loop(0, tt, issue, 0, unroll=8)

    def drain(t, carry):
        _row_copy(hn_ref, 0, xs_ref, 0, sem).wait()
        _row_copy(hn_ref, 0, xs_ref, 0, sem).wait()
        return carry

    lax.fori_loop(0, tt, drain, 0, unroll=8)


def _scatter(pos, hn_rt, n_rows):
    n = hn_rt.shape[0] // ROW_TILE
    tt = min(512, n)
    xs0 = jnp.zeros((n_rows * ROW_TILE, LANES), hn_rt.dtype)
    return pl.pallas_call(
        _scatter_kernel,
        grid=(n // tt,),
        in_specs=[
            pl.BlockSpec((ROUTE_ROWS, tt), lambda i: (0, i), memory_space=pltpu.SMEM),
            pl.BlockSpec((tt * ROW_TILE, LANES), lambda i: (i, 0)),
            pl.BlockSpec(memory_space=pl.ANY),
        ],
        out_specs=pl.BlockSpec(memory_space=pl.ANY),
        out_shape=jax.ShapeDtypeStruct((n_rows * ROW_TILE, LANES), hn_rt.dtype),
        scratch_shapes=[pltpu.SemaphoreType.DMA(())],
        input_output_aliases={2: 0},
        compiler_params=_params(("arbitrary",)),
        name="moe_scatter",
    )(pos, hn_rt, xs0)


def _expert_kernel(te_ref, nu_ref, xs_ref, w1_ref, w3_ref, w2_ref, o_ref):
    i = pl.program_id(0)

    @pl.when(i < nu_ref[0])
    def _():
        x = _load_row_tiles(xs_ref).astype(BF16)
        up = _dot(x, w1_ref[...].astype(BF16))
        lin = _dot(x, w3_ref[...].astype(BF16))
        z = (up * _sigmoid(up)) * lin
        _store_row_tiles(o_ref, _dot(z.astype(BF16), w2_ref[...].astype(BF16)))

    @pl.when(i >= nu_ref[0])
    def _():
        o_ref[...] = jnp.zeros_like(o_ref)


def _experts(tile_expert, n_used, xs_rt, w1, w3, w2, tm):
    n_rows = xs_rt.shape[0] // ROW_TILE
    _, d, f = w1.shape
    last = lambda i, nu: jnp.minimum(i, jnp.maximum(nu[0] - 1, 0))
    return pl.pallas_call(
        _expert_kernel,
        grid_spec=pltpu.PrefetchScalarGridSpec(
            num_scalar_prefetch=2,
            grid=(n_rows // tm,),
            in_specs=[
                pl.BlockSpec((tm * ROW_TILE, LANES), lambda i, te, nu: (last(i, nu), 0)),
                pl.BlockSpec((None, d, f), lambda i, te, nu: (te[i], 0, 0)),
                pl.BlockSpec((None, d, f), lambda i, te, nu: (te[i], 0, 0)),
                pl.BlockSpec((None, f, d), lambda i, te, nu: (te[i], 0, 0)),
            ],
            out_specs=pl.BlockSpec((tm * ROW_TILE, LANES), lambda i, te, nu: (i, 0)),
        ),
        out_shape=jax.ShapeDtypeStruct((n_rows * ROW_TILE, LANES), F32),
        compiler_params=_params(("arbitrary",)),
        name="moe_experts",
    )(tile_expert, n_used, xs_rt, w1, w3, w2)


def _combine_kernel(pos_ref, pos_next_ref, rf_ref, x2_ref, gf_ref, o_ref, y_ref, buf1, buf2, sems):
    i = pl.program_id(0)
    tt = pos_ref.shape[1]
    slot = i % 2

    def gather(p_ref, s):
        def issue(t, carry):
            _row_copy(o_ref, p_ref[0, t], buf1.at[s], t, sems.at[s]).start(priority=0)
            _row_copy(o_ref, p_ref[1, t], buf2.at[s], t, sems.at[s]).start(priority=1)
            return carry

        lax.fori_loop(0, tt, issue, 0, unroll=8)

    @pl.when(i == 0)
    def _():
        gather(pos_ref, 0)

    @pl.when(i + 1 < pl.num_programs(0))
    def _():
        gather(pos_next_ref, 1 - slot)

    def drain(t, carry):
        _row_copy(o_ref, 0, buf1.at[slot], 0, sems.at[slot]).wait()
        _row_copy(o_ref, 0, buf2.at[slot], 0, sems.at[slot]).wait()
        return carry

    lax.fori_loop(0, tt, drain, 0, unroll=8)
    rf = rf_ref[...]
    y = (x2_ref[...] + rf[:, 0:1] * _load_row_tiles(buf1.at[slot])
         + rf[:, 1:2] * _load_row_tiles(buf2.at[slot]))
    y_ref[...] = _rmsnorm(y, gf_ref[...], NORM_EPS)


def _combine(pos, route_f, x2, gain_final, o_rt):
    n, d = x2.shape
    tt = min(256, n)
    steps = n // tt
    return pl.pallas_call(
        _combine_kernel,
        grid=(steps,),
        in_specs=[
            pl.BlockSpec((ROUTE_ROWS, tt), lambda i: (0, i), memory_space=pltpu.SMEM),
            pl.BlockSpec((ROUTE_ROWS, tt), lambda i: (0, jnp.minimum(i + 1, steps - 1)), memory_space=pltpu.SMEM),
            pl.BlockSpec((tt, LANES), lambda i: (i, 0)),
            pl.BlockSpec((tt, d), lambda i: (i, 0)),
            pl.BlockSpec((1, d), lambda i: (0, 0)),
            pl.BlockSpec(memory_space=pl.ANY),
        ],
        out_specs=pl.BlockSpec((tt, d), lambda i: (i, 0)),
        out_shape=jax.ShapeDtypeStruct((n, d), F32),
        scratch_shapes=[pltpu.VMEM((2, tt * ROW_TILE, LANES), F32), pltpu.VMEM((2, tt * ROW_TILE, LANES), F32),
                        pltpu.SemaphoreType.DMA((2,))],
        compiler_params=_params(("arbitrary",)),
        name="moe_combine",
    )(pos, pos, route_f, x2, gain_final, o_rt)


def _moe(hn_rt, route_f, route_t, w1, w3, w2, x2, gain_final):
    n = hn_rt.shape[0] // ROW_TILE
    ne = w1.shape[0]
    ne_pad = -(-ne // 8) * 8
    tm = EXPERT_TILE if 2 * n >= 2 * ne * EXPERT_TILE else EXPERT_TILE // 4
    n_tiles = -(-2 * n // tm) + ne
    route_i, counts = _rank(route_t, ne_pad)
    counts = counts[:ne, 0].astype(jnp.int32)
    tiles = (counts + tm - 1) // tm
    ends = jnp.cumsum(tiles)
    starts = (ends - tiles) * tm
    n_used = ends[-1:]
    tile_expert = jnp.minimum(jnp.sum(jnp.arange(n_tiles)[:, None] >= ends[None, :], axis=1), ne - 1)
    pos = _positions(starts.astype(jnp.int32), route_i, ne)
    xs_rt = _scatter(pos, hn_rt, n_tiles * tm)
    o_rt = _experts(tile_expert.astype(jnp.int32), n_used.astype(jnp.int32), xs_rt, w1, w3, w2, tm)
    return _combine(pos, route_f, x2, gain_final, o_rt)


def _run_path(x, prev_row, s0, cache, wts, dims):
    b, t, d = x.shape
    x2d = x.reshape(b * t, d)
    q_scale = dims["da"] ** -0.5 * math.log2(math.e)
    gates = _gate_proj(x2d, wts["norm_mix"], wts["w_gates"])

    if cache is None:
        qkv, kf, vf, qt, vt = _qkv_proj(x2d, wts["norm_mix"], wts["w_qkv"], q_scale, True)
        qkv3 = qkv.reshape(b, t, -1)
        tiles = lambda a: a.reshape(b, t // a.shape[2], a.shape[1], a.shape[2])
        o_a = _attn_prompt(qkv3, tiles(qt), tiles(vt), wts["lams"], wts["subln_w"].reshape(-1, 1),
                           n_heads=dims["ha"], chunk=dims["chunk"], lam_init=dims["lam_init"])
    else:
        qkv, kf, vf = _qkv_proj(x2d, wts["norm_mix"], wts["w_qkv"], q_scale, False)
        qkv3 = qkv.reshape(b, t, -1)
        o_a = _attn_sample(qkv3, cache[0], cache[1], wts["lams"], wts["subln_w"], n_heads=dims["ha"],
                           lam_init=dims["lam_init"])

    *prep, shift_new = _rwkv_prep(x, wts["norm_mix"], wts["w_pb"], prev_row, wts["mu"], wts["w0"], wts["w2p"],
                                  wts["a0"], wts["a2p"], wts["g2"], wts["k_k"], wts["k_a"], wts["r_k"],
                                  wb=dims["wb"])
    o_b, s_new = _rwkv_chunks(prep, _state_to_blockdiag(s0), wts["lnx_w"], wts["lnx_b"])

    x2, hn, route_f, route_t = _mix(x2d, o_a.reshape(b * t, -1), o_b, gates, wts["w_a_out"], wts["w_b_out"],
                                    wts["w_o"], wts["norm_ffn"], wts["w_route"], wts["b_route"],
                                    n_groups=dims["n_groups"], n_per_group=dims["n_per_group"])
    y = _moe(hn, route_f, route_t, wts["moe_w1"], wts["moe_w3"], wts["moe_w2"], x2, wts["norm_final"])

    ha = dims["ha"]
    return (y.reshape(b, t, d), kf.reshape(1, b, t, ha, -1), vf.reshape(1, b, t, ha, -1),
            shift_new[None], _state_from_blockdiag(s_new)[None])


def kernel(x_prompt, x_sample, cache_attn_k, cache_attn_v, state_rwkv_shift, state_rwkv_wkv, norm_mix, w_in, lambda_q1, lambda_k1, lambda_q2, lambda_k2, subln_w, w_a_out, rwkv_mu, rwkv_w0, rwkv_w2, rwkv_a0, rwkv_a2, rwkv_g2, rwkv_k_k, rwkv_k_a, rwkv_r_k, rwkv_lnx_w, rwkv_lnx_b, w_b_out, w_o, norm_ffn, moe_w_group, moe_b_group, moe_w_router, moe_b_router, moe_w1, moe_w3, moe_w2, norm_final):
    assert w_in.shape[0] == 1, "single-layer trunk"
    l = 0
    d = x_prompt.shape[-1]
    ha, dva = cache_attn_k.shape[3], cache_attn_v.shape[4]
    wa = ha * dva
    hb, db = state_rwkv_wkv.shape[2], state_rwkv_wkv.shape[3]
    wb = hb * db
    lora_w, lora_a, lora_g = rwkv_w2.shape[1], rwkv_a2.shape[1], rwkv_g2.shape[1]
    rwkv_cols = 3 * wb + lora_w + lora_a + lora_g
    n_groups = moe_w_group.shape[-1]
    n_per_group = moe_w_router.shape[-1] // n_groups
    ne = n_groups * n_per_group
    assert db == HEAD_B and lora_w + lora_a == LANES and lora_g == LANES and dva == LANES
    assert ne + n_groups <= LANES

    row = lambda v: v.reshape(1, -1).astype(F32)
    w_l = w_in[l]
    zeros_w = jnp.zeros((lora_a, wb), F32)
    zeros_a = jnp.zeros((lora_w, wb), F32)
    w_route = jnp.zeros((d, LANES), F32)
    w_route = w_route.at[:, :ne].set(moe_w_router[l]).at[:, ne:ne + n_groups].set(moe_w_group[l])
    b_route = jnp.zeros((1, LANES), F32)
    b_route = b_route.at[0, :ne].set(moe_b_router[l]).at[0, ne:ne + n_groups].set(moe_b_group[l])
    f = moe_w1.shape[-1]
    wts = dict(
        norm_mix=row(norm_mix[l]),
        w_qkv=w_l[:, :3 * wa].astype(BF16),
        w_pb=w_l[:, 3 * wa:3 * wa + rwkv_cols].astype(BF16),
        w_gates=w_l[:, 3 * wa + rwkv_cols:].astype(BF16),
        lams=[row(lambda_q1[l]), row(lambda_k1[l]), row(lambda_q2[l]), row(lambda_k2[l])],
        subln_w=row(subln_w[l]),
        mu=row(rwkv_mu[l]), w0=row(rwkv_w0[l]), a0=row(rwkv_a0[l]),
        w2p=jnp.concatenate([rwkv_w2[l], zeros_w], axis=0),
        a2p=jnp.concatenate([zeros_a, rwkv_a2[l]], axis=0),
        g2=rwkv_g2[l].astype(F32),
        k_k=row(rwkv_k_k[l]), k_a=row(rwkv_k_a[l]), r_k=row(rwkv_r_k[l]),
        lnx_w=row(rwkv_lnx_w[l]), lnx_b=row(rwkv_lnx_b[l]),
        w_a_out=w_a_out[l].astype(BF16), w_b_out=w_b_out[l].astype(BF16), w_o=w_o[l].astype(BF16),
        norm_ffn=row(norm_ffn[l]), w_route=w_route, b_route=b_route,
        moe_w1=moe_w1[l].reshape(ne, d, f), moe_w3=moe_w3[l].reshape(ne, d, f), moe_w2=moe_w2[l].reshape(ne, f, d),
        norm_final=row(norm_final),
    )
    dims = dict(ha=ha, da=dva // 2, wb=wb, chunk=CHUNK_B, lam_init=0.8 - 0.6 * math.exp(-0.3 * l),
                n_groups=n_groups, n_per_group=n_per_group)

    bp = x_prompt.shape[0]
    yp, kp, vp, shp, wkp = _run_path(
        x_prompt.astype(F32), jnp.zeros((bp, 1, rwkv_cols), F32), jnp.zeros((bp, hb, db, db), F32), None, wts, dims)

    bs, past = cache_attn_k.shape[1], cache_attn_k.shape[2]
    cache = (cache_attn_k[l].reshape(bs, past * ha, dva), cache_attn_v[l].reshape(bs, past * ha, dva))
    ys, ks_, vs_, shs, wks = _run_path(
        x_sample.astype(F32), state_rwkv_shift[l].astype(F32), state_rwkv_wkv[l].astype(F32), cache, wts, dims)

    return (yp, ys, kp, vp, shp, wkp, ks_, vs_, shs, wks)
```

```python
import functools
import math

import jax
import jax.numpy as jnp
from jax import lax
from jax.experimental import pallas as pl
from jax.experimental.pallas import tpu as pltpu

F32 = jnp.float32
BF16 = jnp.bfloat16

LANES = 128
HEAD_B = 64
CHUNK_B = 64
CHUNKS_PER_STEP = 8
ATTN_TILE = 512
ROW_TILE = 8
ROUTE_ROWS = 8
EXPERT_TILE = 512
NORM_EPS = 1e-6
DIFF_EPS = 1e-5
GN_EPS = 64e-5
NEG_INF = -1e30
VMEM_LIMIT = 56 * 1024 * 1024

NN = (((1,), (0,)), ((), ()))
NT = (((1,), (1,)), ((), ()))
TN = (((0,), (0,)), ((), ()))


def _dot(a, b, dims=NN):
    return lax.dot_general(a, b, dims, preferred_element_type=F32)


def _split2(a):
    hi = a.astype(BF16)
    lo = (a - hi.astype(F32)).astype(BF16)
    return hi, lo


def _split3(a):
    hi = a.astype(BF16)
    r = a - hi.astype(F32)
    mid = r.astype(BF16)
    lo = (r - mid.astype(F32)).astype(BF16)
    return hi, mid, lo


def _mm(a, b, passes=1, dims=NN):
    if passes == 1:
        return _dot(a.astype(BF16), b.astype(BF16), dims)
    a_hi, a_lo = _split2(a)
    if passes == 2:
        b_hi = b.astype(BF16)
        return _dot(a_hi, b_hi, dims) + _dot(a_lo, b_hi, dims)
    b_hi, b_lo = _split2(b)
    return _dot(a_hi, b_hi, dims) + (_dot(a_hi, b_lo, dims) + _dot(a_lo, b_hi, dims))


def _mm_exact_rhs2(a, b_bf16, dims=NN):
    hi, lo = _split2(a)
    return _dot(hi, b_bf16, dims) + _dot(lo, b_bf16, dims)


def _rmsnorm(x, g, eps):
    return x * lax.rsqrt(jnp.mean(x * x, axis=-1, keepdims=True) + eps) * g


def _sigmoid(x):
    return 1.0 / (1.0 + jnp.exp(-x))


def _params(sem):
    return pltpu.CompilerParams(dimension_semantics=sem, vmem_limit_bytes=VMEM_LIMIT)


def _qkv_kernel(x_ref, g_ref, w_ref, qkv_ref, kf_ref, vf_ref, *rest, q_scale, transposed):
    if transposed:
        qt_ref, vt_ref, h_ref = rest
    else:
        (h_ref,) = rest
    j = pl.program_id(1)

    @pl.when(j == 0)
    def _():
        h_ref[...] = _rmsnorm(x_ref[...], g_ref[...], NORM_EPS).astype(BF16)

    p = _dot(h_ref[...], w_ref[...])

    @pl.when(j == 0)
    def _():
        q = p * q_scale
        qkv_ref[...] = q.astype(BF16)
        if transposed:
            qt_ref[...] = q.T.astype(BF16)

    @pl.when(j == 1)
    def _():
        qkv_ref[...] = p.astype(BF16)
        kf_ref[...] = p

    @pl.when(j == 2)
    def _():
        qkv_ref[...] = p.astype(BF16)
        vf_ref[...] = p
        if transposed:
            vt_ref[...] = p.T.astype(BF16)


def _qkv_proj(x2d, gain, w_qkv, q_scale, transposed):
    n, d = x2d.shape
    wa = w_qkv.shape[1] // 3
    tm = min(ATTN_TILE, n)
    out_specs = [
        pl.BlockSpec((tm, wa), lambda i, j: (i, j)),
        pl.BlockSpec((tm, wa), lambda i, j: (i, 0)),
        pl.BlockSpec((tm, wa), lambda i, j: (i, 0)),
    ]
    out_shape = [
        jax.ShapeDtypeStruct((n, 3 * wa), BF16),
        jax.ShapeDtypeStruct((n, wa), F32),
        jax.ShapeDtypeStruct((n, wa), F32),
    ]
    if transposed:
        out_specs += [pl.BlockSpec((None, wa, tm), lambda i, j: (i, 0, 0))] * 2
        out_shape += [jax.ShapeDtypeStruct((n // tm, wa, tm), BF16)] * 2
    return pl.pallas_call(
        functools.partial(_qkv_kernel, q_scale=q_scale, transposed=transposed),
        grid=(n // tm, 3),
        in_specs=[
            pl.BlockSpec((tm, d), lambda i, j: (i, 0)),
            pl.BlockSpec((1, d), lambda i, j: (0, 0)),
            pl.BlockSpec((d, wa), lambda i, j: (0, j)),
        ],
        out_specs=out_specs,
        out_shape=out_shape,
        scratch_shapes=[pltpu.VMEM((tm, d), BF16)],
        compiler_params=_params(("parallel", "arbitrary")),
        name="qkv_proj",
    )(x2d, gain, w_qkv)


def _gate_kernel(x_ref, g_ref, w_ref, o_ref):
    h = _rmsnorm(x_ref[...], g_ref[...], NORM_EPS).astype(BF16)
    o_ref[...] = _sigmoid(_dot(h, w_ref[...])).astype(o_ref.dtype)


def _gate_proj(x2d, gain, w):
    n, d = x2d.shape
    c = w.shape[1]
    tm = min(512, n)
    return pl.pallas_call(
        _gate_kernel,
        grid=(n // tm,),
        in_specs=[
            pl.BlockSpec((tm, d), lambda i: (i, 0)),
            pl.BlockSpec((1, d), lambda i: (0, 0)),
            pl.BlockSpec((d, c), lambda i: (0, 0)),
        ],
        out_specs=pl.BlockSpec((tm, c), lambda i: (i, 0)),
        out_shape=jax.ShapeDtypeStruct((n, c), BF16),
        compiler_params=_params(("parallel",)),
        name="gate_proj",
    )(x2d, gain, w)


def _lambda(lq1_ref, lk1_ref, lq2_ref, lk2_ref, lam_init):
    s1 = jnp.sum(lq1_ref[...] * lk1_ref[...], axis=-1, keepdims=True)
    s2 = jnp.sum(lq2_ref[...] * lk2_ref[...], axis=-1, keepdims=True)
    return jnp.exp(s1) - jnp.exp(s2) + lam_init


def _split_q(q):
    lane = lax.broadcasted_iota(jnp.int32, (1, q.shape[1]), 1)
    first = lane < (q.shape[1] // 2)
    zero = jnp.zeros_like(q)
    return jnp.where(first, q, zero), jnp.where(first, zero, q)


def _attn_prompt_kernel(lq1_ref, lk1_ref, lq2_ref, lk2_ref, subc_ref, qt_ref, k_ref, vt_ref, o_ref,
                        s0_ref, s1_ref, s2_ref, s3_ref, mt_ref, m_ref, l_ref, acc_ref, *, chunk, lam_init):
    i = pl.program_id(2)
    dv, tq = qt_ref.shape
    tk = vt_ref.shape[2]
    lam = _lambda(lq1_ref, lk1_ref, lq2_ref, lk2_ref, lam_init)
    qt = qt_ref[...]
    feat = lax.broadcasted_iota(jnp.int32, (dv, 1), 0)
    zq = jnp.zeros_like(qt)
    q_maps = (jnp.where(feat < dv // 2, qt, zq), jnp.where(feat < dv // 2, zq, qt))
    acc_ref[...] = jnp.zeros_like(acc_ref)
    rb = 64
    kb = min(256, tk)

    m_ref[...] = jnp.full(m_ref.shape, NEG_INF, F32)
    l_ref[...] = jnp.zeros_like(l_ref)
    buf_a, buf_b = (s0_ref, s1_ref), (s2_ref, s3_ref)

    def scores(j, buf, masked):
        ks = k_ref[pl.ds(pl.multiple_of(j * tk, tk), tk), :]
        for mi in range(2):
            s = _dot(ks, q_maps[mi])
            if masked:
                kc = lax.broadcasted_iota(jnp.int32, (tk, tq), 0) // chunk
                qc = lax.broadcasted_iota(jnp.int32, (tk, tq), 1) // chunk
                s = jnp.where(kc <= qc, s, NEG_INF)
            buf[mi][...] = s
            mt_ref[2 * (buf is buf_b) + mi] = jnp.max(s, axis=0, keepdims=True)

    def softmax_pv(j, buf):
        vt = vt_ref[j]
        m_old = [m_ref[mi] for mi in range(2)]
        m_new = [jnp.maximum(m_old[mi], mt_ref[2 * (buf is buf_b) + mi]) for mi in range(2)]
        alpha = [jnp.exp2(m_old[mi] - m_new[mi]) for mi in range(2)]
        lsum = [jnp.zeros((8, tq), F32) for _ in range(2)]
        pv = [None, None]
        for hf in range(tk // kb):
            for mi in range(2):
                blocks = []
                for r in range(kb // rb):
                    lo = hf * kb + r * rb
                    p = jnp.exp2(buf[mi][lo:lo + rb, :] - m_new[mi])
                    lsum[mi] = lsum[mi] + jnp.sum(p.reshape(rb // 8, 8, tq), axis=0)
                    blocks.append(p.astype(BF16))
                part = _dot(vt[:, hf * kb:(hf + 1) * kb], jnp.concatenate(blocks, axis=0))
                pv[mi] = part if pv[mi] is None else pv[mi] + part
        for mi in range(2):
            acc_ref[mi] = acc_ref[mi] * alpha[mi] + pv[mi]
            m_ref[mi] = m_new[mi]
            l_ref[mi] = alpha[mi] * l_ref[mi] + lsum[mi]

    def stage_even(j, next_masked):
        scores(j + 1, buf_b, next_masked)
        softmax_pv(j, buf_a)

    def stage_odd(j, next_masked):
        scores(j + 1, buf_a, next_masked)
        softmax_pv(j, buf_b)

    @pl.when(i == 0)
    def _():
        scores(0, buf_a, True)
        softmax_pv(0, buf_a)

    @pl.when(i > 0)
    def _():
        scores(0, buf_a, False)

    def pair(jj, carry):
        stage_even(2 * jj, False)
        stage_odd(2 * jj + 1, False)
        return carry

    def quad(jj, carry):
        pair(2 * jj, carry)
        return pair(2 * jj + 1, carry)

    n_pairs = jnp.maximum(i - 1, 0) // 2
    lax.fori_loop(0, n_pairs // 2, quad, 0)
    lax.fori_loop(2 * (n_pairs // 2), n_pairs, pair, 0)

    @pl.when(i % 2 == 1)
    def _():
        stage_even(i - 1, True)
        softmax_pv(i, buf_b)

    @pl.when((i % 2 == 0) & (i > 0))
    def _():
        stage_even(i - 2, False)
        stage_odd(i - 1, True)
        softmax_pv(i, buf_a)

    l1 = jnp.sum(l_ref[0], axis=0, keepdims=True)
    l2 = jnp.sum(l_ref[1], axis=0, keepdims=True)
    ot = acc_ref[0] / l1 - lam * (acc_ref[1] / l2)
    ms = jnp.mean(ot * ot, axis=0, keepdims=True)
    ot = ot * lax.rsqrt(ms + DIFF_EPS) * subc_ref[...] * (1.0 - lam_init)
    o_ref[...] = ot.T.astype(o_ref.dtype)


def _attn_prompt(qkv, qt, vt, lams, sub_col, *, n_heads, chunk, lam_init):
    b, t, c3 = qkv.shape
    wa = c3 // 3
    dv = wa // n_heads
    nk, tile = qt.shape[1], qt.shape[3]
    lam_specs = [pl.BlockSpec((1, lams[0].shape[1]), lambda bb, h, i: (0, 0)) for _ in range(4)]
    return pl.pallas_call(
        functools.partial(_attn_prompt_kernel, chunk=chunk, lam_init=lam_init),
        grid=(b, n_heads, nk),
        in_specs=lam_specs + [
            pl.BlockSpec((dv, 1), lambda bb, h, i: (0, 0)),
            pl.BlockSpec((None, None, dv, tile), lambda bb, h, i: (bb, i, h, 0)),
            pl.BlockSpec((None, t, dv), lambda bb, h, i: (bb, 0, n_heads + h)),
            pl.BlockSpec((None, nk, dv, tile), lambda bb, h, i: (bb, 0, h, 0)),
        ],
        out_specs=pl.BlockSpec((None, tile, dv), lambda bb, h, i: (bb, i, h)),
        out_shape=jax.ShapeDtypeStruct((b, t, wa), BF16),
        scratch_shapes=[pltpu.VMEM((tile, tile), F32)] * 4
        + [pltpu.VMEM((4, 1, tile), F32),
           pltpu.VMEM((2, 1, tile), F32),
           pltpu.VMEM((2, 8, tile), F32),
           pltpu.VMEM((2, dv, tile), F32)],
        compiler_params=_params(("parallel", "parallel", "arbitrary")),
        name="diff_attn_prompt",
    )(*lams, sub_col, qt, qkv, vt)


def _attn_sample_kernel(lq1_ref, lk1_ref, lq2_ref, lk2_ref, sub_ref, q_ref, kn_ref, vn_ref, ck_ref, cv_ref,
                        o_ref, *, n_heads, lam_init):
    lam = _lambda(lq1_ref, lk1_ref, lq2_ref, lk2_ref, lam_init)
    tq = q_ref.shape[0]
    dv = q_ref.shape[1] // n_heads
    past = ck_ref.shape[0] // n_heads
    ones_c = jnp.ones((past, dv), BF16)
    ones_n = jnp.ones((tq, dv), BF16)
    for h in range(n_heads):
        cols = slice(h * dv, (h + 1) * dv)
        q1, q2 = _split_q(q_ref[:, cols].astype(F32))
        w = jnp.concatenate([q1, q2], axis=0).T.astype(BF16)
        ck = ck_ref[pl.ds(h, past, stride=n_heads), :].astype(BF16)
        cv = cv_ref[pl.ds(h, past, stride=n_heads), :].astype(BF16)
        s_c = _dot(ck, w)
        s_n = _dot(kn_ref[:, cols], w)
        m = jnp.maximum(jnp.max(s_c, axis=0, keepdims=True), jnp.max(s_n, axis=0, keepdims=True))
        p_c = jnp.exp2(s_c - m).astype(BF16)
        p_n = jnp.exp2(s_n - m).astype(BF16)
        acc = (_dot(p_c, jnp.concatenate([cv, ones_c], axis=1), TN)
               + _dot(p_n, jnp.concatenate([vn_ref[:, cols], ones_n], axis=1), TN))
        o1 = acc[:tq, :dv] / acc[:tq, dv:]
        o2 = acc[tq:, :dv] / acc[tq:, dv:]
        o = o1 - lam * o2
        o_ref[:, cols] = (_rmsnorm(o, sub_ref[...], DIFF_EPS) * (1.0 - lam_init)).astype(o_ref.dtype)


def _attn_sample(qkv, cache_k, cache_v, lams, sub_w, *, n_heads, lam_init):
    b, t, c3 = qkv.shape
    wa = c3 // 3
    dv = wa // n_heads
    rows = cache_k.shape[1]
    lam_specs = [pl.BlockSpec((1, lams[0].shape[1]), lambda bb: (0, 0)) for _ in range(4)]
    return pl.pallas_call(
        functools.partial(_attn_sample_kernel, n_heads=n_heads, lam_init=lam_init),
        grid=(b,),
        in_specs=lam_specs + [
            pl.BlockSpec((1, dv), lambda bb: (0, 0)),
            pl.BlockSpec((None, t, wa), lambda bb: (bb, 0, 0)),
            pl.BlockSpec((None, t, wa), lambda bb: (bb, 0, 1)),
            pl.BlockSpec((None, t, wa), lambda bb: (bb, 0, 2)),
            pl.BlockSpec((None, rows, dv), lambda bb: (bb, 0, 0)),
            pl.BlockSpec((None, rows, dv), lambda bb: (bb, 0, 0)),
        ],
        out_specs=pl.BlockSpec((None, t, wa), lambda bb: (bb, 0, 0)),
        out_shape=jax.ShapeDtypeStruct((b, t, wa), BF16),
        compiler_params=_params(("parallel",)),
        name="diff_attn_sample",
    )(*lams, sub_w, qkv, qkv, qkv, cache_k, cache_v)


def _seg_ones(n=LANES, seg=HEAD_B):
    r = lax.broadcasted_iota(jnp.int32, (n, n), 0) // seg
    c = lax.broadcasted_iota(jnp.int32, (n, n), 1) // seg
    return jnp.where(r == c, 1.0, 0.0).astype(BF16)


def _rwkv_prep_kernel(x_ref, gn_ref, wpb_ref, prev_ref, mu_ref, w0_ref, w2p_ref, a0_ref, a2p_ref, g2_ref, kk_ref,
                      ka_ref, rk_ref, r_o, k_o, v_o, ld_o, na_o, b_o, g_o, bon_o, shift_o, carry_ref, *, wb):
    i = pl.program_id(1)

    @pl.when(i == 0)
    def _():
        carry_ref[...] = prev_ref[...]

    h = _rmsnorm(x_ref[...], gn_ref[...], NORM_EPS).astype(BF16)
    pb = _dot(h, wpb_ref[...])
    shift_o[...] = pb[pb.shape[0] - 1:, :]
    tm = pb.shape[0]
    row = lax.broadcasted_iota(jnp.int32, (tm, 1), 0)
    prev = jnp.where(row == 0, carry_ref[...], pltpu.roll(pb, 1, axis=0))
    carry_ref[...] = pb[tm - 1:tm, :]
    xs = pb + (prev - pb) * mu_ref[...]

    n_slab = wb // LANES
    x_wa = xs[:, 3 * wb:3 * wb + LANES]
    x_g = xs[:, 3 * wb + LANES:3 * wb + 2 * LANES]
    lw = _mm(jnp.tanh(x_wa), w2p_ref[...], 2)
    la = _mm(x_wa, a2p_ref[...])
    g = _mm(_sigmoid(x_g), g2_ref[...])
    seg = _seg_ones(2 * LANES)

    for s in range(n_slab):
        sl = slice(s * LANES, (s + 1) * LANES)
        r = xs[:, s * LANES:(s + 1) * LANES]
        k = xs[:, wb + s * LANES:wb + (s + 1) * LANES]
        v = xs[:, 2 * wb + s * LANES:2 * wb + (s + 1) * LANES]
        z = -(w0_ref[:, sl] + lw[:, sl])
        softplus = jnp.maximum(z, 0.0) + jnp.log1p(jnp.exp(-jnp.abs(z)))
        w_log = -softplus - 0.5
        a = _sigmoid(a0_ref[:, sl] + la[:, sl])
        kk = k * kk_ref[:, sl]
        k2 = k * (1.0 + (a - 1.0) * ka_ref[:, sl])
        sums = _mm_exact_rhs2(jnp.concatenate([kk * kk, r * k2 * rk_ref[:, sl]], axis=1), seg)
        kk = kk / jnp.maximum(jnp.sqrt(sums[:, :LANES]), 1e-12)
        bonus = sums[:, LANES:] * v
        r_o[s] = r.astype(r_o.dtype)
        k_o[s] = k2.astype(k_o.dtype)
        v_o[s] = v.astype(v_o.dtype)
        ld_o[s] = -jnp.exp(w_log)
        na_o[s] = (-kk).astype(na_o.dtype)
        b_o[s] = (kk * a).astype(b_o.dtype)
        g_o[s] = g[:, sl].astype(g_o.dtype)
        bon_o[s] = bonus.astype(bon_o.dtype)


def _rwkv_prep(x, gain, w_pb, prev_row, mu, w0, w2p, a0, a2p, g2, k_k, k_a, r_k, *, wb):
    b, t, d = x.shape
    cols = w_pb.shape[1]
    tm = min(256, t)
    n_slab = wb // LANES
    vec = lambda n: pl.BlockSpec((1, n), lambda bb, i: (0, 0))
    full = lambda a: pl.BlockSpec(a.shape, lambda bb, i: (0, 0))
    out_spec = pl.BlockSpec((None, n_slab, tm, LANES), lambda bb, i: (bb, 0, i, 0))
    sds = lambda dt: jax.ShapeDtypeStruct((b, n_slab, t, LANES), dt)
    out_dtypes = [BF16, BF16, BF16, F32, BF16, BF16, BF16, BF16]
    return pl.pallas_call(
        functools.partial(_rwkv_prep_kernel, wb=wb),
        grid=(b, t // tm),
        in_specs=[
            pl.BlockSpec((None, tm, d), lambda bb, i: (bb, i, 0)),
            vec(d), full(w_pb),
            pl.BlockSpec((None, 1, cols), lambda bb, i: (bb, 0, 0)),
            vec(cols), vec(wb), full(w2p), vec(wb), full(a2p), full(g2), vec(wb), vec(wb), vec(wb),
        ],
        out_specs=[out_spec] * 8 + [pl.BlockSpec((None, 1, cols), lambda bb, i: (bb, 0, 0))],
        out_shape=[sds(dt) for dt in out_dtypes] + [jax.ShapeDtypeStruct((b, 1, cols), F32)],
        scratch_shapes=[pltpu.VMEM((1, cols), F32)],
        compiler_params=_params(("parallel", "arbitrary")),
        name="rwkv_prep",
    )(x, gain, w_pb, prev_row, mu, w0, w2p, a0, a2p, g2, k_k, k_a, r_k)


def _stack2(x, first):
    xb = x.astype(BF16)
    zero = jnp.zeros_like(xb)
    return jnp.concatenate([jnp.where(first, xb, zero), jnp.where(first, zero, xb)], axis=0)


def _rwkv_chunk_kernel(r_ref, k_ref, v_ref, ld_ref, na_ref, b_ref, g_ref, bon_ref, s0_ref, lnw_ref, lnb_ref,
                       y_ref, sT_ref, st_ref, *, cl):
    c = pl.program_id(1)
    n_chunks = pl.num_programs(1)
    n_slab = r_ref.shape[0]
    n_sub = r_ref.shape[1] // cl

    @pl.when(c == 0)
    def _():
        st_ref[...] = s0_ref[...]

    lane = lax.broadcasted_iota(jnp.int32, (1, LANES), 1)
    first = lane < HEAD_B
    ti = lax.broadcasted_iota(jnp.int32, (cl, cl), 0)
    tj = lax.broadcasted_iota(jnp.int32, (cl, cl), 1)
    tri_incl = jnp.where(ti >= tj, 1.0, 0.0).astype(BF16)
    seg = _seg_ones()
    t_i = lax.broadcasted_iota(jnp.int32, (cl, LANES), 0)
    s_i = lax.broadcasted_iota(jnp.int32, (cl, LANES), 1) % HEAD_B
    strict = t_i > s_i
    incl = t_i >= s_i
    zero = jnp.zeros((cl, LANES), F32)
    eye = jnp.where(t_i == s_i, 1.0, 0.0).astype(F32)
    inv_n = 1.0 / HEAD_B

    items = [(s, q) for q in range(n_sub) for s in range(n_slab)]
    rows = lambda q: slice(q * cl, (q + 1) * cl)

    pre = []
    for s, q in items:
        ld = ld_ref[s, rows(q), :]
        hi, mid, lo = _split3(ld)
        cum = _dot(tri_incl, hi) + (_dot(tri_incl, mid) + _dot(tri_incl, lo))
        pre.append((ld, cum))

    stacks = []
    for (s, q), (ld, cum) in zip(items, pre):
        cum_last = cum[cl - 1:cl, :]
        g_incl = jnp.exp(cum)
        g_excl = jnp.exp(cum - ld)
        g_inv = jnp.exp(-cum)
        g_end = jnp.exp(cum_last - cum)
        k = k_ref[s, rows(q), :]
        bv = b_ref[s, rows(q), :]
        stacks.append(dict(
            g_last=jnp.exp(cum_last),
            la=(na_ref[s, rows(q), :] * g_excl).astype(BF16),
            lr=(r_ref[s, rows(q), :] * g_incl).astype(BF16),
            l2=jnp.concatenate([_stack2(bv * g_inv, first), _stack2(k * g_inv, first)], axis=0),
            lh=jnp.concatenate([_stack2(bv * g_end, first), _stack2(k * g_end, first)], axis=0),
            v2=_stack2(v_ref[s, rows(q), :], first)))

    grams = [_dot(jnp.concatenate([d["la"], d["lr"]], axis=0), d["l2"], NT) for d in stacks]
    for d, gram in zip(stacks, grams):
        d["lar"] = jnp.concatenate([d["la"], d["lr"]], axis=0)
        d["g_akrk"] = jnp.concatenate([jnp.where(strict, gram[:cl, LANES:], zero),
                                       jnp.where(incl, gram[cl:, LANES:], zero)], axis=0).astype(BF16)
        d["g_rb"] = jnp.where(incl, gram[cl:, :LANES], zero).astype(BF16)
    n_lev = int(math.log2(cl))
    pows = [jnp.where(strict, gram[:cl, :LANES], zero) for gram in grams]
    invs = [eye + p for p in pows]
    pows = [_dot(p.astype(BF16), _stack2(p, first)) for p in pows]
    for lev in range(1, n_lev):
        last = lev == n_lev - 1
        lhs = [inv.astype(BF16) if last else jnp.concatenate([inv, p], axis=0).astype(BF16)
               for inv, p in zip(invs, pows)]
        prods = [_dot(a, _stack2(p, first)) for a, p in zip(lhs, pows)]
        invs = [inv + pr[:cl, :] for inv, pr in zip(invs, prods)]
        if not last:
            pows = [pr[cl:, :] for pr in prods]
    invs = [inv.astype(BF16) for inv in invs]
    akrk = [_dot(d["g_akrk"], d["v2"]) for d in stacks]

    sts = [st_ref[s] for s in range(n_slab)]
    for q in range(n_sub):
        base = q * n_slab
        stb = [st.astype(BF16) for st in sts]
        ars = [_dot(stacks[base + s]["lar"], stb[s], NT) + akrk[base + s] for s in range(n_slab)]
        us = [_stack2(_dot(invs[base + s], _stack2(ars[s][:cl, :], first)), first) for s in range(n_slab)]
        ys = [ars[s][cl:, :] + _dot(stacks[base + s]["g_rb"], us[s])
              for s in range(n_slab)]
        sts = [sts[s] * stacks[base + s]["g_last"]
               + _dot(jnp.concatenate([us[s], stacks[base + s]["v2"]], axis=0), stacks[base + s]["lh"], TN)
               for s in range(n_slab)]
        y_all = jnp.concatenate(ys, axis=0)
        d_all = y_all - _mm_exact_rhs2(y_all, seg) * inv_n
        var_all = _mm_exact_rhs2(d_all * d_all, seg) * inv_n
        yn_all = d_all * lax.rsqrt(var_all + GN_EPS)
        for s in range(n_slab):
            sl = slice(s * LANES, (s + 1) * LANES)
            yn = yn_all[s * cl:(s + 1) * cl, :] * lnw_ref[:, sl] + lnb_ref[:, sl]
            y_ref[s, rows(q), :] = ((yn + bon_ref[s, rows(q), :]) * g_ref[s, rows(q), :]).astype(y_ref.dtype)
    for s in range(n_slab):
        st_ref[s] = sts[s]

    @pl.when(c == n_chunks - 1)
    def _():
        sT_ref[...] = st_ref[...]


def _rwkv_chunks(prep, s0_bd, lnx_w, lnx_b):
    b, n_slab, t, _ = prep[0].shape
    cl = min(CHUNK_B, t)
    step = min(CHUNKS_PER_STEP * cl, t)
    wb = n_slab * LANES
    in_spec = pl.BlockSpec((None, n_slab, step, LANES), lambda bb, c: (bb, 0, c, 0))
    st_spec = pl.BlockSpec((None, n_slab, LANES, LANES), lambda bb, c: (bb, 0, 0, 0))
    vec = pl.BlockSpec((1, wb), lambda bb, c: (0, 0))
    return pl.pallas_call(
        functools.partial(_rwkv_chunk_kernel, cl=cl),
        grid=(b, t // step),
        in_specs=[in_spec] * 8 + [st_spec, vec, vec],
        out_specs=[in_spec, st_spec],
        out_shape=[
            jax.ShapeDtypeStruct((b, n_slab, t, LANES), BF16),
            jax.ShapeDtypeStruct((b, n_slab, LANES, LANES), F32),
        ],
        scratch_shapes=[pltpu.VMEM((n_slab, LANES, LANES), F32)],
        compiler_params=_params(("parallel", "arbitrary")),
        name="rwkv_chunks",
    )(*prep, s0_bd, lnx_w, lnx_b)


def _state_to_blockdiag(s):
    b, h, n, _ = s.shape
    s = s.reshape(b, h // 2, 2, n, n)
    z = jnp.zeros_like(s[:, :, 0])
    top = jnp.concatenate([s[:, :, 0], z], axis=-1)
    bot = jnp.concatenate([z, s[:, :, 1]], axis=-1)
    return jnp.concatenate([top, bot], axis=-2)


def _state_from_blockdiag(sbd):
    b, hs, n2, _ = sbd.shape
    n = n2 // 2
    return jnp.stack([sbd[:, :, :n, :n], sbd[:, :, n:, n:]], axis=2).reshape(b, 2 * hs, n, n)


def _route(logits, n_groups, n_per_group):
    ne = n_groups * n_per_group
    lane = lax.broadcasted_iota(jnp.int32, logits.shape, 1)
    big = jnp.int32(1 << 30)
    is_grp = (lane >= ne) & (lane < ne + n_groups)
    lg = jnp.where(is_grp, logits, NEG_INF)
    mg = jnp.max(lg, axis=-1, keepdims=True)
    gi = jnp.min(jnp.where(is_grp & (lg == mg), lane, big), axis=-1, keepdims=True) - ne
    eg = jnp.where(is_grp, jnp.exp(lg - mg), 0.0)
    p_sel = 1.0 / jnp.sum(eg, axis=-1, keepdims=True)
    in_grp = (lane < ne) & ((lane // n_per_group) == gi)
    le = jnp.where(in_grp, logits, NEG_INF)
    me = jnp.max(le, axis=-1, keepdims=True)
    ee = jnp.where(in_grp, jnp.exp(le - me), 0.0)
    pe = ee / jnp.sum(ee, axis=-1, keepdims=True)
    v1 = jnp.max(jnp.where(in_grp, pe, -1.0), axis=-1, keepdims=True)
    i1 = jnp.min(jnp.where(in_grp & (pe == v1), lane, big), axis=-1, keepdims=True)
    rest = in_grp & (lane != i1)
    v2 = jnp.max(jnp.where(rest, pe, -1.0), axis=-1, keepdims=True)
    i2 = jnp.min(jnp.where(rest & (pe == v2), lane, big), axis=-1, keepdims=True)
    tot = v1 + v2
    w1 = (v1 / tot) * p_sel
    w2 = (v2 / tot) * p_sel
    return (jnp.where(lane == 0, w1, 0.0) + jnp.where(lane == 1, w2, 0.0)
            + jnp.where(lane == 2, i1.astype(F32), 0.0) + jnp.where(lane == 3, i2.astype(F32), 0.0))


def _mix_kernel(x_ref, oa_ref, ob_ref, gt_ref, wa_ref, wb_ref, wo_ref, gn_ref, wr_ref, br_ref,
                x2_ref, hn_ref, route_ref, route_t_ref, *, n_groups, n_per_group):
    d = x_ref.shape[1]
    nb, n_slab, t_blk, _ = ob_ref.shape
    y_a = _dot(oa_ref[...], wa_ref[...])
    ob = jnp.concatenate([ob_ref[:, s].reshape(nb * t_blk, LANES) for s in range(n_slab)], axis=1)
    y_b = _dot(ob, wb_ref[...])
    merged = gt_ref[:, :d] * y_a + gt_ref[:, d:] * y_b
    x2 = x_ref[...] + _dot(merged.astype(BF16), wo_ref[...])
    x2_ref[...] = x2
    hn = _rmsnorm(x2, gn_ref[...], NORM_EPS)
    _store_row_tiles(hn_ref, hn)
    logits = _mm(hn, wr_ref[...], 3) + br_ref[...]
    route = _route(logits, n_groups, n_per_group)
    route_ref[...] = route
    route_t_ref[...] = route.T[:ROUTE_ROWS, :]


def _mix(x2d, o_a, o_b, gates, w_a, w_b, w_o, gain_ffn, w_route, b_route, *, n_groups, n_per_group):
    b, n_slab, t, _ = o_b.shape
    n, d = x2d.shape
    if t >= 512:
        nb, t_blk = 1, 512
    else:
        nb, t_blk = min(b, 512 // t), t
    tm = nb * t_blk
    nt = t // t_blk
    full = lambda a: pl.BlockSpec(a.shape, lambda i: (0,) * a.ndim)
    return pl.pallas_call(
        functools.partial(_mix_kernel, n_groups=n_groups, n_per_group=n_per_group),
        grid=(n // tm,),
        in_specs=[
            pl.BlockSpec((tm, d), lambda i: (i, 0)),
            pl.BlockSpec((tm, o_a.shape[1]), lambda i: (i, 0)),
            pl.BlockSpec((nb, n_slab, t_blk, LANES), lambda i: (i // nt, 0, i % nt, 0)),
            pl.BlockSpec((tm, 2 * d), lambda i: (i, 0)),
            full(w_a), full(w_b), full(w_o), full(gain_ffn), full(w_route), full(b_route),
        ],
        out_specs=[
            pl.BlockSpec((tm, d), lambda i: (i, 0)),
            pl.BlockSpec((tm * ROW_TILE, LANES), lambda i: (i, 0)),
            pl.BlockSpec((tm, LANES), lambda i: (i, 0)),
            pl.BlockSpec((ROUTE_ROWS, tm), lambda i: (0, i)),
        ],
        out_shape=[
            jax.ShapeDtypeStruct((n, d), F32),
            jax.ShapeDtypeStruct((n * ROW_TILE, LANES), F32),
            jax.ShapeDtypeStruct((n, LANES), F32),
            jax.ShapeDtypeStruct((ROUTE_ROWS, n), F32),
        ],
        compiler_params=_params(("parallel",)),
        name="mix_route",
    )(x2d, o_a, o_b, gates, w_a, w_b, w_o, gain_ffn, w_route, b_route)


def _rank_kernel(rt_ref, ri_ref, cnt_ref, carry_ref, *, ne_pad):
    i = pl.program_id(0)

    @pl.when(i == 0)
    def _():
        carry_ref[...] = jnp.zeros_like(carry_ref)

    rt = rt_ref[...]
    tt = rt.shape[1]
    e1 = rt[2:3, :].astype(jnp.int32)
    e2 = rt[3:4, :].astype(jnp.int32)
    eid = lax.broadcasted_iota(jnp.int32, (ne_pad, tt), 0)
    m1 = eid == e1
    m2 = eid == e2
    member = jnp.where(m1 | m2, 1.0, 0.0)
    before = jnp.where(lax.broadcasted_iota(jnp.int32, (tt, tt), 0) < lax.broadcasted_iota(jnp.int32, (tt, tt), 1),
                       1.0, 0.0).astype(BF16)
    carry = carry_ref[...]
    rank = _dot(member.astype(BF16), before) + carry[:, :1]
    r1 = jnp.sum(jnp.where(m1, rank, 0.0), axis=0, keepdims=True).astype(jnp.int32)
    r2 = jnp.sum(jnp.where(m2, rank, 0.0), axis=0, keepdims=True).astype(jnp.int32)
    ri_ref[...] = jnp.concatenate([e1, e2, r1, r2, jnp.zeros((ROUTE_ROWS - 4, tt), jnp.int32)], axis=0)
    carry = carry + jnp.sum(member, axis=1, keepdims=True)
    carry_ref[...] = carry

    @pl.when(i == pl.num_programs(0) - 1)
    def _():
        cnt_ref[...] = carry


def _rank(route_t, ne_pad):
    _, n = route_t.shape
    tt = min(512, n)
    return pl.pallas_call(
        functools.partial(_rank_kernel, ne_pad=ne_pad),
        grid=(n // tt,),
        in_specs=[pl.BlockSpec((ROUTE_ROWS, tt), lambda i: (0, i))],
        out_specs=[pl.BlockSpec((ROUTE_ROWS, tt), lambda i: (0, i)),
                   pl.BlockSpec((ne_pad, LANES), lambda i: (0, 0))],
        out_shape=[jax.ShapeDtypeStruct((ROUTE_ROWS, n), jnp.int32),
                   jax.ShapeDtypeStruct((ne_pad, LANES), F32)],
        scratch_shapes=[pltpu.VMEM((ne_pad, LANES), F32)],
        compiler_params=_params(("arbitrary",)),
        name="moe_rank",
    )(route_t)


def _store_row_tiles(ref2, x2d):
    rows, d = x2d.shape
    for j in range(d // LANES):
        ref2[pl.ds(j, rows, stride=d // LANES), :] = x2d[:, j * LANES:(j + 1) * LANES]


def _load_row_tiles(ref2):
    rows = ref2.shape[0] // ROW_TILE
    return jnp.concatenate([ref2[pl.ds(j, rows, stride=ROW_TILE), :] for j in range(ROW_TILE)], axis=1)


def _row_copy(src, src_row, dst, dst_row, sem):
    s0 = pl.multiple_of(src_row * ROW_TILE, ROW_TILE)
    d0 = pl.multiple_of(dst_row * ROW_TILE, ROW_TILE)
    return pltpu.make_async_copy(src.at[pl.ds(s0, ROW_TILE)], dst.at[pl.ds(d0, ROW_TILE)], sem)


def _pos_kernel(starts_ref, ri_ref, po_ref, *, ne):
    ri = ri_ref[...]
    e1, e2 = ri[0:1, :], ri[1:2, :]
    s1 = jnp.zeros_like(e1)
    s2 = jnp.zeros_like(e2)
    for e in range(ne):
        st = starts_ref[e]
        s1 = jnp.where(e1 == e, st, s1)
        s2 = jnp.where(e2 == e, st, s2)
    po_ref[...] = jnp.concatenate([s1 + ri[2:3, :], s2 + ri[3:4, :],
                                   jnp.zeros((ROUTE_ROWS - 2, ri.shape[1]), jnp.int32)], axis=0)


def _positions(starts, route_i, ne):
    _, n = route_i.shape
    tt = min(1024, n)
    assert n % tt == 0
    return pl.pallas_call(
        functools.partial(_pos_kernel, ne=ne),
        grid_spec=pltpu.PrefetchScalarGridSpec(
            num_scalar_prefetch=1,
            grid=(n // tt,),
            in_specs=[pl.BlockSpec((ROUTE_ROWS, tt), lambda i, st: (0, i))],
            out_specs=pl.BlockSpec((ROUTE_ROWS, tt), lambda i, st: (0, i)),
        ),
        out_shape=jax.ShapeDtypeStruct((ROUTE_ROWS, n), jnp.int32),
        compiler_params=_params(("arbitrary",)),
        name="moe_positions",
    )(starts, route_i)


def _scatter_kernel(pos_ref, hn_ref, xs_in_ref, xs_ref, sem):
    del xs_in_ref
    tt = pos_ref.shape[1]

    def issue(t, carry):
        _row_copy(hn_ref, t, xs_ref, pos_ref[0, t], sem).start(priority=0)
        _row_copy(hn_ref, t, xs_ref, pos_ref[1, t], sem).start(priority=1)
        return carry

    lax.fori_loop(0, tt, issue, 0, unroll=8)

    def drain(t, carry):
        _row_copy(hn_ref, 0, xs_ref, 0, sem).wait()
        _row_copy(hn_ref, 0, xs_ref, 0, sem).wait()
        return carry

    lax.fori_loop(0, tt, drain, 0, unroll=8)


def _scatter(pos, hn_rt, n_rows):
    n = hn_rt.shape[0] // ROW_TILE
    tt = min(512, n)
    xs0 = jnp.zeros((n_rows * ROW_TILE, LANES), hn_rt.dtype)
    return pl.pallas_call(
        _scatter_kernel,
        grid=(n // tt,),
        in_specs=[
            pl.BlockSpec((ROUTE_ROWS, tt), lambda i: (0, i), memory_space=pltpu.SMEM),
            pl.BlockSpec((tt * ROW_TILE, LANES), lambda i: (i, 0)),
            pl.BlockSpec(memory_space=pl.ANY),
        ],
        out_specs=pl.BlockSpec(memory_space=pl.ANY),
        out_shape=jax.ShapeDtypeStruct((n_rows * ROW_TILE, LANES), hn_rt.dtype),
        scratch_shapes=[pltpu.SemaphoreType.DMA(())],
        input_output_aliases={2: 0},
        compiler_params=_params(("arbitrary",)),
        name="moe_scatter",
    )(pos, hn_rt, xs0)


def _expert_kernel(te_ref, nu_ref, xs_ref, w1_ref, w3_ref, w2_ref, o_ref):
    i = pl.program_id(0)

    @pl.when(i < nu_ref[0])
    def _():
        x = _load_row_tiles(xs_ref).astype(BF16)
        up = _dot(x, w1_ref[...].astype(BF16))
        lin = _dot(x, w3_ref[...].astype(BF16))
        z = (up * _sigmoid(up)) * lin
        _store_row_tiles(o_ref, _dot(z.astype(BF16), w2_ref[...].astype(BF16)))

    @pl.when(i >= nu_ref[0])
    def _():
        o_ref[...] = jnp.zeros_like(o_ref)


def _experts(tile_expert, n_used, xs_rt, w1, w3, w2, tm):
    n_rows = xs_rt.shape[0] // ROW_TILE
    _, d, f = w1.shape
    last = lambda i, nu: jnp.minimum(i, jnp.maximum(nu[0] - 1, 0))
    return pl.pallas_call(
        _expert_kernel,
        grid_spec=pltpu.PrefetchScalarGridSpec(
            num_scalar_prefetch=2,
            grid=(n_rows // tm,),
            in_specs=[
                pl.BlockSpec((tm * ROW_TILE, LANES), lambda i, te, nu: (last(i, nu), 0)),
                pl.BlockSpec((None, d, f), lambda i, te, nu: (te[i], 0, 0)),
                pl.BlockSpec((None, d, f), lambda i, te, nu: (te[i], 0, 0)),
                pl.BlockSpec((None, f, d), lambda i, te, nu: (te[i], 0, 0)),
            ],
            out_specs=pl.BlockSpec((tm * ROW_TILE, LANES), lambda i, te, nu: (i, 0)),
        ),
        out_shape=jax.ShapeDtypeStruct((n_rows * ROW_TILE, LANES), F32),
        compiler_params=_params(("arbitrary",)),
        name="moe_experts",
    )(tile_expert, n_used, xs_rt, w1, w3, w2)


def _combine_kernel(pos_ref, pos_next_ref, rf_ref, x2_ref, gf_ref, o_ref, y_ref, buf1, buf2, sems):
    i = pl.program_id(0)
    tt = pos_ref.shape[1]
    slot = i % 2

    def gather(p_ref, s):
        def issue(t, carry):
            _row_copy(o_ref, p_ref[0, t], buf1.at[s], t, sems.at[s]).start(priority=0)
            _row_copy(o_ref, p_ref[1, t], buf2.at[s], t, sems.at[s]).start(priority=1)
            return carry

        lax.fori_loop(0, tt, issue, 0, unroll=8)

    @pl.when(i == 0)
    def _():
        gather(pos_ref, 0)

    @pl.when(i + 1 < pl.num_programs(0))
    def _():
        gather(pos_next_ref, 1 - slot)

    def drain(t, carry):
        _row_copy(o_ref, 0, buf1.at[slot], 0, sems.at[slot]).wait()
        _row_copy(o_ref, 0, buf2.at[slot], 0, sems.at[slot]).wait()
        return carry

    lax.fori_loop(0, tt, drain, 0, unroll=8)
    rf = rf_ref[...]
    y = (x2_ref[...] + rf[:, 0:1] * _load_row_tiles(buf1.at[slot])
         + rf[:, 1:2] * _load_row_tiles(buf2.at[slot]))
    y_ref[...] = _rmsnorm(y, gf_ref[...], NORM_EPS)


def _combine(pos, route_f, x2, gain_final, o_rt):
    n, d = x2.shape
    tt = min(512, n)
    steps = n // tt
    return pl.pallas_call(
        _combine_kernel,
        grid=(steps,),
        in_specs=[
            pl.BlockSpec((ROUTE_ROWS, tt), lambda i: (0, i), memory_space=pltpu.SMEM),
            pl.BlockSpec((ROUTE_ROWS, tt), lambda i: (0, jnp.minimum(i + 1, steps - 1)), memory_space=pltpu.SMEM),
            pl.BlockSpec((tt, LANES), lambda i: (i, 0)),
            pl.BlockSpec((tt, d), lambda i: (i, 0)),
            pl.BlockSpec((1, d), lambda i: (0, 0)),
            pl.BlockSpec(memory_space=pl.ANY),
        ],
        out_specs=pl.BlockSpec((tt, d), lambda i: (i, 0)),
        out_shape=jax.ShapeDtypeStruct((n, d), F32),
        scratch_shapes=[pltpu.VMEM((2, tt * ROW_TILE, LANES), F32), pltpu.VMEM((2, tt * ROW_TILE, LANES), F32),
                        pltpu.SemaphoreType.DMA((2,))],
        compiler_params=_params(("arbitrary",)),
        name="moe_combine",
    )(pos, pos, route_f, x2, gain_final, o_rt)


def _moe(hn_rt, route_f, route_t, w1, w3, w2, x2, gain_final):
    n = hn_rt.shape[0] // ROW_TILE
    ne = w1.shape[0]
    ne_pad = -(-ne // 8) * 8
    tm = EXPERT_TILE if 2 * n >= 2 * ne * EXPERT_TILE else EXPERT_TILE // 4
    n_tiles = -(-2 * n // tm) + ne
    route_i, counts = _rank(route_t, ne_pad)
    counts = counts[:ne, 0].astype(jnp.int32)
    tiles = (counts + tm - 1) // tm
    ends = jnp.cumsum(tiles)
    starts = (ends - tiles) * tm
    n_used = ends[-1:]
    tile_expert = jnp.minimum(jnp.sum(jnp.arange(n_tiles)[:, None] >= ends[None, :], axis=1), ne - 1)
    pos = _positions(starts.astype(jnp.int32), route_i, ne)
    xs_rt = _scatter(pos, hn_rt, n_tiles * tm)
    o_rt = _experts(tile_expert.astype(jnp.int32), n_used.astype(jnp.int32), xs_rt, w1, w3, w2, tm)
    return _combine(pos, route_f, x2, gain_final, o_rt)


def _run_path(x, prev_row, s0, cache, wts, dims):
    b, t, d = x.shape
    x2d = x.reshape(b * t, d)
    q_scale = dims["da"] ** -0.5 * math.log2(math.e)
    gates = _gate_proj(x2d, wts["norm_mix"], wts["w_gates"])

    if cache is None:
        qkv, kf, vf, qt, vt = _qkv_proj(x2d, wts["norm_mix"], wts["w_qkv"], q_scale, True)
        qkv3 = qkv.reshape(b, t, -1)
        tiles = lambda a: a.reshape(b, t // a.shape[2], a.shape[1], a.shape[2])
        o_a = _attn_prompt(qkv3, tiles(qt), tiles(vt), wts["lams"], wts["subln_w"].reshape(-1, 1),
                           n_heads=dims["ha"], chunk=dims["chunk"], lam_init=dims["lam_init"])
    else:
        qkv, kf, vf = _qkv_proj(x2d, wts["norm_mix"], wts["w_qkv"], q_scale, False)
        qkv3 = qkv.reshape(b, t, -1)
        o_a = _attn_sample(qkv3, cache[0], cache[1], wts["lams"], wts["subln_w"], n_heads=dims["ha"],
                           lam_init=dims["lam_init"])

    *prep, shift_new = _rwkv_prep(x, wts["norm_mix"], wts["w_pb"], prev_row, wts["mu"], wts["w0"], wts["w2p"],
                                  wts["a0"], wts["a2p"], wts["g2"], wts["k_k"], wts["k_a"], wts["r_k"],
                                  wb=dims["wb"])
    o_b, s_new = _rwkv_chunks(prep, _state_to_blockdiag(s0), wts["lnx_w"], wts["lnx_b"])

    x2, hn, route_f, route_t = _mix(x2d, o_a.reshape(b * t, -1), o_b, gates, wts["w_a_out"], wts["w_b_out"],
                                    wts["w_o"], wts["norm_ffn"], wts["w_route"], wts["b_route"],
                                    n_groups=dims["n_groups"], n_per_group=dims["n_per_group"])
    y = _moe(hn, route_f, route_t, wts["moe_w1"], wts["moe_w3"], wts["moe_w2"], x2, wts["norm_final"])

    ha = dims["ha"]
    return (y.reshape(b, t, d), kf.reshape(1, b, t, ha, -1), vf.reshape(1, b, t, ha, -1),
            shift_new[None], _state_from_blockdiag(s_new)[None])


def kernel(x_prompt, x_sample, cache_attn_k, cache_attn_v, state_rwkv_shift, state_rwkv_wkv, norm_mix, w_in, lambda_q1, lambda_k1, lambda_q2, lambda_k2, subln_w, w_a_out, rwkv_mu, rwkv_w0, rwkv_w2, rwkv_a0, rwkv_a2, rwkv_g2, rwkv_k_k, rwkv_k_a, rwkv_r_k, rwkv_lnx_w, rwkv_lnx_b, w_b_out, w_o, norm_ffn, moe_w_group, moe_b_group, moe_w_router, moe_b_router, moe_w1, moe_w3, moe_w2, norm_final):
    assert w_in.shape[0] == 1, "single-layer trunk"
    l = 0
    d = x_prompt.shape[-1]
    ha, dva = cache_attn_k.shape[3], cache_attn_v.shape[4]
    wa = ha * dva
    hb, db = state_rwkv_wkv.shape[2], state_rwkv_wkv.shape[3]
    wb = hb * db
    lora_w, lora_a, lora_g = rwkv_w2.shape[1], rwkv_a2.shape[1], rwkv_g2.shape[1]
    rwkv_cols = 3 * wb + lora_w + lora_a + lora_g
    n_groups = moe_w_group.shape[-1]
    n_per_group = moe_w_router.shape[-1] // n_groups
    ne = n_groups * n_per_group
    assert db == HEAD_B and lora_w + lora_a == LANES and lora_g == LANES and dva == LANES
    assert ne + n_groups <= LANES

    row = lambda v: v.reshape(1, -1).astype(F32)
    w_l = w_in[l]
    zeros_w = jnp.zeros((lora_a, wb), F32)
    zeros_a = jnp.zeros((lora_w, wb), F32)
    w_route = jnp.zeros((d, LANES), F32)
    w_route = w_route.at[:, :ne].set(moe_w_router[l]).at[:, ne:ne + n_groups].set(moe_w_group[l])
    b_route = jnp.zeros((1, LANES), F32)
    b_route = b_route.at[0, :ne].set(moe_b_router[l]).at[0, ne:ne + n_groups].set(moe_b_group[l])
    f = moe_w1.shape[-1]
    wts = dict(
        norm_mix=row(norm_mix[l]),
        w_qkv=w_l[:, :3 * wa].astype(BF16),
        w_pb=w_l[:, 3 * wa:3 * wa + rwkv_cols].astype(BF16),
        w_gates=w_l[:, 3 * wa + rwkv_cols:].astype(BF16),
        lams=[row(lambda_q1[l]), row(lambda_k1[l]), row(lambda_q2[l]), row(lambda_k2[l])],
        subln_w=row(subln_w[l]),
        mu=row(rwkv_mu[l]), w0=row(rwkv_w0[l]), a0=row(rwkv_a0[l]),
        w2p=jnp.concatenate([rwkv_w2[l], zeros_w], axis=0),
        a2p=jnp.concatenate([zeros_a, rwkv_a2[l]], axis=0),
        g2=rwkv_g2[l].astype(F32),
        k_k=row(rwkv_k_k[l]), k_a=row(rwkv_k_a[l]), r_k=row(rwkv_r_k[l]),
        lnx_w=row(rwkv_lnx_w[l]), lnx_b=row(rwkv_lnx_b[l]),
        w_a_out=w_a_out[l].astype(BF16), w_b_out=w_b_out[l].astype(BF16), w_o=w_o[l].astype(BF16),
        norm_ffn=row(norm_ffn[l]), w_route=w_route, b_route=b_route,
        moe_w1=moe_w1[l].reshape(ne, d, f), moe_w3=moe_w3[l].reshape(ne, d, f), moe_w2=moe_w2[l].reshape(ne, f, d),
        norm_final=row(norm_final),
    )
    dims = dict(ha=ha, da=dva // 2, wb=wb, chunk=CHUNK_B, lam_init=0.8 - 0.6 * math.exp(-0.3 * l),
                n_groups=n_groups, n_per_group=n_per_group)

    bp = x_prompt.shape[0]
    yp, kp, vp, shp, wkp = _run_path(
        x_prompt.astype(F32), jnp.zeros((bp, 1, rwkv_cols), F32), jnp.zeros((bp, hb, db, db), F32), None, wts, dims)

    bs, past = cache_attn_k.shape[1], cache_attn_k.shape[2]
    cache = (cache_attn_k[l].reshape(bs, past * ha, dva), cache_attn_v[l].reshape(bs, past * ha, dva))
    ys, ks_, vs_, shs, wks = _run_path(
        x_sample.astype(F32), state_rwkv_shift[l].astype(F32), state_rwkv_wkv[l].astype(F32), cache, wts, dims)

    return (yp, ys, kp, vp, shp, wkp, ks_, vs_, shs, wks)
```
